```python
import math
import jax, jax.numpy as jnp
from jax import lax
import numpy as np

D_MODEL = 1024
BATCH = 8
SEQ = 4096
DEPTH = 2

GRID_W = 64
CTX_LEN = 256
N_MIXERS = 2
N_HEADS = 16
N_KV_HEADS = 4
HEAD_DIM = D_MODEL // N_HEADS
GQA_GROUP = N_HEADS // N_KV_HEADS
Q_DIM = N_HEADS * HEAD_DIM
KV_DIM = N_KV_HEADS * HEAD_DIM
ROPE_AXIS_DIM = HEAD_DIM // 2
ROPE_THETA = 10000.0
ATTN_SCALE = HEAD_DIM ** -0.5
Q_BLOCK = 128
FOURIER_GROUPS = 4
N_GROUPS = 4
EXPERTS_PER_GROUP = 8
N_EXPERTS = N_GROUPS * EXPERTS_PER_GROUP
TOP_K_IN_GROUP = 2
D_EXPERT = D_MODEL // 2
MOE_BLOCK = 128
N_ATTN_LAYERS = (DEPTH + N_MIXERS - 1) // N_MIXERS
N_FOURIER_LAYERS = DEPTH // N_MIXERS
EPS = 1e-6

kernel_name = 'hybrid_attn_fourier_hmoe_dit'


def _rms_norm(x, gain):
    xf = x.astype(jnp.float32)
    y = xf * lax.rsqrt(jnp.mean(xf * xf, axis=-1, keepdims=True) + EPS)
    return y.astype(x.dtype) * gain


def _ada_params(silu_cond, w_mod, b_mod):
    return jnp.split(silu_cond @ w_mod + b_mod, 6, axis=-1)


def _modulate(x, gain, shift, scale):
    return _rms_norm(x, gain) * (1 + scale) + shift


def _axial_rope_tables(rows, dtype):
    half = ROPE_AXIS_DIM // 2
    inv = ROPE_THETA ** (-jnp.arange(half, dtype=jnp.float32) / half)
    ang_r = jnp.arange(rows, dtype=jnp.float32)[:, None] * inv
    ang_c = jnp.arange(GRID_W, dtype=jnp.float32)[:, None] * inv
    ang_r = jnp.broadcast_to(ang_r[:, None, :], (rows, GRID_W, half)).reshape(-1, half)
    ang_c = jnp.broadcast_to(ang_c[None, :, :], (rows, GRID_W, half)).reshape(-1, half)
    return (jnp.cos(ang_r).astype(dtype), jnp.sin(ang_r).astype(dtype),
            jnp.cos(ang_c).astype(dtype), jnp.sin(ang_c).astype(dtype))


def _rotate_half(x, cos, sin):
    shp = (cos.shape[0],) + (1,) * (x.ndim - 3) + (cos.shape[-1],)
    cos, sin = cos.reshape(shp), sin.reshape(shp)
    x1, x2 = jnp.split(x, 2, axis=-1)
    return jnp.concatenate([x1 * cos - x2 * sin, x2 * cos + x1 * sin], axis=-1)


def _apply_axial_rope(x, tables):
    cos_r, sin_r, cos_c, sin_c = tables
    return jnp.concatenate([_rotate_half(x[..., :ROPE_AXIS_DIM], cos_r, sin_r),
                            _rotate_half(x[..., ROPE_AXIS_DIM:], cos_c, sin_c)], axis=-1)


def _project_qkv(h, w_qkv, q_gain, k_gain, with_q):
    b, l, _ = h.shape
    if with_q:
        qkv = h @ w_qkv
        q = _rms_norm(qkv[..., :Q_DIM].reshape(b, l, N_KV_HEADS, GQA_GROUP, HEAD_DIM), q_gain)
        kv = qkv[..., Q_DIM:]
    else:
        q = None
        kv = h @ w_qkv[:, Q_DIM:]
    k = _rms_norm(kv[..., :KV_DIM].reshape(b, l, N_KV_HEADS, HEAD_DIM), k_gain)
    v = kv[..., KV_DIM:].reshape(b, l, N_KV_HEADS, HEAD_DIM)
    return q, k, v


def _attend(q, k, v):
    s = jnp.einsum('bqkgd,bskd->bkgqs', q, k).astype(jnp.float32) * ATTN_SCALE
    p = jax.nn.softmax(s, axis=-1).astype(v.dtype)
    return jnp.einsum('bkgqs,bskd->bqkgd', p, v)


def _gqa_attention(h_lat, h_ctx, w_qkv, q_gain, k_gain, w_o, rope, need_ctx_out):
    b, n, _ = h_lat.shape
    q_l, k_l, v_l = _project_qkv(h_lat, w_qkv, q_gain, k_gain, True)
    q_c, k_c, v_c = _project_qkv(h_ctx, w_qkv, q_gain, k_gain, need_ctx_out)
    q_l = _apply_axial_rope(q_l, rope)
    k_l = _apply_axial_rope(k_l, rope)
    k_all = jnp.concatenate([k_l, k_c], axis=1)
    v_all = jnp.concatenate([v_l, v_c], axis=1)
    n_blk = n // Q_BLOCK
    q_blocks = jnp.moveaxis(q_l.reshape(b, n_blk, Q_BLOCK, N_KV_HEADS, GQA_GROUP, HEAD_DIM), 1, 0)
    o = lax.map(lambda qb: _attend(qb, k_all, v_all), q_blocks)
    y_lat = jnp.moveaxis(o, 0, 1).reshape(b, n, Q_DIM) @ w_o
    y_ctx = None
    if need_ctx_out:
        y_ctx = _attend(q_c, k_c, v_c).reshape(b, h_ctx.shape[1], Q_DIM) @ w_o
    return y_lat, y_ctx


def _fourier_mix(h, w_o):
    b, n, d = h.shape
    hg = h.astype(jnp.float32).reshape(b, n, FOURIER_GROUPS, d // FOURIER_GROUPS)
    f = jnp.fft.fft2(hg, axes=(1, 3), norm='ortho').real
    return f.reshape(b, n, d).astype(h.dtype) @ w_o


def _hier_moe(h, w_rg, b_rg, w_re, b_re, w1, w3, w2):
    b, n, d = h.shape
    xt = h.reshape(-1, d)
    t = xt.shape[0]
    lg = (xt @ w_rg + b_rg).astype(jnp.float32)
    g_sel = jnp.argmax(lg, axis=-1)
    p_grp = jnp.max(jax.nn.softmax(lg, axis=-1), axis=-1)
    le = (xt @ w_re + b_re).astype(jnp.float32).reshape(t, N_GROUPS, EXPERTS_PER_GROUP)
    le_sel = jnp.take_along_axis(le, g_sel[:, None, None], axis=1)[:, 0]
    top_l, top_i = lax.top_k(le_sel, TOP_K_IN_GROUP)
    gate = p_grp[:, None] * jax.nn.softmax(top_l, axis=-1)
    eid = (g_sel[:, None] * EXPERTS_PER_GROUP + top_i).astype(jnp.int32)
    a = t * TOP_K_IN_GROUP
    flat_e = eid.reshape(-1)
    flat_t = jnp.arange(a, dtype=jnp.int32) // TOP_K_IN_GROUP
    flat_w = gate.reshape(-1).astype(h.dtype)
    order = jnp.argsort(flat_e)
    se = flat_e[order]
    counts = jnp.bincount(flat_e, length=N_EXPERTS)
    pcounts = (counts + MOE_BLOCK - 1) // MOE_BLOCK * MOE_BLOCK
    pend = jnp.cumsum(pcounts)
    pstart = pend - pcounts
    start = jnp.cumsum(counts) - counts
    dest = pstart[se] + jnp.arange(a, dtype=jnp.int32) - start[se]
    n_blk = (a + N_EXPERTS * (MOE_BLOCK - 1) + MOE_BLOCK - 1) // MOE_BLOCK
    p_rows = n_blk * MOE_BLOCK
    row_t = jnp.full((p_rows,), t, jnp.int32).at[dest].set(flat_t[order])
    row_w = jnp.zeros((p_rows,), h.dtype).at[dest].set(flat_w[order])
    blk_e = jnp.minimum(jnp.searchsorted(pend, jnp.arange(n_blk) * MOE_BLOCK, side='right'),
                        N_EXPERTS - 1).astype(jnp.int32)
    x_pad = jnp.concatenate([xt, jnp.zeros((1, d), xt.dtype)], axis=0)
    xb = x_pad[row_t].reshape(n_blk, MOE_BLOCK, d)

    def expert_block(args):
        xblk, e = args
        return (jax.nn.silu(xblk @ w1[e]) * (xblk @ w3[e])) @ w2[e]

    yb = lax.map(expert_block, (xb, blk_e)).reshape(p_rows, d)
    y = jax.ops.segment_sum(yb * row_w[:, None], row_t, num_segments=t + 1)[:t]
    return y.reshape(b, n, d)


def _context_needed_after(i):
    return any(j % N_MIXERS == 0 for j in range(i + 1, DEPTH))


def setup_inputs(seed: int = 0) -> dict:
    key = jax.random.key(seed)
    ks = jax.random.split(key, 21)
    nrm = lambda k, shape, s: jax.random.normal(k, shape, jnp.float32) * s
    return {
        'x': nrm(ks[0], (BATCH, SEQ, D_MODEL), 1.0),
        'c': nrm(ks[1], (BATCH, D_MODEL), 1.0),
        'ctx': nrm(ks[2], (BATCH, CTX_LEN, D_MODEL), 1.0),
        'c_ctx': nrm(ks[3], (D_MODEL,), 1.0),
        'w_mod': nrm(ks[4], (DEPTH, D_MODEL, 6 * D_MODEL), 0.5 * D_MODEL ** -0.5),
        'b_mod': nrm(ks[5], (DEPTH, 6 * D_MODEL), 0.02),
        'norm_mix': 1.0 + nrm(ks[6], (DEPTH, D_MODEL), 0.02),
        'norm_ffn': 1.0 + nrm(ks[7], (DEPTH, D_MODEL), 0.02),
        'attn_w_qkv': nrm(ks[8], (N_ATTN_LAYERS, D_MODEL, Q_DIM + 2 * KV_DIM), D_MODEL ** -0.5),
        'attn_q_norm': 1.0 + nrm(ks[9], (N_ATTN_LAYERS, HEAD_DIM), 0.02),
        'attn_k_norm': 1.0 + nrm(ks[10], (N_ATTN_LAYERS, HEAD_DIM), 0.02),
        'attn_w_o': nrm(ks[11], (N_ATTN_LAYERS, Q_DIM, D_MODEL), Q_DIM ** -0.5),
        'fourier_w_o': nrm(ks[12], (N_FOURIER_LAYERS, D_MODEL, D_MODEL), D_MODEL ** -0.5),
        'moe_w_rg': nrm(ks[13], (DEPTH, D_MODEL, N_GROUPS), D_MODEL ** -0.5),
        'moe_b_rg': nrm(ks[14], (DEPTH, N_GROUPS), 0.01),
        'moe_w_re': nrm(ks[15], (DEPTH, D_MODEL, N_EXPERTS), D_MODEL ** -0.5),
        'moe_b_re': nrm(ks[16], (DEPTH, N_EXPERTS), 0.01),
        'moe_w1': nrm(ks[17], (DEPTH, N_EXPERTS, D_MODEL, D_EXPERT), D_MODEL ** -0.5),
        'moe_w3': nrm(ks[18], (DEPTH, N_EXPERTS, D_MODEL, D_EXPERT), D_MODEL ** -0.5),
        'moe_w2': nrm(ks[19], (DEPTH, N_EXPERTS, D_EXPERT, D_MODEL), D_EXPERT ** -0.5),
        'final_norm': 1.0 + nrm(ks[20], (D_MODEL,), 0.02),
    }


def reference(x, c, ctx, c_ctx, w_mod, b_mod, norm_mix, norm_ffn, attn_w_qkv, attn_q_norm, attn_k_norm,
              attn_w_o, fourier_w_o, moe_w_rg, moe_b_rg, moe_w_re, moe_b_re, moe_w1, moe_w3, moe_w2,
              final_norm):
    n_tok = x.shape[1]
    rows = n_tok // GRID_W
    rope = _axial_rope_tables(rows, x.dtype)
    silu_c = jax.nn.silu(c)[:, None, :]
    silu_cc = jax.nn.silu(c_ctx)[None, None, :]
    s_ctx = ctx
    for i in range(DEPTH):
        kind = i % N_MIXERS
        j = i // N_MIXERS
        ctx_next = _context_needed_after(i)
        moe_p = (moe_w_rg[i], moe_b_rg[i], moe_w_re[i], moe_b_re[i], moe_w1[i], moe_w3[i], moe_w2[i])
        sh1, sc1, g1, sh2, sc2, g2 = _ada_params(silu_c, w_mod[i], b_mod[i])
        h = _modulate(x, norm_mix[i], sh1, sc1)
        if kind == 0 or ctx_next:
            csh1, csc1, cg1, csh2, csc2, cg2 = _ada_params(silu_cc, w_mod[i], b_mod[i])
            h_c = _modulate(s_ctx, norm_mix[i], csh1, csc1)
        if kind == 0:
            y, y_c = _gqa_attention(h, h_c, attn_w_qkv[j], attn_q_norm[j], attn_k_norm[j], attn_w_o[j],
                                    rope, ctx_next)
        else:
            y = _fourier_mix(h, fourier_w_o[j])
            y_c = _fourier_mix(h_c, fourier_w_o[j]) if ctx_next else None
        x = x + g1 * y
        x = x + g2 * _hier_moe(_modulate(x, norm_ffn[i], sh2, sc2), *moe_p)
        if ctx_next:
            s_ctx = s_ctx + cg1 * y_c
            s_ctx = s_ctx + cg2 * _hier_moe(_modulate(s_ctx, norm_ffn[i], csh2, csc2), *moe_p)
    return _rms_norm(x, final_norm)
```

```python
import functools
import math

import jax
import jax.numpy as jnp
from jax import lax
from jax.experimental import pallas as pl
from jax.experimental.pallas import tpu as pltpu

F32 = jnp.float32
BF16 = jnp.bfloat16

N_HEADS = 16
N_KV_HEADS = 4
HEAD_DIM = 64
GQA_GROUP = N_HEADS // N_KV_HEADS
GRID_W = 64
ROPE_AXIS_DIM = HEAD_DIM // 2
ROPE_THETA = 10000.0
FOURIER_GROUPS = 4
N_GROUPS = 4
EXPERTS_PER_GROUP = 8
N_EXPERTS = N_GROUPS * EXPERTS_PER_GROUP
EPS = 1e-6
LANES = 128
NEG_BIG = -1e30
VMEM_LIMIT = 48 * 1024 * 1024

FFN_BLOCK = 256
COMBINE_TILE = 256


def _dot(a, b):
    return jnp.dot(a, b, preferred_element_type=F32)


def _split_bf16(a):
    hi = a.astype(BF16)
    lo = (a - hi.astype(F32)).astype(BF16)
    return hi, lo


def _params(sem):
    return pltpu.CompilerParams(dimension_semantics=sem, vmem_limit_bytes=VMEM_LIMIT)


def _ada_kernel(c_ref, w_ref, b_ref, o_ref):
    c = c_ref[...]
    s = c / (1.0 + jnp.exp(-c))
    s_hi, s_lo = _split_bf16(s)
    w_hi, w_lo = _split_bf16(w_ref[0])
    o_ref[0] = _dot(s_hi, w_hi) + _dot(s_hi, w_lo) + _dot(s_lo, w_hi) + b_ref[0]


def _ada(cond, w_mod, b_mod):
    depth, d, d6 = w_mod.shape
    rows = cond.shape[0]
    tn = 1536
    return pl.pallas_call(
        _ada_kernel,
        grid=(depth, d6 // tn),
        in_specs=[
            pl.BlockSpec((rows, d), lambda l, j: (0, 0)),
            pl.BlockSpec((1, d, tn), lambda l, j: (l, 0, j)),
            pl.BlockSpec((1, 1, tn), lambda l, j: (l, 0, j)),
        ],
        out_specs=pl.BlockSpec((1, rows, tn), lambda l, j: (l, 0, j)),
        out_shape=jax.ShapeDtypeStruct((depth, rows, d6), F32),
        compiler_params=_params(("arbitrary", "arbitrary")),
        name="ada",
    )(cond, w_mod, b_mod.reshape(depth, 1, d6))


def _modulate(xf, gain, shift, scale):
    ms = jnp.mean(xf * xf, axis=-1, keepdims=True)
    return (xf * lax.rsqrt(ms + EPS)) * gain * (1.0 + scale) + shift


def _qkv_kernel(x_ref, sh_ref, sc_ref, gain_ref, w_ref, hg_ref, ebd_ref, cos_ref, sin_ref,
                *out_refs, with_q):
    h = _modulate(x_ref[0], gain_ref[...], sh_ref[0], sc_ref[0])
    qkv = _dot(h.astype(BF16), w_ref[...])
    tm = qkv.shape[0]
    lane = lax.broadcasted_iota(jnp.int32, (tm, LANES), 1)
    first_half = (lane & 16) == 0
    low = lane < HEAD_DIM
    if with_q:
        q_ref, k_ref, v_ref = out_refs
        nq = N_HEADS * HEAD_DIM // LANES
    else:
        k_ref, v_ref = out_refs
        nq = 0
    nk = N_KV_HEADS * HEAD_DIM // LANES
    cos = cos_ref[...]
    sin = sin_ref[...]

    def norm_rope(c, gain_row):
        ms = _dot((c * c).astype(BF16), ebd_ref[...])
        cn = c * lax.rsqrt(ms + EPS) * gain_row
        partner = jnp.where(first_half, pltpu.roll(cn, LANES - 16, 1), pltpu.roll(cn, 16, 1))
        return cn * cos + partner * sin

    for j in range(nq):
        out = norm_rope(qkv[:, j * LANES:(j + 1) * LANES], hg_ref[0:1, :]).astype(BF16)
        q_ref[0, 2 * j] = out[:, :HEAD_DIM]
        q_ref[0, 2 * j + 1] = out[:, HEAD_DIM:]
    for j in range(nk):
        out = norm_rope(qkv[:, (nq + j) * LANES:(nq + j + 1) * LANES], hg_ref[1:2, :]).astype(BF16)
        k_ref[0, 2 * j] = out[:, :HEAD_DIM]
        k_ref[0, 2 * j + 1] = out[:, HEAD_DIM:]
    for j in range(nk):
        c = qkv[:, (nq + nk + j) * LANES:(nq + nk + j + 1) * LANES]
        v_ref[0, 2 * j] = jnp.where(low, c, 1.0).astype(BF16)
        v_ref[0, 2 * j + 1] = jnp.where(low, pltpu.roll(c, HEAD_DIM, 1), 1.0).astype(BF16)


def _qkv(x, shift, scale, gain, w, head_gain, ebd, cos, sin, *, with_q, per_batch_mods):
    b, n, d = x.shape
    tm = min(512, n)
    ncol = w.shape[1]
    mod_idx = (lambda bi, i: (bi, 0, 0)) if per_batch_mods else (lambda bi, i: (0, 0, 0))
    out_shape = []
    out_specs = []
    if with_q:
        out_shape.append(jax.ShapeDtypeStruct((b, N_HEADS, n, HEAD_DIM), BF16))
        out_specs.append(pl.BlockSpec((1, N_HEADS, tm, HEAD_DIM), lambda bi, i: (bi, 0, i, 0)))
    out_shape.append(jax.ShapeDtypeStruct((b, N_KV_HEADS, n, HEAD_DIM), BF16))
    out_specs.append(pl.BlockSpec((1, N_KV_HEADS, tm, HEAD_DIM), lambda bi, i: (bi, 0, i, 0)))
    out_shape.append(jax.ShapeDtypeStruct((b, N_KV_HEADS, n, LANES), BF16))
    out_specs.append(pl.BlockSpec((1, N_KV_HEADS, tm, LANES), lambda bi, i: (bi, 0, i, 0)))
    return pl.pallas_call(
        functools.partial(_qkv_kernel, with_q=with_q),
        grid=(b, n // tm),
        in_specs=[
            pl.BlockSpec((1, tm, d), lambda bi, i: (bi, i, 0)),
            pl.BlockSpec((1, 1, d), mod_idx),
            pl.BlockSpec((1, 1, d), mod_idx),
            pl.BlockSpec((1, d), lambda bi, i: (0, 0)),
            pl.BlockSpec((d, ncol), lambda bi, i: (0, 0)),
            pl.BlockSpec((2, LANES), lambda bi, i: (0, 0)),
            pl.BlockSpec((LANES, LANES), lambda bi, i: (0, 0)),
            pl.BlockSpec((tm, LANES), lambda bi, i: (i, 0)),
            pl.BlockSpec((tm, LANES), lambda bi, i: (i, 0)),
        ],
        out_specs=out_specs,
        out_shape=out_shape,
        compiler_params=_params(("arbitrary", "arbitrary")),
        name="qkv" if with_q else "ctx_kv",
    )(x, shift, scale, gain, w, head_gain, ebd, cos, sin)


def _attn_kernel(q_ref, k_ref, v_ref, o_ref, *, tq, tk):
    m_rows = GQA_GROUP * tq
    q = q_ref[0].reshape(m_rows, HEAD_DIM)
    n_keys = k_ref.shape[2]
    m = jnp.full((m_rows, 1), -jnp.inf, F32)
    acc = jnp.zeros((m_rows, LANES), F32)
    for c in range(n_keys // tk):
        kc = k_ref[0, 0, c * tk:(c + 1) * tk, :]
        vc = v_ref[0, 0, c * tk:(c + 1) * tk, :]
        s = lax.dot_general(q, kc, (((1,), (1,)), ((), ())), preferred_element_type=F32)
        m_new = jnp.maximum(m, jnp.max(s, axis=-1, keepdims=True))
        alpha = jnp.exp2(m - m_new)
        p = jnp.exp2(s - m_new)
        acc = alpha * acc + _dot(p.astype(BF16), vc)
        m = m_new
    o = acc * (1.0 / pltpu.roll(acc, HEAD_DIM, 1))
    o_ref[0] = jnp.concatenate(
        [o[g * tq:(g + 1) * tq, :HEAD_DIM] for g in range(GQA_GROUP)], axis=-1).astype(BF16)


def _attention(q, k, v):
    b, _, n, _ = q.shape
    n_keys = k.shape[2]
    tq = min(256, n)
    tk = 256
    assert n_keys % tk == 0
    return pl.pallas_call(
        functools.partial(_attn_kernel, tq=tq, tk=tk),
        grid=(b, N_KV_HEADS, n // tq),
        in_specs=[
            pl.BlockSpec((1, GQA_GROUP, tq, HEAD_DIM), lambda bi, kh, qi: (bi, kh, qi, 0)),
            pl.BlockSpec((1, 1, n_keys, HEAD_DIM), lambda bi, kh, qi: (bi, kh, 0, 0)),
            pl.BlockSpec((1, 1, n_keys, LANES), lambda bi, kh, qi: (bi, kh, 0, 0)),
        ],
        out_specs=pl.BlockSpec((1, tq, GQA_GROUP * HEAD_DIM), lambda bi, kh, qi: (bi, qi, kh)),
        out_shape=jax.ShapeDtypeStruct((b, n, N_HEADS * HEAD_DIM), BF16),
        compiler_params=_params(("arbitrary", "arbitrary", "arbitrary")),
        name="attn",
    )(q, k, v)


def _proj_moe_kernel(a_ref, wo_ref, x_ref, g1_ref, gain_ref, sh_ref, sc_ref, wrh_ref, wrl_ref, br_ref,
                     x1_ref, h2_ref, gate_ref, meta_ref, cnt_ref, carry_ref):
    i = pl.program_id(0)

    @pl.when(i == 0)
    def _():
        carry_ref[...] = jnp.zeros_like(carry_ref)

    x1 = x_ref[...] + g1_ref[0] * _dot(a_ref[...], wo_ref[...])
    x1_ref[...] = x1
    h2 = _modulate(x1, gain_ref[...], sh_ref[0], sc_ref[0])
    h2_ref[...] = h2
    h_hi, h_lo = _split_bf16(h2)
    logits = (_dot(h_hi, wrh_ref[...]) + _dot(h_lo, wrh_ref[...]) + _dot(h_hi, wrl_ref[...])
              + br_ref[...])
    tm = logits.shape[0]
    lane = lax.broadcasted_iota(jnp.int32, (tm, LANES), 1)
    lane_f = lane.astype(F32)

    def top(vals):
        best = jnp.max(vals, axis=-1, keepdims=True)
        idx = jnp.min(jnp.where(vals == best, lane_f, float(LANES)), axis=-1, keepdims=True)
        return best, idx

    lg = jnp.where(lane < N_GROUPS, logits, NEG_BIG)
    g_max, g_sel = top(lg)
    p_grp = 1.0 / jnp.sum(jnp.where(lane < N_GROUPS, jnp.exp(lg - g_max), 0.0), axis=-1, keepdims=True)
    lo = N_GROUPS + EXPERTS_PER_GROUP * g_sel
    le = jnp.where((lane_f >= lo) & (lane_f < lo + EXPERTS_PER_GROUP), logits, NEG_BIG)
    t1, i1 = top(le)
    t2, i2 = top(jnp.where(lane_f == i1, NEG_BIG, le))
    e21 = jnp.exp(t2 - t1)
    w1 = 1.0 / (1.0 + e21)
    gate0 = p_grp * w1
    gate1 = p_grp * (e21 * w1)
    e0 = i1 - N_GROUPS
    e1 = i2 - N_GROUPS

    onehot = jnp.where((lane_f == e0) | (lane_f == e1), 1.0, 0.0)
    row = lax.broadcasted_iota(jnp.int32, (tm, tm), 0)
    col = lax.broadcasted_iota(jnp.int32, (tm, tm), 1)
    lower = jnp.where(col < row, 1.0, 0.0).astype(BF16)
    before = _dot(lower, onehot.astype(BF16)) + carry_ref[...]
    rank0 = jnp.sum(jnp.where(lane_f == e0, before, 0.0), axis=-1, keepdims=True)
    rank1 = jnp.sum(jnp.where(lane_f == e1, before, 0.0), axis=-1, keepdims=True)
    carry = carry_ref[...] + jnp.sum(onehot, axis=0, keepdims=True)
    carry_ref[...] = carry
    cnt_ref[...] = carry

    gate_ref[...] = jnp.where(lane == 0, gate0, jnp.where(lane == 1, gate1, 0.0))
    meta = jnp.where(lane == 0, e0, jnp.where(lane == 1, e1, jnp.where(lane == 2, rank0,
                                                                      jnp.where(lane == 3, rank1, 0.0))))
    meta_ref[...] = meta.astype(jnp.int32)


def _proj_moe(a, w_o, x, g1, gain, shift, scale, wr_hi, wr_lo, b_r, *, tokens_per_batch):
    t, d = x.shape
    tm = min(512, tokens_per_batch)
    per_b = tokens_per_batch // tm
    mod_idx = lambda i: (i // per_b, 0, 0)
    row_spec = lambda w: pl.BlockSpec((tm, w), lambda i: (i, 0))
    full = lambda s: pl.BlockSpec(s, lambda i: (0,) * len(s))
    return pl.pallas_call(
        _proj_moe_kernel,
        grid=(t // tm,),
        in_specs=[
            row_spec(d), full((d, d)), row_spec(d),
            pl.BlockSpec((1, 1, d), mod_idx), full((1, d)),
            pl.BlockSpec((1, 1, d), mod_idx), pl.BlockSpec((1, 1, d), mod_idx),
            full((d, LANES)), full((d, LANES)), full((1, LANES)),
        ],
        out_specs=[row_spec(d), row_spec(d), row_spec(LANES), row_spec(LANES), full((1, LANES))],
        out_shape=[
            jax.ShapeDtypeStruct((t, d), F32), jax.ShapeDtypeStruct((t, d), F32),
            jax.ShapeDtypeStruct((t, LANES), F32), jax.ShapeDtypeStruct((t, LANES), jnp.int32),
            jax.ShapeDtypeStruct((1, LANES), F32),
        ],
        scratch_shapes=[pltpu.VMEM((1, LANES), F32)],
        compiler_params=_params(("arbitrary",)),
        name="proj_moe",
    )(a, w_o, x, g1, gain, shift, scale, wr_hi, wr_lo, b_r)


def _row_gather_start(idx_ref, count, src_hbm, dst, sem):
    def body(j, carry):
        r = idx_ref[0, 0, j]
        pltpu.make_async_copy(src_hbm.at[pl.ds(r, 1)], dst.at[pl.ds(j, 1)], sem).start()
        return carry

    lax.fori_loop(0, count, body, 0, unroll=8)


def _row_gather_wait(count, src_hbm, dst, sem):
    pltpu.make_async_copy(src_hbm.at[pl.ds(0, count)], dst, sem).wait()


def _ffn_kernel(blk_e_ref, nused_ref, idx_cur_ref, idx_next_ref, h_hbm, w1_ref, w3_ref, w2_ref,
                y_ref, xg, sem):
    del blk_e_ref
    i = pl.program_id(0)
    nused = nused_ref[0]
    bm = xg.shape[1]

    @pl.when(i == 0)
    def _():
        _row_gather_start(idx_cur_ref, bm, h_hbm, xg.at[0], sem.at[0])

    @pl.when(i + 1 < nused)
    def _():
        nxt = (i + 1) % 2
        _row_gather_start(idx_next_ref, bm, h_hbm, xg.at[nxt], sem.at[nxt])

    @pl.when(i < nused)
    def _():
        slot = i % 2
        _row_gather_wait(bm, h_hbm, xg.at[slot], sem.at[slot])
        xb = xg[slot].astype(BF16)
        a = _dot(xb, w1_ref[0])
        b = _dot(xb, w3_ref[0])
        mid = (a / (1.0 + jnp.exp(-a))) * b
        y_ref[...] = _dot(mid.astype(BF16), w2_ref[0])

    @pl.when(i >= nused)
    def _():
        y_ref[...] = jnp.zeros_like(y_ref)


def _ffn(blk_e, nused, row_t, h2, w1, w3, w2):
    nblk = blk_e.shape[0]
    bm = FFN_BLOCK
    t, d = h2.shape
    de = w1.shape[2]
    idx = row_t.reshape(nblk, 1, bm)
    grid_spec = pltpu.PrefetchScalarGridSpec(
        num_scalar_prefetch=2,
        grid=(nblk,),
        in_specs=[
            pl.BlockSpec((1, 1, bm), lambda i, be, nu: (i, 0, 0), memory_space=pltpu.SMEM),
            pl.BlockSpec((1, 1, bm), lambda i, be, nu: (jnp.minimum(i + 1, nblk - 1), 0, 0),
                         memory_space=pltpu.SMEM),
            pl.BlockSpec(memory_space=pl.ANY),
            pl.BlockSpec((1, d, de), lambda i, be, nu: (be[i], 0, 0)),
            pl.BlockSpec((1, d, de), lambda i, be, nu: (be[i], 0, 0)),
            pl.BlockSpec((1, de, d), lambda i, be, nu: (be[i], 0, 0)),
        ],
        out_specs=pl.BlockSpec((bm, d), lambda i, be, nu: (i, 0)),
        scratch_shapes=[pltpu.VMEM((2, bm, d), F32), pltpu.SemaphoreType.DMA((2,))],
    )
    return pl.pallas_call(
        _ffn_kernel,
        grid_spec=grid_spec,
        out_shape=jax.ShapeDtypeStruct((nblk * bm, d), F32),
        compiler_params=_params(("arbitrary",)),
        name="ffn",
    )(blk_e, nused, idx, idx, h2, w1, w3, w2)


def _combine_kernel(idx_cur_ref, idx_next_ref, y_hbm, x_ref, g2_ref, gate_ref, fgain_ref, o_ref,
                    rbuf, sem, *, nsteps, final):
    i = pl.program_id(0)
    rows = rbuf.shape[1]
    tc = rows // 2

    @pl.when(i == 0)
    def _():
        _row_gather_start(idx_cur_ref, rows, y_hbm, rbuf.at[0], sem.at[0])

    @pl.when(i + 1 < nsteps)
    def _():
        nxt = (i + 1) % 2
        _row_gather_start(idx_next_ref, rows, y_hbm, rbuf.at[nxt], sem.at[nxt])

    slot = i % 2
    _row_gather_wait(rows, y_hbm, rbuf.at[slot], sem.at[slot])
    gate = gate_ref[...]
    y = gate[:, 0:1] * rbuf[slot, 0:tc, :] + gate[:, 1:2] * rbuf[slot, tc:rows, :]
    x2 = x_ref[...] + g2_ref[0] * y
    if final:
        ms = jnp.mean(x2 * x2, axis=-1, keepdims=True)
        x2 = (x2 * lax.rsqrt(ms + EPS)) * fgain_ref[...]
    o_ref[...] = x2


def _combine(dest, yb, x1, g2, gate, fgain, *, tokens_per_batch, final):
    t, d = x1.shape
    tc = min(COMBINE_TILE, tokens_per_batch)
    nsteps = t // tc
    per_b = tokens_per_batch // tc
    idx = dest.reshape(nsteps, tc, 2).transpose(0, 2, 1).reshape(nsteps, 1, 2 * tc)
    return pl.pallas_call(
        functools.partial(_combine_kernel, nsteps=nsteps, final=final),
        grid=(nsteps,),
        in_specs=[
            pl.BlockSpec((1, 1, 2 * tc), lambda i: (i, 0, 0), memory_space=pltpu.SMEM),
            pl.BlockSpec((1, 1, 2 * tc), lambda i: (jnp.minimum(i + 1, nsteps - 1), 0, 0),
                         memory_space=pltpu.SMEM),
            pl.BlockSpec(memory_space=pl.ANY),
            pl.BlockSpec((tc, d), lambda i: (i, 0)),
            pl.BlockSpec((1, 1, d), lambda i: (i // per_b, 0, 0)),
            pl.BlockSpec((tc, LANES), lambda i: (i, 0)),
            pl.BlockSpec((1, d), lambda i: (0, 0)),
        ],
        out_specs=pl.BlockSpec((tc, d), lambda i: (i, 0)),
        out_shape=jax.ShapeDtypeStruct((t, d), F32),
        scratch_shapes=[pltpu.VMEM((2, 2 * tc, d), F32), pltpu.SemaphoreType.DMA((2,))],
        compiler_params=_params(("arbitrary",)),
        name="combine_final" if final else "combine",
    )(idx, idx, yb, x1, g2, gate, fgain)


def _fmix_ch_kernel(x_ref, sh_ref, sc_ref, gain_ref, cc_ref, sc_mat_ref, a_ref, b_ref):
    h = _modulate(x_ref[...], gain_ref[...], sh_ref[0], sc_ref[0]).astype(BF16)
    gw = cc_ref.shape[0]
    for g in range(FOURIER_GROUPS):
        hg = h[:, g * gw:(g + 1) * gw]
        a_ref[:, g * gw:(g + 1) * gw] = _dot(hg, cc_ref[...]).astype(BF16)
        b_ref[:, g * gw:(g + 1) * gw] = _dot(hg, sc_mat_ref[...]).astype(BF16)


def _fmix_ch(x, shift, scale, gain, cc, sc, *, tokens_per_batch):
    t, d = x.shape
    tm = min(512, tokens_per_batch)
    per_b = tokens_per_batch // tm
    gw = d // FOURIER_GROUPS
    mod_idx = lambda i: (i // per_b, 0, 0)
    return pl.pallas_call(
        _fmix_ch_kernel,
        grid=(t // tm,),
        in_specs=[
            pl.BlockSpec((tm, d), lambda i: (i, 0)),
            pl.BlockSpec((1, 1, d), mod_idx), pl.BlockSpec((1, 1, d), mod_idx),
            pl.BlockSpec((1, d), lambda i: (0, 0)),
            pl.BlockSpec((gw, gw), lambda i: (0, 0)), pl.BlockSpec((gw, gw), lambda i: (0, 0)),
        ],
        out_specs=[pl.BlockSpec((tm, d), lambda i: (i, 0)), pl.BlockSpec((tm, d), lambda i: (i, 0))],
        out_shape=[jax.ShapeDtypeStruct((t, d), BF16), jax.ShapeDtypeStruct((t, d), BF16)],
        compiler_params=_params(("arbitrary",)),
        name="fmix_ch",
    )(x, shift, scale, gain, cc, sc)


def _fmix_pos_kernel(cn_ref, sn_ref, a_ref, b_ref, o_ref, acc_ref):
    k = pl.program_id(2)

    @pl.when(k == 0)
    def _():
        acc_ref[...] = jnp.zeros_like(acc_ref)

    acc_ref[...] += _dot(cn_ref[...], a_ref[0]) - _dot(sn_ref[...], b_ref[0])

    @pl.when(k == pl.num_programs(2) - 1)
    def _():
        o_ref[0] = acc_ref[...].astype(BF16)


def _fmix_pos(cn, sn, a, b):
    bsz, n, d = a.shape
    tmr = min(1024, n)
    tk = min(512, n)
    return pl.pallas_call(
        _fmix_pos_kernel,
        grid=(n // tmr, bsz, n // tk),
        in_specs=[
            pl.BlockSpec((tmr, tk), lambda i, bi, k: (i, k)),
            pl.BlockSpec((tmr, tk), lambda i, bi, k: (i, k)),
            pl.BlockSpec((1, tk, d), lambda i, bi, k: (bi, k, 0)),
            pl.BlockSpec((1, tk, d), lambda i, bi, k: (bi, k, 0)),
        ],
        out_specs=pl.BlockSpec((1, tmr, d), lambda i, bi, k: (bi, i, 0)),
        out_shape=jax.ShapeDtypeStruct((bsz, n, d), BF16),
        scratch_shapes=[pltpu.VMEM((tmr, d), F32)],
        compiler_params=_params(("arbitrary", "arbitrary", "arbitrary")),
        name="fmix_pos",
    )(cn, sn, a, b)


def _rope_tables(n):
    half = ROPE_AXIS_DIM // 2
    rows = n // GRID_W
    inv = ROPE_THETA ** (-jnp.arange(half, dtype=F32) / half)
    pos = jnp.arange(n, dtype=jnp.int32)
    ang_r = (pos // GRID_W).astype(F32)[:, None] * inv
    ang_c = (pos % GRID_W).astype(F32)[:, None] * inv
    del rows
    cos = jnp.concatenate([jnp.cos(ang_r)] * 2 + [jnp.cos(ang_c)] * 2, axis=-1)
    sin = jnp.concatenate([-jnp.sin(ang_r), jnp.sin(ang_r), -jnp.sin(ang_c), jnp.sin(ang_c)], axis=-1)
    return jnp.tile(cos, (1, LANES // HEAD_DIM)), jnp.tile(sin, (1, LANES // HEAD_DIM))


def _dft_mats(n, scale):
    r = 1
    while r * r < n:
        r *= 2
    c = n // r
    k = jnp.arange(n, dtype=jnp.int32)[:, None]
    ang_a = ((k * jnp.arange(r, dtype=jnp.int32)[None, :] * c) % n).astype(F32) * (2.0 * math.pi / n)
    ang_b = ((k * jnp.arange(c, dtype=jnp.int32)[None, :]) % n).astype(F32) * (2.0 * math.pi / n)
    ca, sa, cb, sb = jnp.cos(ang_a), jnp.sin(ang_a), jnp.cos(ang_b), jnp.sin(ang_b)
    cosm = (ca[:, :, None] * cb[:, None, :] - sa[:, :, None] * sb[:, None, :]).reshape(n, n)
    sinm = (sa[:, :, None] * cb[:, None, :] + ca[:, :, None] * sb[:, None, :]).reshape(n, n)
    return (cosm * scale).astype(BF16), (sinm * scale).astype(BF16)


def _moe(a, w_o, x, g1, g2, gain, shift, scale, w_rg, b_rg, w_re, b_re, w1, w3, w2, fgain, *,
         tokens_per_batch, final):
    t, d = x.shape
    pad = LANES - N_GROUPS - N_EXPERTS
    w_r = jnp.concatenate([w_rg, w_re, jnp.zeros((d, pad), F32)], axis=1)
    b_r = jnp.concatenate([b_rg, b_re, jnp.zeros((pad,), F32)])[None, :]
    wr_hi = w_r.astype(BF16)
    wr_lo = (w_r - wr_hi.astype(F32)).astype(BF16)
    x1, h2, gate, meta, cnt = _proj_moe(a, w_o, x, g1, gain, shift, scale, wr_hi, wr_lo, b_r,
                                        tokens_per_batch=tokens_per_batch)
    bm = FFN_BLOCK
    counts = cnt[0, :N_EXPERTS].astype(jnp.int32)
    pcounts = (counts + bm - 1) // bm * bm
    pend = jnp.cumsum(pcounts)
    pstart = pend - pcounts
    nblk = (2 * t + N_EXPERTS * (bm - 1) + bm - 1) // bm
    dest = pstart[meta[:, 0:2]] + meta[:, 2:4]
    blk_e = jnp.minimum(jnp.searchsorted(pend, jnp.arange(nblk, dtype=jnp.int32) * bm, side='right'),
                        N_EXPERTS - 1).astype(jnp.int32)
    nused = (pend[-1] // bm).astype(jnp.int32)[None]
    tok = jnp.broadcast_to(jnp.arange(t, dtype=jnp.int32)[:, None], (t, 2))
    row_t = jnp.zeros((nblk * bm,), jnp.int32).at[dest.reshape(-1)].set(tok.reshape(-1))
    yb = _ffn(blk_e, nused, row_t, h2, w1, w3, w2)
    return _combine(dest, yb, x1, g2, gate, fgain, tokens_per_batch=tokens_per_batch, final=final)


def kernel(x, c, ctx, c_ctx, w_mod, b_mod, norm_mix, norm_ffn, attn_w_qkv, attn_q_norm, attn_k_norm,
           attn_w_o, fourier_w_o, moe_w_rg, moe_b_rg, moe_w_re, moe_b_re, moe_w1, moe_w3, moe_w2,
           final_norm):
    bsz, n, d = x.shape
    t = bsz * n
    q_dim = N_HEADS * HEAD_DIM
    kv_dim = N_KV_HEADS * HEAD_DIM

    cond = jnp.zeros((16, d), F32).at[:bsz].set(c).at[bsz].set(c_ctx)
    mods = _ada(cond, w_mod, b_mod)

    def lat(layer, j):
        return mods[layer, :bsz, j * d:(j + 1) * d][:, None, :]

    def ctxm(layer, j):
        return mods[layer, bsz:bsz + 1, j * d:(j + 1) * d][:, None, :]

    w_qkv = attn_w_qkv[0].astype(BF16)
    qscale = (HEAD_DIM ** -0.5) * math.log2(math.e)
    head_gain = jnp.stack([jnp.tile(attn_q_norm[0] * qscale, LANES // HEAD_DIM),
                           jnp.tile(attn_k_norm[0], LANES // HEAD_DIM)])
    lane = jnp.arange(LANES)
    ebd = jnp.where((lane[:, None] // HEAD_DIM) == (lane[None, :] // HEAD_DIM),
                    1.0 / HEAD_DIM, 0.0).astype(BF16)
    cos, sin = _rope_tables(n)
    gain0 = norm_mix[0][None, :]
    q, k_l, v_l = _qkv(x, lat(0, 0), lat(0, 1), gain0, w_qkv, head_gain, ebd, cos, sin,
                       with_q=True, per_batch_mods=True)
    n_ctx = ctx.shape[1]
    k_c, v_c = _qkv(ctx, ctxm(0, 0), ctxm(0, 1), gain0, w_qkv[:, q_dim:], head_gain, ebd,
                    jnp.ones((n_ctx, LANES), F32), jnp.zeros((n_ctx, LANES), F32),
                    with_q=False, per_batch_mods=False)
    del kv_dim
    k_all = jnp.concatenate([k_l, k_c], axis=2)
    v_all = jnp.concatenate([v_l, v_c], axis=2)
    o = _attention(q, k_all, v_all).reshape(t, q_dim)

    xf = x.reshape(t, d)
    moe_w = lambda i: (moe_w_rg[i], moe_b_rg[i], moe_w_re[i], moe_b_re[i],
                       moe_w1[i].astype(BF16), moe_w3[i].astype(BF16), moe_w2[i].astype(BF16))
    fgain = final_norm[None, :]
    x2 = _moe(o, attn_w_o[0].astype(BF16), xf, lat(0, 2), lat(0, 5), norm_ffn[0][None, :],
              lat(0, 3), lat(0, 4), *moe_w(0), fgain, tokens_per_batch=n, final=False)

    gw = d // FOURIER_GROUPS
    cc, sc = _dft_mats(gw, gw ** -0.5)
    cn, sn = _dft_mats(n, n ** -0.5)
    a, b = _fmix_ch(x2, lat(1, 0), lat(1, 1), norm_mix[1][None, :], cc, sc, tokens_per_batch=n)
    f = _fmix_pos(cn, sn, a.reshape(bsz, n, d), b.reshape(bsz, n, d)).reshape(t, d)
    out = _moe(f, fourier_w_o[0].astype(BF16), x2, lat(1, 2), lat(1, 5), norm_ffn[1][None, :],
               lat(1, 3), lat(1, 4), *moe_w(1), fgain, tokens_per_batch=n, final=True)
    return out.reshape(bsz, n, d)
```

```python
import functools
import math

import jax
import jax.numpy as jnp
from jax import lax
from jax.experimental import pallas as pl
from jax.experimental.pallas import tpu as pltpu

F32 = jnp.float32
BF16 = jnp.bfloat16

N_HEADS = 16
N_KV_HEADS = 4
HEAD_DIM = 64
GQA_GROUP = N_HEADS // N_KV_HEADS
GRID_W = 64
ROPE_AXIS_DIM = HEAD_DIM // 2
ROPE_THETA = 10000.0
FOURIER_GROUPS = 4
N_GROUPS = 4
EXPERTS_PER_GROUP = 8
N_EXPERTS = N_GROUPS * EXPERTS_PER_GROUP
EPS = 1e-6
LANES = 128
NEG_BIG = -1e30
VMEM_LIMIT = 48 * 1024 * 1024

FFN_BLOCK = 256
MOE_TILE = 512
RUN_ALIGN = 8


def _sorted_rows(tile_tokens):
    return 2 * tile_tokens + N_EXPERTS * RUN_ALIGN


def _dot(a, b):
    return jnp.dot(a, b, preferred_element_type=F32)


def _split_bf16(a):
    hi = a.astype(BF16)
    lo = (a - hi.astype(F32)).astype(BF16)
    return hi, lo


def _params(sem):
    return pltpu.CompilerParams(dimension_semantics=sem, vmem_limit_bytes=VMEM_LIMIT)


def _ada_kernel(c_ref, w_ref, b_ref, o_ref):
    c = c_ref[...]
    s = c / (1.0 + jnp.exp(-c))
    s_hi, s_lo = _split_bf16(s)
    w_hi, w_lo = _split_bf16(w_ref[0])
    o_ref[0] = _dot(s_hi, w_hi) + _dot(s_hi, w_lo) + _dot(s_lo, w_hi) + b_ref[0]


def _ada(cond, w_mod, b_mod):
    depth, d, d6 = w_mod.shape
    rows = cond.shape[0]
    tn = 1536
    return pl.pallas_call(
        _ada_kernel,
        grid=(depth, d6 // tn),
        in_specs=[
            pl.BlockSpec((rows, d), lambda l, j: (0, 0)),
            pl.BlockSpec((1, d, tn), lambda l, j: (l, 0, j)),
            pl.BlockSpec((1, 1, tn), lambda l, j: (l, 0, j)),
        ],
        out_specs=pl.BlockSpec((1, rows, tn), lambda l, j: (l, 0, j)),
        out_shape=jax.ShapeDtypeStruct((depth, rows, d6), F32),
        compiler_params=_params(("arbitrary", "arbitrary")),
        name="ada",
    )(cond, w_mod, b_mod.reshape(depth, 1, d6))


def _modulate(xf, gain, shift, scale):
    ms = jnp.mean(xf * xf, axis=-1, keepdims=True)
    return (xf * lax.rsqrt(ms + EPS)) * gain * (1.0 + scale) + shift


def _qkv_kernel(x_ref, sh_ref, sc_ref, gain_ref, w_ref, hg_ref, ebd_ref, cos_ref, sin_ref,
                *out_refs, with_q):
    h = _modulate(x_ref[0], gain_ref[...], sh_ref[0], sc_ref[0])
    qkv = _dot(h.astype(BF16), w_ref[...])
    tm = qkv.shape[0]
    lane = lax.broadcasted_iota(jnp.int32, (tm, LANES), 1)
    first_half = (lane & 16) == 0
    low = lane < HEAD_DIM
    if with_q:
        q_ref, k_ref, v_ref = out_refs
        nq = N_HEADS * HEAD_DIM // LANES
    else:
        k_ref, v_ref = out_refs
        nq = 0
    nk = N_KV_HEADS * HEAD_DIM // LANES
    cos = cos_ref[...]
    sin = sin_ref[...]

    def norm_rope(c, gain_row):
        ms = _dot((c * c).astype(BF16), ebd_ref[...])
        cn = c * lax.rsqrt(ms + EPS) * gain_row
        partner = jnp.where(first_half, pltpu.roll(cn, LANES - 16, 1), pltpu.roll(cn, 16, 1))
        return cn * cos + partner * sin

    for j in range(nq):
        out = norm_rope(qkv[:, j * LANES:(j + 1) * LANES], hg_ref[0:1, :]).astype(BF16)
        q_ref[0, 2 * j] = out[:, :HEAD_DIM]
        q_ref[0, 2 * j + 1] = out[:, HEAD_DIM:]
    for j in range(nk):
        out = norm_rope(qkv[:, (nq + j) * LANES:(nq + j + 1) * LANES], hg_ref[1:2, :]).astype(BF16)
        k_ref[0, 2 * j] = out[:, :HEAD_DIM]
        k_ref[0, 2 * j + 1] = out[:, HEAD_DIM:]
    for j in range(nk):
        c = qkv[:, (nq + nk + j) * LANES:(nq + nk + j + 1) * LANES]
        v_ref[0, 2 * j] = jnp.where(low, c, 1.0).astype(BF16)
        v_ref[0, 2 * j + 1] = jnp.where(low, pltpu.roll(c, HEAD_DIM, 1), 1.0).astype(BF16)


def _qkv(x, shift, scale, gain, w, head_gain, ebd, cos, sin, *, with_q, per_batch_mods):
    b, n, d = x.shape
    tm = min(512, n)
    ncol = w.shape[1]
    mod_idx = (lambda bi, i: (bi, 0, 0)) if per_batch_mods else (lambda bi, i: (0, 0, 0))
    out_shape = []
    out_specs = []
    if with_q:
        out_shape.append(jax.ShapeDtypeStruct((b, N_HEADS, n, HEAD_DIM), BF16))
        out_specs.append(pl.BlockSpec((1, N_HEADS, tm, HEAD_DIM), lambda bi, i: (bi, 0, i, 0)))
    out_shape.append(jax.ShapeDtypeStruct((b, N_KV_HEADS, n, HEAD_DIM), BF16))
    out_specs.append(pl.BlockSpec((1, N_KV_HEADS, tm, HEAD_DIM), lambda bi, i: (bi, 0, i, 0)))
    out_shape.append(jax.ShapeDtypeStruct((b, N_KV_HEADS, n, LANES), BF16))
    out_specs.append(pl.BlockSpec((1, N_KV_HEADS, tm, LANES), lambda bi, i: (bi, 0, i, 0)))
    return pl.pallas_call(
        functools.partial(_qkv_kernel, with_q=with_q),
        grid=(b, n // tm),
        in_specs=[
            pl.BlockSpec((1, tm, d), lambda bi, i: (bi, i, 0)),
            pl.BlockSpec((1, 1, d), mod_idx),
            pl.BlockSpec((1, 1, d), mod_idx),
            pl.BlockSpec((1, d), lambda bi, i: (0, 0)),
            pl.BlockSpec((d, ncol), lambda bi, i: (0, 0)),
            pl.BlockSpec((2, LANES), lambda bi, i: (0, 0)),
            pl.BlockSpec((LANES, LANES), lambda bi, i: (0, 0)),
            pl.BlockSpec((tm, LANES), lambda bi, i: (i, 0)),
            pl.BlockSpec((tm, LANES), lambda bi, i: (i, 0)),
        ],
        out_specs=out_specs,
        out_shape=out_shape,
        compiler_params=_params(("arbitrary", "arbitrary")),
        name="qkv" if with_q else "ctx_kv",
    )(x, shift, scale, gain, w, head_gain, ebd, cos, sin)


def _attn_kernel(q_ref, k_ref, v_ref, kc_ref, vc_ref, o_ref, *, tq, tk):
    m_rows = GQA_GROUP * tq
    q = q_ref[0].reshape(m_rows, HEAD_DIM)
    m = jnp.full((m_rows, 1), -jnp.inf, F32)
    acc = jnp.zeros((m_rows, LANES), F32)
    chunks = [(k_ref, v_ref, c) for c in range(k_ref.shape[2] // tk)]
    chunks += [(kc_ref, vc_ref, c) for c in range(kc_ref.shape[2] // tk)]
    for kr, vr, c in chunks:
        kc = kr[0, 0, c * tk:(c + 1) * tk, :]
        vc = vr[0, 0, c * tk:(c + 1) * tk, :]
        s = lax.dot_general(q, kc, (((1,), (1,)), ((), ())), preferred_element_type=F32)
        m_new = jnp.maximum(m, jnp.max(s, axis=-1, keepdims=True))
        alpha = jnp.exp2(m - m_new)
        p = jnp.exp2(s - m_new)
        acc = alpha * acc + _dot(p.astype(BF16), vc)
        m = m_new
    o = acc * (1.0 / pltpu.roll(acc, HEAD_DIM, 1))
    o_ref[0] = jnp.concatenate(
        [o[g * tq:(g + 1) * tq, :HEAD_DIM] for g in range(GQA_GROUP)], axis=-1).astype(BF16)


def _attention(q, k, v, k_ctx, v_ctx):
    b, _, n, _ = q.shape
    n_ctx = k_ctx.shape[2]
    tq = min(256, n)
    tk = 256
    assert n % tk == 0 and n_ctx % tk == 0
    kv_spec = lambda rows, width: pl.BlockSpec((1, 1, rows, width), lambda bi, kh, qi: (bi, kh, 0, 0))
    return pl.pallas_call(
        functools.partial(_attn_kernel, tq=tq, tk=tk),
        grid=(b, N_KV_HEADS, n // tq),
        in_specs=[
            pl.BlockSpec((1, GQA_GROUP, tq, HEAD_DIM), lambda bi, kh, qi: (bi, kh, qi, 0)),
            kv_spec(n, HEAD_DIM), kv_spec(n, LANES), kv_spec(n_ctx, HEAD_DIM), kv_spec(n_ctx, LANES),
        ],
        out_specs=pl.BlockSpec((1, tq, GQA_GROUP * HEAD_DIM), lambda bi, kh, qi: (bi, qi, kh)),
        out_shape=jax.ShapeDtypeStruct((b, n, N_HEADS * HEAD_DIM), BF16),
        compiler_params=_params(("arbitrary", "arbitrary", "arbitrary")),
        name="attn",
    )(q, k, v, k_ctx, v_ctx)


def _proj_moe_kernel(a_ref, wo_ref, x_ref, g1_ref, gain_ref, sh_ref, sc_ref, wrh_ref, wrl_ref, br_ref,
                     x1_ref, s_ref, gp_ref, pc_ref):
    x1 = x_ref[...] + g1_ref[0] * _dot(a_ref[...], wo_ref[...])
    x1_ref[...] = x1
    h2 = _modulate(x1, gain_ref[...], sh_ref[0], sc_ref[0])
    h_hi, h_lo = _split_bf16(h2)
    logits = (_dot(h_hi, wrh_ref[...]) + _dot(h_lo, wrh_ref[...]) + _dot(h_hi, wrl_ref[...])
              + br_ref[...])
    tm = logits.shape[0]
    lane = lax.broadcasted_iota(jnp.int32, (tm, LANES), 1)
    lane_f = lane.astype(F32)

    def top(vals):
        best = jnp.max(vals, axis=-1, keepdims=True)
        idx = jnp.min(jnp.where(vals == best, lane_f, float(LANES)), axis=-1, keepdims=True)
        return best, idx

    lg = jnp.where(lane < N_GROUPS, logits, NEG_BIG)
    g_max, g_sel = top(lg)
    p_grp = 1.0 / jnp.sum(jnp.where(lane < N_GROUPS, jnp.exp(lg - g_max), 0.0), axis=-1, keepdims=True)
    lo = N_GROUPS + EXPERTS_PER_GROUP * g_sel
    le = jnp.where((lane_f >= lo) & (lane_f < lo + EXPERTS_PER_GROUP), logits, NEG_BIG)
    t1, i1 = top(le)
    t2, i2 = top(jnp.where(lane_f == i1, NEG_BIG, le))
    e21 = jnp.exp(t2 - t1)
    w1 = 1.0 / (1.0 + e21)
    gate0 = p_grp * w1
    gate1 = p_grp * (e21 * w1)
    e0 = i1 - N_GROUPS
    e1 = i2 - N_GROUPS

    onehot = jnp.where((lane_f == e0) | (lane_f == e1), 1.0, 0.0)
    row = lax.broadcasted_iota(jnp.int32, (tm, tm), 0)
    col = lax.broadcasted_iota(jnp.int32, (tm, tm), 1)
    lower = jnp.where(col < row, 1.0, 0.0).astype(BF16)
    before = _dot(lower, onehot.astype(BF16))
    cnt = jnp.sum(onehot, axis=0, keepdims=True)
    pc = jnp.floor((cnt + (RUN_ALIGN - 1)) * (1.0 / RUN_ALIGN))
    urow = lax.broadcasted_iota(jnp.int32, (LANES, LANES), 0)
    ucol = lax.broadcasted_iota(jnp.int32, (LANES, LANES), 1)
    upper = jnp.where(urow < ucol, 1.0, 0.0).astype(BF16)
    run_start = _dot(jnp.broadcast_to(pc, (8, LANES)).astype(BF16), upper)[0:1] * RUN_ALIGN
    base = run_start + before
    pos0 = jnp.sum(jnp.where(lane_f == e0, base, 0.0), axis=-1, keepdims=True)
    pos1 = jnp.sum(jnp.where(lane_f == e1, base, 0.0), axis=-1, keepdims=True)
    gp = jnp.where(lane == 0, gate0, jnp.where(lane == 1, gate1, jnp.where(lane == 2, pos0,
                                                                       jnp.where(lane == 3, pos1, 0.0))))
    gp_ref[...] = gp
    pc_ref[...] = jnp.broadcast_to(pc, (8, LANES)).astype(jnp.int32)

    pos_t = jnp.where(lane == 0, pos0, jnp.where(lane == 1, pos1, -1.0)).T
    sr = s_ref.shape[0]
    srow = lax.broadcasted_iota(jnp.int32, (sr, tm), 0).astype(F32)
    perm = jnp.where((srow == pos_t[0:1, :]) | (srow == pos_t[1:2, :]), 1.0, 0.0).astype(BF16)
    s_ref[...] = _dot(perm, h_hi)


def _proj_moe(a, w_o, x, g1, gain, shift, scale, wr_hi, wr_lo, b_r, *, tokens_per_batch):
    t, d = x.shape
    tm = min(MOE_TILE, tokens_per_batch)
    nt = t // tm
    sr = _sorted_rows(tm)
    per_b = tokens_per_batch // tm
    mod_idx = lambda i: (i // per_b, 0, 0)
    row_spec = lambda w: pl.BlockSpec((tm, w), lambda i: (i, 0))
    full = lambda s: pl.BlockSpec(s, lambda i: (0,) * len(s))
    return pl.pallas_call(
        _proj_moe_kernel,
        grid=(nt,),
        in_specs=[
            row_spec(d), full((d, d)), row_spec(d),
            pl.BlockSpec((1, 1, d), mod_idx), full((1, d)),
            pl.BlockSpec((1, 1, d), mod_idx), pl.BlockSpec((1, 1, d), mod_idx),
            full((d, LANES)), full((d, LANES)), full((1, LANES)),
        ],
        out_specs=[row_spec(d), pl.BlockSpec((sr, d), lambda i: (i, 0)), row_spec(LANES),
                   pl.BlockSpec((8, LANES), lambda i: (i, 0))],
        out_shape=[
            jax.ShapeDtypeStruct((t, d), F32), jax.ShapeDtypeStruct((nt * sr, d), F32),
            jax.ShapeDtypeStruct((t, LANES), F32), jax.ShapeDtypeStruct((nt * 8, LANES), jnp.int32),
        ],
        compiler_params=_params(("arbitrary",)),
        name="proj_moe",
    )(a, w_o, x, g1, gain, shift, scale, wr_hi, wr_lo, b_r)


def _rows(ref, start_units, n_units):
    return ref.at[pl.ds(pl.multiple_of(start_units * RUN_ALIGN, RUN_ALIGN), n_units * RUN_ALIGN)]


def _ffn_kernel(blk_e, blk_b0, blk_n, blk_tlo, blk_thi, seg_start, run_start, nused_ref,
                s_hbm, w1_ref, w3_ref, w2_ref, y_ref, xg, w1b, w3b, w2b, sem, *, sr_units):
    i = pl.program_id(0)
    nused = nused_ref[0]
    bm_units = xg.shape[1] // RUN_ALIGN

    def gather(b, slot):
        e = blk_e[b]
        b0 = blk_b0[b]

        def body(ti, carry):
            rs = seg_start[ti * N_EXPERTS + e]
            re = seg_start[(ti + 1) * N_EXPERTS + e]
            lo = jnp.maximum(rs, b0)
            n = jnp.minimum(re, b0 + bm_units) - lo

            @pl.when(n > 0)
            def _():
                src = ti * sr_units + run_start[ti * N_EXPERTS + e] + (lo - rs)
                pltpu.make_async_copy(_rows(s_hbm, src, n), _rows(xg.at[slot], lo - b0, n),
                                      sem.at[slot]).start()

            return carry

        lax.fori_loop(blk_tlo[b], blk_thi[b], body, 0)

    @pl.when(i == 0)
    def _():
        xg[...] = jnp.zeros_like(xg)
        gather(0, 0)

    @pl.when(i + 1 < nused)
    def _():
        gather(i + 1, (i + 1) % 2)

    @pl.when(i < nused)
    def _():
        slot = i % 2
        pltpu.make_async_copy(_rows(s_hbm, 0, blk_n[i]), _rows(xg.at[slot], 0, blk_n[i]),
                              sem.at[slot]).wait()

        @pl.when((i == 0) | (blk_e[i] != blk_e[jnp.maximum(i - 1, 0)]))
        def _():
            w1b[...] = w1_ref[0].astype(BF16)
            w3b[...] = w3_ref[0].astype(BF16)
            w2b[...] = w2_ref[0].astype(BF16)

        xb = xg[slot].astype(BF16)
        a = _dot(xb, w1b[...])
        b = _dot(xb, w3b[...])
        mid = (a / (1.0 + jnp.exp(-a))) * b
        y_ref[...] = _dot(mid.astype(BF16), w2b[...])

    @pl.when(i >= nused)
    def _():
        y_ref[...] = jnp.zeros_like(y_ref)


def _ffn(tables, s_tiles, w1, w3, w2, *, nblk, sr):
    bm = FFN_BLOCK
    d = s_tiles.shape[1]
    de = w1.shape[2]
    nsp = len(tables)
    grid_spec = pltpu.PrefetchScalarGridSpec(
        num_scalar_prefetch=nsp,
        grid=(nblk,),
        in_specs=[
            pl.BlockSpec(memory_space=pl.ANY),
            pl.BlockSpec((1, d, de), lambda i, be, *_: (be[i], 0, 0)),
            pl.BlockSpec((1, d, de), lambda i, be, *_: (be[i], 0, 0)),
            pl.BlockSpec((1, de, d), lambda i, be, *_: (be[i], 0, 0)),
        ],
        out_specs=pl.BlockSpec((bm, d), lambda i, *_: (i, 0)),
        scratch_shapes=[pltpu.VMEM((2, bm, d), F32), pltpu.VMEM((d, de), BF16), pltpu.VMEM((d, de), BF16),
                        pltpu.VMEM((de, d), BF16), pltpu.SemaphoreType.DMA((2,))],
    )
    return pl.pallas_call(
        functools.partial(_ffn_kernel, sr_units=sr // RUN_ALIGN),
        grid_spec=grid_spec,
        out_shape=jax.ShapeDtypeStruct((nblk * bm, d), F32),
        compiler_params=_params(("arbitrary",)),
        name="ffn",
    )(*tables, s_tiles, w1, w3, w2)


def _combine_kernel(run_len, run_start, seg_start, exp_start, tile_rows,
                    y_hbm, x_ref, g2_ref, gp_ref, fgain_ref, o_ref, ybuf, sem, *, nsteps, final):
    i = pl.program_id(0)

    def gather(ti, slot):
        for e in range(N_EXPERTS):
            n = run_len[ti * N_EXPERTS + e]

            @pl.when(n > 0)
            def _():
                src = exp_start[e] + seg_start[ti * N_EXPERTS + e]
                pltpu.make_async_copy(_rows(y_hbm, src, n),
                                      _rows(ybuf.at[slot], run_start[ti * N_EXPERTS + e], n),
                                      sem.at[slot]).start()

    @pl.when(i == 0)
    def _():
        ybuf[...] = jnp.zeros_like(ybuf)
        gather(0, 0)

    @pl.when(i + 1 < nsteps)
    def _():
        gather(i + 1, (i + 1) % 2)

    slot = i % 2
    pltpu.make_async_copy(_rows(y_hbm, 0, tile_rows[i]), _rows(ybuf.at[slot], 0, tile_rows[i]),
                          sem.at[slot]).wait()
    gp = gp_ref[...]
    tm = gp.shape[0]
    sr = ybuf.shape[1]
    col = lax.broadcasted_iota(jnp.int32, (tm, sr), 1).astype(F32)
    weights = (jnp.where(col == gp[:, 2:3], gp[:, 0:1], 0.0)
               + jnp.where(col == gp[:, 3:4], gp[:, 1:2], 0.0)).astype(BF16)
    y = _dot(weights, ybuf[slot].astype(BF16))
    x2 = x_ref[...] + g2_ref[0] * y
    if final:
        ms = jnp.mean(x2 * x2, axis=-1, keepdims=True)
        x2 = (x2 * lax.rsqrt(ms + EPS)) * fgain_ref[...]
    o_ref[...] = x2


def _combine(tables, yb, x1, g2, gp, fgain, *, tokens_per_batch, final):
    t, d = x1.shape
    tm = min(MOE_TILE, tokens_per_batch)
    nsteps = t // tm
    per_b = tokens_per_batch // tm
    sr = _sorted_rows(tm)
    grid_spec = pltpu.PrefetchScalarGridSpec(
        num_scalar_prefetch=len(tables),
        grid=(nsteps,),
        in_specs=[
            pl.BlockSpec(memory_space=pl.ANY),
            pl.BlockSpec((tm, d), lambda i, *_: (i, 0)),
            pl.BlockSpec((1, 1, d), lambda i, *_: (i // per_b, 0, 0)),
            pl.BlockSpec((tm, LANES), lambda i, *_: (i, 0)),
            pl.BlockSpec((1, d), lambda i, *_: (0, 0)),
        ],
        out_specs=pl.BlockSpec((tm, d), lambda i, *_: (i, 0)),
        scratch_shapes=[pltpu.VMEM((2, sr, d), F32), pltpu.SemaphoreType.DMA((2,))],
    )
    return pl.pallas_call(
        functools.partial(_combine_kernel, nsteps=nsteps, final=final),
        grid_spec=grid_spec,
        out_shape=jax.ShapeDtypeStruct((t, d), F32),
        compiler_params=_params(("arbitrary",)),
        name="combine_final" if final else "combine",
    )(*tables, yb, x1, g2, gp, fgain)


def _fmix_ch_kernel(x_ref, sh_ref, sc_ref, gain_ref, cc_ref, sc_mat_ref, a_ref, b_ref):
    h = _modulate(x_ref[...], gain_ref[...], sh_ref[0], sc_ref[0]).astype(BF16)
    gw = cc_ref.shape[0]
    for g in range(FOURIER_GROUPS):
        hg = h[:, g * gw:(g + 1) * gw]
        a_ref[:, g * gw:(g + 1) * gw] = _dot(hg, cc_ref[...]).astype(BF16)
        b_ref[:, g * gw:(g + 1) * gw] = _dot(hg, sc_mat_ref[...]).astype(BF16)


def _fmix_ch(x, shift, scale, gain, cc, sc, *, tokens_per_batch):
    t, d = x.shape
    tm = min(512, tokens_per_batch)
    per_b = tokens_per_batch // tm
    gw = d // FOURIER_GROUPS
    mod_idx = lambda i: (i // per_b, 0, 0)
    return pl.pallas_call(
        _fmix_ch_kernel,
        grid=(t // tm,),
        in_specs=[
            pl.BlockSpec((tm, d), lambda i: (i, 0)),
            pl.BlockSpec((1, 1, d), mod_idx), pl.BlockSpec((1, 1, d), mod_idx),
            pl.BlockSpec((1, d), lambda i: (0, 0)),
            pl.BlockSpec((gw, gw), lambda i: (0, 0)), pl.BlockSpec((gw, gw), lambda i: (0, 0)),
        ],
        out_specs=[pl.BlockSpec((tm, d), lambda i: (i, 0)), pl.BlockSpec((tm, d), lambda i: (i, 0))],
        out_shape=[jax.ShapeDtypeStruct((t, d), BF16), jax.ShapeDtypeStruct((t, d), BF16)],
        compiler_params=_params(("arbitrary",)),
        name="fmix_ch",
    )(x, shift, scale, gain, cc, sc)


def _fmix_pos_kernel(cn_ref, sn_ref, a_ref, b_ref, o_ref, acc_ref):
    k = pl.program_id(2)

    @pl.when(k == 0)
    def _():
        acc_ref[...] = jnp.zeros_like(acc_ref)

    acc_ref[...] += _dot(cn_ref[...], a_ref[0]) - _dot(sn_ref[...], b_ref[0])

    @pl.when(k == pl.num_programs(2) - 1)
    def _():
        o_ref[0] = acc_ref[...].astype(BF16)


def _fmix_pos(cn, sn, a, b):
    bsz, n, d = a.shape
    tmr = min(1024, n)
    tk = min(512, n)
    return pl.pallas_call(
        _fmix_pos_kernel,
        grid=(n // tmr, bsz, n // tk),
        in_specs=[
            pl.BlockSpec((tmr, tk), lambda i, bi, k: (i, k)),
            pl.BlockSpec((tmr, tk), lambda i, bi, k: (i, k)),
            pl.BlockSpec((1, tk, d), lambda i, bi, k: (bi, k, 0)),
            pl.BlockSpec((1, tk, d), lambda i, bi, k: (bi, k, 0)),
        ],
        out_specs=pl.BlockSpec((1, tmr, d), lambda i, bi, k: (bi, i, 0)),
        out_shape=jax.ShapeDtypeStruct((bsz, n, d), BF16),
        scratch_shapes=[pltpu.VMEM((tmr, d), F32)],
        compiler_params=_params(("arbitrary", "arbitrary", "arbitrary")),
        name="fmix_pos",
    )(cn, sn, a, b)


def _rope_tables(n):
    half = ROPE_AXIS_DIM // 2
    rows = n // GRID_W
    inv = ROPE_THETA ** (-jnp.arange(half, dtype=F32) / half)
    pos = jnp.arange(n, dtype=jnp.int32)
    ang_r = (pos // GRID_W).astype(F32)[:, None] * inv
    ang_c = (pos % GRID_W).astype(F32)[:, None] * inv
    del rows
    cos = jnp.concatenate([jnp.cos(ang_r)] * 2 + [jnp.cos(ang_c)] * 2, axis=-1)
    sin = jnp.concatenate([-jnp.sin(ang_r), jnp.sin(ang_r), -jnp.sin(ang_c), jnp.sin(ang_c)], axis=-1)
    return jnp.tile(cos, (1, LANES // HEAD_DIM)), jnp.tile(sin, (1, LANES // HEAD_DIM))


def _dft_mats(n, scale):
    r = 1
    while r * r < n:
        r *= 2
    c = n // r
    k = jnp.arange(n, dtype=jnp.int32)[:, None]
    ang_a = ((k * jnp.arange(r, dtype=jnp.int32)[None, :] * c) % n).astype(F32) * (2.0 * math.pi / n)
    ang_b = ((k * jnp.arange(c, dtype=jnp.int32)[None, :]) % n).astype(F32) * (2.0 * math.pi / n)
    ca, sa, cb, sb = jnp.cos(ang_a), jnp.sin(ang_a), jnp.cos(ang_b), jnp.sin(ang_b)
    cosm = (ca[:, :, None] * cb[:, None, :] - sa[:, :, None] * sb[:, None, :]).reshape(n, n)
    sinm = (sa[:, :, None] * cb[:, None, :] + ca[:, :, None] * sb[:, None, :]).reshape(n, n)
    return (cosm * scale).astype(BF16), (sinm * scale).astype(BF16)


def _moe(a, w_o, x, g1, g2, gain, shift, scale, w_rg, b_rg, w_re, b_re, w1, w3, w2, fgain, *,
         tokens_per_batch, final):
    t, d = x.shape
    pad = LANES - N_GROUPS - N_EXPERTS
    w_r = jnp.concatenate([w_rg, w_re, jnp.zeros((d, pad), F32)], axis=1)
    b_r = jnp.concatenate([b_rg, b_re, jnp.zeros((pad,), F32)])[None, :]
    wr_hi = w_r.astype(BF16)
    wr_lo = (w_r - wr_hi.astype(F32)).astype(BF16)
    x1, s_tiles, gp, pc = _proj_moe(a, w_o, x, g1, gain, shift, scale, wr_hi, wr_lo, b_r,
                                    tokens_per_batch=tokens_per_batch)
    tm = min(MOE_TILE, tokens_per_batch)
    nt = t // tm
    sr = _sorted_rows(tm)
    bmu = FFN_BLOCK // RUN_ALIGN
    i32 = jnp.int32
    run_len = pc.reshape(nt, 8, LANES)[:, 0, :N_EXPERTS]
    seg_start = jnp.concatenate([jnp.zeros((1, N_EXPERTS), i32), jnp.cumsum(run_len, axis=0)])
    seg_len = seg_start[-1]
    seg_pad = (seg_len + bmu - 1) // bmu * bmu
    exp_end = jnp.cumsum(seg_pad)
    exp_start = exp_end - seg_pad
    run_start = jnp.cumsum(run_len, axis=1) - run_len
    tile_rows = run_start[:, -1] + run_len[:, -1]
    max_units = (2 * t) // RUN_ALIGN + nt * N_EXPERTS + N_EXPERTS * (bmu - 1)
    nblk = (max_units + bmu - 1) // bmu
    blk_start = jnp.arange(nblk, dtype=i32) * bmu
    blk_e = jnp.minimum(jnp.sum((exp_end[None, :] <= blk_start[:, None]).astype(i32), axis=1),
                        N_EXPERTS - 1)
    blk_b0 = blk_start - exp_start[blk_e]
    blk_n = jnp.clip(seg_len[blk_e] - blk_b0, 0, bmu)
    seg_e = seg_start.T[blk_e]
    blk_tlo = jnp.sum((seg_e[:, 1:] <= blk_b0[:, None]).astype(i32), axis=1)
    blk_thi = jnp.sum((seg_e[:, :-1] < (blk_b0 + bmu)[:, None]).astype(i32), axis=1)
    nused = (exp_end[-1] // bmu)[None]
    flat = lambda v: v.reshape(-1).astype(i32)
    yb = _ffn((blk_e, blk_b0, blk_n, blk_tlo, blk_thi, flat(seg_start), flat(run_start), nused),
              s_tiles, w1, w3, w2, nblk=nblk, sr=sr)
    return _combine((flat(run_len), flat(run_start), flat(seg_start), exp_start, tile_rows),
                    yb, x1, g2, gp, fgain, tokens_per_batch=tokens_per_batch, final=final)


def kernel(x, c, ctx, c_ctx, w_mod, b_mod, norm_mix, norm_ffn, attn_w_qkv, attn_q_norm, attn_k_norm,
           attn_w_o, fourier_w_o, moe_w_rg, moe_b_rg, moe_w_re, moe_b_re, moe_w1, moe_w3, moe_w2,
           final_norm):
    bsz, n, d = x.shape
    t = bsz * n
    q_dim = N_HEADS * HEAD_DIM
    kv_dim = N_KV_HEADS * HEAD_DIM

    cond = jnp.zeros((16, d), F32).at[:bsz].set(c).at[bsz].set(c_ctx)
    mods = _ada(cond, w_mod, b_mod)

    def lat(layer, j):
        return mods[layer, :bsz, j * d:(j + 1) * d][:, None, :]

    def ctxm(layer, j):
        return mods[layer, bsz:bsz + 1, j * d:(j + 1) * d][:, None, :]

    w_qkv = attn_w_qkv[0].astype(BF16)
    qscale = (HEAD_DIM ** -0.5) * math.log2(math.e)
    head_gain = jnp.stack([jnp.tile(attn_q_norm[0] * qscale, LANES // HEAD_DIM),
                           jnp.tile(attn_k_norm[0], LANES // HEAD_DIM)])
    lane = jnp.arange(LANES)
    ebd = jnp.where((lane[:, None] // HEAD_DIM) == (lane[None, :] // HEAD_DIM),
                    1.0 / HEAD_DIM, 0.0).astype(BF16)
    cos, sin = _rope_tables(n)
    gain0 = norm_mix[0][None, :]
    q, k_l, v_l = _qkv(x, lat(0, 0), lat(0, 1), gain0, w_qkv, head_gain, ebd, cos, sin,
                       with_q=True, per_batch_mods=True)
    n_ctx = ctx.shape[1]
    k_c, v_c = _qkv(ctx, ctxm(0, 0), ctxm(0, 1), gain0, w_qkv[:, q_dim:], head_gain, ebd,
                    jnp.ones((n_ctx, LANES), F32), jnp.zeros((n_ctx, LANES), F32),
                    with_q=False, per_batch_mods=False)
    del kv_dim
    o = _attention(q, k_l, v_l, k_c, v_c).reshape(t, q_dim)

    xf = x.reshape(t, d)
    moe_w = lambda i: (moe_w_rg[i], moe_b_rg[i], moe_w_re[i], moe_b_re[i],
                       moe_w1[i], moe_w3[i], moe_w2[i])
    fgain = final_norm[None, :]
    x2 = _moe(o, attn_w_o[0].astype(BF16), xf, lat(0, 2), lat(0, 5), norm_ffn[0][None, :],
              lat(0, 3), lat(0, 4), *moe_w(0), fgain, tokens_per_batch=n, final=False)

    gw = d // FOURIER_GROUPS
    cc, sc = _dft_mats(gw, gw ** -0.5)
    cn, sn = _dft_mats(n, n ** -0.5)
    a, b = _fmix_ch(x2, lat(1, 0), lat(1, 1), norm_mix[1][None, :], cc, sc, tokens_per_batch=n)
    f = _fmix_pos(cn, sn, a.reshape(bsz, n, d), b.reshape(bsz, n, d)).reshape(t, d)
    out = _moe(f, fourier_w_o[0].astype(BF16), x2, lat(1, 2), lat(1, 5), norm_ffn[1][None, :],
               lat(1, 3), lat(1, 4), *moe_w(1), fgain, tokens_per_batch=n, final=True)
    return out.reshape(bsz, n, d)
```

```python
import functools
import math

import jax
import jax.numpy as jnp
from jax import lax
from jax.experimental import pallas as pl
from jax.experimental.pallas import tpu as pltpu

F32 = jnp.float32
BF16 = jnp.bfloat16

N_HEADS = 16
N_KV_HEADS = 4
HEAD_DIM = 64
GQA_GROUP = N_HEADS // N_KV_HEADS
GRID_W = 64
ROPE_AXIS_DIM = HEAD_DIM // 2
ROPE_THETA = 10000.0
FOURIER_GROUPS = 4
N_GROUPS = 4
EXPERTS_PER_GROUP = 8
N_EXPERTS = N_GROUPS * EXPERTS_PER_GROUP
EPS = 1e-6
LANES = 128
NEG_BIG = -1e30
VMEM_LIMIT = 48 * 1024 * 1024

FFN_BLOCK = 512
MOE_TILE = 512
RUN_ALIGN = 8


def _sorted_rows(tile_tokens):
    return 2 * tile_tokens + N_EXPERTS * RUN_ALIGN


def _dot(a, b):
    return jnp.dot(a, b, preferred_element_type=F32)


def _split_bf16(a):
    hi = a.astype(BF16)
    lo = (a - hi.astype(F32)).astype(BF16)
    return hi, lo


def _params(sem):
    return pltpu.CompilerParams(dimension_semantics=sem, vmem_limit_bytes=VMEM_LIMIT)


def _ada_kernel(c_ref, w_ref, b_ref, o_ref):
    c = c_ref[...]
    s = c / (1.0 + jnp.exp(-c))
    s_hi, s_lo = _split_bf16(s)
    w_hi, w_lo = _split_bf16(w_ref[0])
    o_ref[0] = _dot(s_hi, w_hi) + _dot(s_hi, w_lo) + _dot(s_lo, w_hi) + b_ref[0]


def _ada(cond, w_mod, b_mod):
    depth, d, d6 = w_mod.shape
    rows = cond.shape[0]
    tn = 1536
    return pl.pallas_call(
        _ada_kernel,
        grid=(depth, d6 // tn),
        in_specs=[
            pl.BlockSpec((rows, d), lambda l, j: (0, 0)),
            pl.BlockSpec((1, d, tn), lambda l, j: (l, 0, j)),
            pl.BlockSpec((1, 1, tn), lambda l, j: (l, 0, j)),
        ],
        out_specs=pl.BlockSpec((1, rows, tn), lambda l, j: (l, 0, j)),
        out_shape=jax.ShapeDtypeStruct((depth, rows, d6), F32),
        compiler_params=_params(("arbitrary", "arbitrary")),
        name="ada",
    )(cond, w_mod, b_mod.reshape(depth, 1, d6))


def _modulate(xf, gain, shift, scale):
    ms = jnp.mean(xf * xf, axis=-1, keepdims=True)
    return (xf * lax.rsqrt(ms + EPS)) * gain * (1.0 + scale) + shift


def _qkv_kernel(x_ref, sh_ref, sc_ref, gain_ref, w_ref, hg_ref, ebd_ref, cos_ref, sin_ref,
                *out_refs, with_q):
    h = _modulate(x_ref[0], gain_ref[...], sh_ref[0], sc_ref[0])
    qkv = _dot(h.astype(BF16), w_ref[...])
    tm = qkv.shape[0]
    lane = lax.broadcasted_iota(jnp.int32, (tm, LANES), 1)
    first_half = (lane & 16) == 0
    low = lane < HEAD_DIM
    if with_q:
        q_ref, k_ref, v_ref = out_refs
        nq = N_HEADS * HEAD_DIM // LANES
    else:
        k_ref, v_ref = out_refs
        nq = 0
    nk = N_KV_HEADS * HEAD_DIM // LANES
    cos = cos_ref[...]
    sin = sin_ref[...]

    def norm_rope(c, gain_row):
        ms = _dot((c * c).astype(BF16), ebd_ref[...])
        cn = c * lax.rsqrt(ms + EPS) * gain_row
        partner = jnp.where(first_half, pltpu.roll(cn, LANES - 16, 1), pltpu.roll(cn, 16, 1))
        return cn * cos + partner * sin

    for j in range(nq):
        out = norm_rope(qkv[:, j * LANES:(j + 1) * LANES], hg_ref[0:1, :]).astype(BF16)
        q_ref[0, 2 * j] = out[:, :HEAD_DIM]
        q_ref[0, 2 * j + 1] = out[:, HEAD_DIM:]
    for j in range(nk):
        out = norm_rope(qkv[:, (nq + j) * LANES:(nq + j + 1) * LANES], hg_ref[1:2, :]).astype(BF16)
        k_ref[0, 2 * j] = out[:, :HEAD_DIM]
        k_ref[0, 2 * j + 1] = out[:, HEAD_DIM:]
    for j in range(nk):
        c = qkv[:, (nq + nk + j) * LANES:(nq + nk + j + 1) * LANES]
        v_ref[0, 2 * j] = jnp.where(low, c, 1.0).astype(BF16)
        v_ref[0, 2 * j + 1] = jnp.where(low, pltpu.roll(c, HEAD_DIM, 1), 1.0).astype(BF16)


def _qkv(x, shift, scale, gain, w, head_gain, ebd, cos, sin, *, with_q, per_batch_mods):
    b, n, d = x.shape
    tm = min(512, n)
    ncol = w.shape[1]
    mod_idx = (lambda bi, i: (bi, 0, 0)) if per_batch_mods else (lambda bi, i: (0, 0, 0))
    out_shape = []
    out_specs = []
    if with_q:
        out_shape.append(jax.ShapeDtypeStruct((b, N_HEADS, n, HEAD_DIM), BF16))
        out_specs.append(pl.BlockSpec((1, N_HEADS, tm, HEAD_DIM), lambda bi, i: (bi, 0, i, 0)))
    out_shape.append(jax.ShapeDtypeStruct((b, N_KV_HEADS, n, HEAD_DIM), BF16))
    out_specs.append(pl.BlockSpec((1, N_KV_HEADS, tm, HEAD_DIM), lambda bi, i: (bi, 0, i, 0)))
    out_shape.append(jax.ShapeDtypeStruct((b, N_KV_HEADS, n, LANES), BF16))
    out_specs.append(pl.BlockSpec((1, N_KV_HEADS, tm, LANES), lambda bi, i: (bi, 0, i, 0)))
    return pl.pallas_call(
        functools.partial(_qkv_kernel, with_q=with_q),
        grid=(b, n // tm),
        in_specs=[
            pl.BlockSpec((1, tm, d), lambda bi, i: (bi, i, 0)),
            pl.BlockSpec((1, 1, d), mod_idx),
            pl.BlockSpec((1, 1, d), mod_idx),
            pl.BlockSpec((1, d), lambda bi, i: (0, 0)),
            pl.BlockSpec((d, ncol), lambda bi, i: (0, 0)),
            pl.BlockSpec((2, LANES), lambda bi, i: (0, 0)),
            pl.BlockSpec((LANES, LANES), lambda bi, i: (0, 0)),
            pl.BlockSpec((tm, LANES), lambda bi, i: (i, 0)),
            pl.BlockSpec((tm, LANES), lambda bi, i: (i, 0)),
        ],
        out_specs=out_specs,
        out_shape=out_shape,
        compiler_params=_params(("arbitrary", "arbitrary")),
        name="qkv" if with_q else "ctx_kv",
    )(x, shift, scale, gain, w, head_gain, ebd, cos, sin)


def _attn_kernel(q_ref, k_ref, v_ref, kc_ref, vc_ref, o_ref, *, tq, tk):
    m_rows = GQA_GROUP * tq
    q = q_ref[0].reshape(m_rows, HEAD_DIM)
    m = jnp.full((m_rows, 1), -jnp.inf, F32)
    acc = jnp.zeros((m_rows, LANES), F32)
    chunks = []
    for kr, vr in ((k_ref, v_ref), (kc_ref, vc_ref)):
        size = min(tk, kr.shape[2])
        chunks += [(kr, vr, c, size) for c in range(kr.shape[2] // size)]
    for kr, vr, c, size in chunks:
        kc = kr[0, 0, c * size:(c + 1) * size, :]
        vc = vr[0, 0, c * size:(c + 1) * size, :]
        s = lax.dot_general(q, kc, (((1,), (1,)), ((), ())), preferred_element_type=F32)
        m_new = jnp.maximum(m, jnp.max(s, axis=-1, keepdims=True))
        alpha = jnp.exp2(m - m_new)
        p = jnp.exp2(s - m_new)
        acc = alpha * acc + _dot(p.astype(BF16), vc)
        m = m_new
    o = acc * (1.0 / pltpu.roll(acc, HEAD_DIM, 1))
    o_ref[0] = jnp.concatenate(
        [o[g * tq:(g + 1) * tq, :HEAD_DIM] for g in range(GQA_GROUP)], axis=-1).astype(BF16)


def _attention(q, k, v, k_ctx, v_ctx):
    b, _, n, _ = q.shape
    n_ctx = k_ctx.shape[2]
    tq = min(256, n)
    tk = 256
    assert n % min(tk, n) == 0 and n_ctx % min(tk, n_ctx) == 0
    kv_spec = lambda rows, width: pl.BlockSpec((1, 1, rows, width), lambda bi, kh, qi: (bi, kh, 0, 0))
    return pl.pallas_call(
        functools.partial(_attn_kernel, tq=tq, tk=tk),
        grid=(b, N_KV_HEADS, n // tq),
        in_specs=[
            pl.BlockSpec((1, GQA_GROUP, tq, HEAD_DIM), lambda bi, kh, qi: (bi, kh, qi, 0)),
            kv_spec(n, HEAD_DIM), kv_spec(n, LANES), kv_spec(n_ctx, HEAD_DIM), kv_spec(n_ctx, LANES),
        ],
        out_specs=pl.BlockSpec((1, tq, GQA_GROUP * HEAD_DIM), lambda bi, kh, qi: (bi, qi, kh)),
        out_shape=jax.ShapeDtypeStruct((b, n, N_HEADS * HEAD_DIM), BF16),
        compiler_params=_params(("arbitrary", "arbitrary", "arbitrary")),
        name="attn",
    )(q, k, v, k_ctx, v_ctx)


def _proj_moe_kernel(a_ref, wo_ref, x_ref, g1_ref, gain_ref, sh_ref, sc_ref, wrh_ref, wrl_ref, br_ref,
                     x1_ref, s_ref, gp_ref, pc_ref):
    x1 = x_ref[...] + g1_ref[0] * _dot(a_ref[...], wo_ref[...])
    x1_ref[...] = x1
    h2 = _modulate(x1, gain_ref[...], sh_ref[0], sc_ref[0])
    h_hi, h_lo = _split_bf16(h2)
    logits = (_dot(h_hi, wrh_ref[...]) + _dot(h_lo, wrh_ref[...]) + _dot(h_hi, wrl_ref[...])
              + br_ref[...])
    tm = logits.shape[0]
    lane = lax.broadcasted_iota(jnp.int32, (tm, LANES), 1)
    lane_f = lane.astype(F32)

    def top(vals):
        best = jnp.max(vals, axis=-1, keepdims=True)
        idx = jnp.min(jnp.where(vals == best, lane_f, float(LANES)), axis=-1, keepdims=True)
        return best, idx

    lg = jnp.where(lane < N_GROUPS, logits, NEG_BIG)
    g_max, g_sel = top(lg)
    p_grp = 1.0 / jnp.sum(jnp.where(lane < N_GROUPS, jnp.exp(lg - g_max), 0.0), axis=-1, keepdims=True)
    lo = N_GROUPS + EXPERTS_PER_GROUP * g_sel
    le = jnp.where((lane_f >= lo) & (lane_f < lo + EXPERTS_PER_GROUP), logits, NEG_BIG)
    t1, i1 = top(le)
    t2, i2 = top(jnp.where(lane_f == i1, NEG_BIG, le))
    e21 = jnp.exp(t2 - t1)
    w1 = 1.0 / (1.0 + e21)
    gate0 = p_grp * w1
    gate1 = p_grp * (e21 * w1)
    e0 = i1 - N_GROUPS
    e1 = i2 - N_GROUPS

    onehot = jnp.where((lane_f == e0) | (lane_f == e1), 1.0, 0.0)
    row = lax.broadcasted_iota(jnp.int32, (tm, tm), 0)
    col = lax.broadcasted_iota(jnp.int32, (tm, tm), 1)
    lower = jnp.where(col < row, 1.0, 0.0).astype(BF16)
    before = _dot(lower, onehot.astype(BF16))
    cnt = jnp.sum(onehot, axis=0, keepdims=True)
    pc = jnp.floor((cnt + (RUN_ALIGN - 1)) * (1.0 / RUN_ALIGN))
    urow = lax.broadcasted_iota(jnp.int32, (LANES, LANES), 0)
    ucol = lax.broadcasted_iota(jnp.int32, (LANES, LANES), 1)
    upper = jnp.where(urow < ucol, 1.0, 0.0).astype(BF16)
    run_start = _dot(jnp.broadcast_to(pc, (8, LANES)).astype(BF16), upper)[0:1] * RUN_ALIGN
    base = run_start + before
    pos0 = jnp.sum(jnp.where(lane_f == e0, base, 0.0), axis=-1, keepdims=True)
    pos1 = jnp.sum(jnp.where(lane_f == e1, base, 0.0), axis=-1, keepdims=True)
    gp = jnp.where(lane == 0, gate0, jnp.where(lane == 1, gate1, jnp.where(lane == 2, pos0,
                                                                       jnp.where(lane == 3, pos1, 0.0))))
    gp_ref[...] = gp
    pc_ref[...] = jnp.broadcast_to(pc, (8, LANES)).astype(jnp.int32)

    pos_t = jnp.where(lane == 0, pos0, jnp.where(lane == 1, pos1, -1.0)).T
    sr = s_ref.shape[0]
    srow = lax.broadcasted_iota(jnp.int32, (sr, tm), 0).astype(F32)
    perm = jnp.where((srow == pos_t[0:1, :]) | (srow == pos_t[1:2, :]), 1.0, 0.0).astype(BF16)
    s_ref[...] = _dot(perm, h_hi)


def _proj_moe(a, w_o, x, g1, gain, shift, scale, wr_hi, wr_lo, b_r, *, tokens_per_batch):
    t, d = x.shape
    tm = min(MOE_TILE, tokens_per_batch)
    nt = t // tm
    sr = _sorted_rows(tm)
    per_b = tokens_per_batch // tm
    mod_idx = lambda i: (i // per_b, 0, 0)
    row_spec = lambda w: pl.BlockSpec((tm, w), lambda i: (i, 0))
    full = lambda s: pl.BlockSpec(s, lambda i: (0,) * len(s))
    return pl.pallas_call(
        _proj_moe_kernel,
        grid=(nt,),
        in_specs=[
            row_spec(d), full((d, d)), row_spec(d),
            pl.BlockSpec((1, 1, d), mod_idx), full((1, d)),
            pl.BlockSpec((1, 1, d), mod_idx), pl.BlockSpec((1, 1, d), mod_idx),
            full((d, LANES)), full((d, LANES)), full((1, LANES)),
        ],
        out_specs=[row_spec(d), pl.BlockSpec((sr, d), lambda i: (i, 0)), row_spec(LANES),
                   pl.BlockSpec((8, LANES), lambda i: (i, 0))],
        out_shape=[
            jax.ShapeDtypeStruct((t, d), F32), jax.ShapeDtypeStruct((nt * sr, d), F32),
            jax.ShapeDtypeStruct((t, LANES), F32), jax.ShapeDtypeStruct((nt * 8, LANES), jnp.int32),
        ],
        compiler_params=_params(("arbitrary",)),
        name="proj_moe",
    )(a, w_o, x, g1, gain, shift, scale, wr_hi, wr_lo, b_r)


def _rows(ref, start_units, n_units):
    return ref.at[pl.ds(pl.multiple_of(start_units * RUN_ALIGN, RUN_ALIGN), n_units * RUN_ALIGN)]


def _ffn_kernel(blk_e, blk_b0, blk_n, blk_tlo, blk_thi, seg_start, run_start, nused_ref,
                s_hbm, w1_ref, w3_ref, w2_ref, y_ref, xg, w1b, w3b, w2b, sem, *, sr_units):
    i = pl.program_id(0)
    nused = nused_ref[0]
    bm_units = xg.shape[1] // RUN_ALIGN

    def gather(b, slot):
        e = blk_e[b]
        b0 = blk_b0[b]

        def body(ti, carry):
            rs = seg_start[ti * N_EXPERTS + e]
            re = seg_start[(ti + 1) * N_EXPERTS + e]
            lo = jnp.maximum(rs, b0)
            n = jnp.minimum(re, b0 + bm_units) - lo

            @pl.when(n > 0)
            def _():
                src = ti * sr_units + run_start[ti * N_EXPERTS + e] + (lo - rs)
                pltpu.make_async_copy(_rows(s_hbm, src, n), _rows(xg.at[slot], lo - b0, n),
                                      sem.at[slot]).start()

            return carry

        lax.fori_loop(blk_tlo[b], blk_thi[b], body, 0)

    @pl.when(i == 0)
    def _():
        xg[...] = jnp.zeros_like(xg)
        gather(0, 0)

    @pl.when(i + 1 < nused)
    def _():
        gather(i + 1, (i + 1) % 2)

    @pl.when(i < nused)
    def _():
        slot = i % 2
        pltpu.make_async_copy(_rows(s_hbm, 0, blk_n[i]), _rows(xg.at[slot], 0, blk_n[i]),
                              sem.at[slot]).wait()

        @pl.when((i == 0) | (blk_e[i] != blk_e[jnp.maximum(i - 1, 0)]))
        def _():
            w1b[...] = w1_ref[0, 0].astype(BF16)
            w3b[...] = w3_ref[0, 0].astype(BF16)
            w2b[...] = w2_ref[0, 0].astype(BF16)

        xb = xg[slot].astype(BF16)
        a = _dot(xb, w1b[...])
        b = _dot(xb, w3b[...])
        mid = (a / (1.0 + jnp.exp(-a))) * b
        y_ref[...] = _dot(mid.astype(BF16), w2b[...])

    @pl.when(i >= nused)
    def _():
        y_ref[...] = jnp.zeros_like(y_ref)


def _ffn(tables, s_tiles, w1, w3, w2, *, layer, nblk, sr):
    bm = FFN_BLOCK
    d = s_tiles.shape[1]
    de = w1.shape[3]
    nsp = len(tables)
    grid_spec = pltpu.PrefetchScalarGridSpec(
        num_scalar_prefetch=nsp,
        grid=(nblk,),
        in_specs=[
            pl.BlockSpec(memory_space=pl.ANY),
            pl.BlockSpec((1, 1, d, de), lambda i, be, *_: (layer, be[i], 0, 0)),
            pl.BlockSpec((1, 1, d, de), lambda i, be, *_: (layer, be[i], 0, 0)),
            pl.BlockSpec((1, 1, de, d), lambda i, be, *_: (layer, be[i], 0, 0)),
        ],
        out_specs=pl.BlockSpec((bm, d), lambda i, *_: (i, 0)),
        scratch_shapes=[pltpu.VMEM((2, bm, d), F32), pltpu.VMEM((d, de), BF16), pltpu.VMEM((d, de), BF16),
                        pltpu.VMEM((de, d), BF16), pltpu.SemaphoreType.DMA((2,))],
    )
    return pl.pallas_call(
        functools.partial(_ffn_kernel, sr_units=sr // RUN_ALIGN),
        grid_spec=grid_spec,
        out_shape=jax.ShapeDtypeStruct((nblk * bm, d), F32),
        compiler_params=_params(("arbitrary",)),
        name="ffn",
    )(*tables, s_tiles, w1, w3, w2)


def _combine_kernel(run_len, run_start, seg_start, exp_start, tile_rows,
                    y_hbm, x_ref, g2_ref, gp_ref, fgain_ref, o_ref, ybuf, sem, *, nsteps, final):
    i = pl.program_id(0)

    def gather(ti, slot):
        for e in range(N_EXPERTS):
            n = run_len[ti * N_EXPERTS + e]

            @pl.when(n > 0)
            def _():
                src = exp_start[e] + seg_start[ti * N_EXPERTS + e]
                pltpu.make_async_copy(_rows(y_hbm, src, n),
                                      _rows(ybuf.at[slot], run_start[ti * N_EXPERTS + e], n),
                                      sem.at[slot]).start()

    @pl.when(i == 0)
    def _():
        ybuf[...] = jnp.zeros_like(ybuf)
        gather(0, 0)

    @pl.when(i + 1 < nsteps)
    def _():
        gather(i + 1, (i + 1) % 2)

    slot = i % 2
    pltpu.make_async_copy(_rows(y_hbm, 0, tile_rows[i]), _rows(ybuf.at[slot], 0, tile_rows[i]),
                          sem.at[slot]).wait()
    gp = gp_ref[...]
    tm = gp.shape[0]
    sr = ybuf.shape[1]
    col = lax.broadcasted_iota(jnp.int32, (tm, sr), 1).astype(F32)
    weights = (jnp.where(col == gp[:, 2:3], gp[:, 0:1], 0.0)
               + jnp.where(col == gp[:, 3:4], gp[:, 1:2], 0.0)).astype(BF16)
    y = _dot(weights, ybuf[slot].astype(BF16))
    x2 = x_ref[...] + g2_ref[0] * y
    if final:
        ms = jnp.mean(x2 * x2, axis=-1, keepdims=True)
        x2 = (x2 * lax.rsqrt(ms + EPS)) * fgain_ref[...]
    o_ref[...] = x2


def _combine(tables, yb, x1, g2, gp, fgain, *, tokens_per_batch, final):
    t, d = x1.shape
    tm = min(MOE_TILE, tokens_per_batch)
    nsteps = t // tm
    per_b = tokens_per_batch // tm
    sr = _sorted_rows(tm)
    grid_spec = pltpu.PrefetchScalarGridSpec(
        num_scalar_prefetch=len(tables),
        grid=(nsteps,),
        in_specs=[
            pl.BlockSpec(memory_space=pl.ANY),
            pl.BlockSpec((tm, d), lambda i, *_: (i, 0)),
            pl.BlockSpec((1, 1, d), lambda i, *_: (i // per_b, 0, 0)),
            pl.BlockSpec((tm, LANES), lambda i, *_: (i, 0)),
            pl.BlockSpec((1, d), lambda i, *_: (0, 0)),
        ],
        out_specs=pl.BlockSpec((tm, d), lambda i, *_: (i, 0)),
        scratch_shapes=[pltpu.VMEM((2, sr, d), F32), pltpu.SemaphoreType.DMA((2,))],
    )
    return pl.pallas_call(
        functools.partial(_combine_kernel, nsteps=nsteps, final=final),
        grid_spec=grid_spec,
        out_shape=jax.ShapeDtypeStruct((t, d), F32),
        compiler_params=_params(("arbitrary",)),
        name="combine_final" if final else "combine",
    )(*tables, yb, x1, g2, gp, fgain)


def _fmix_ch_kernel(x_ref, sh_ref, sc_ref, gain_ref, cc_ref, sc_mat_ref, a_ref, b_ref):
    h = _modulate(x_ref[...], gain_ref[...], sh_ref[0], sc_ref[0]).astype(BF16)
    gw = cc_ref.shape[0]
    for g in range(FOURIER_GROUPS):
        hg = h[:, g * gw:(g + 1) * gw]
        a_ref[:, g * gw:(g + 1) * gw] = _dot(hg, cc_ref[...]).astype(BF16)
        b_ref[:, g * gw:(g + 1) * gw] = _dot(hg, sc_mat_ref[...]).astype(BF16)


def _fmix_ch(x, shift, scale, gain, cc, sc, *, tokens_per_batch):
    t, d = x.shape
    tm = min(512, tokens_per_batch)
    per_b = tokens_per_batch // tm
    gw = d // FOURIER_GROUPS
    mod_idx = lambda i: (i // per_b, 0, 0)
    return pl.pallas_call(
        _fmix_ch_kernel,
        grid=(t // tm,),
        in_specs=[
            pl.BlockSpec((tm, d), lambda i: (i, 0)),
            pl.BlockSpec((1, 1, d), mod_idx), pl.BlockSpec((1, 1, d), mod_idx),
            pl.BlockSpec((1, d), lambda i: (0, 0)),
            pl.BlockSpec((gw, gw), lambda i: (0, 0)), pl.BlockSpec((gw, gw), lambda i: (0, 0)),
        ],
        out_specs=[pl.BlockSpec((tm, d), lambda i: (i, 0)), pl.BlockSpec((tm, d), lambda i: (i, 0))],
        out_shape=[jax.ShapeDtypeStruct((t, d), BF16), jax.ShapeDtypeStruct((t, d), BF16)],
        compiler_params=_params(("arbitrary",)),
        name="fmix_ch",
    )(x, shift, scale, gain, cc, sc)


def _fmix_pos_kernel(cn_ref, sn_ref, a_ref, b_ref, o_ref, acc_ref):
    k = pl.program_id(2)

    @pl.when(k == 0)
    def _():
        acc_ref[...] = jnp.zeros_like(acc_ref)

    acc_ref[...] += _dot(cn_ref[...], a_ref[0]) - _dot(sn_ref[...], b_ref[0])

    @pl.when(k == pl.num_programs(2) - 1)
    def _():
        o_ref[0] = acc_ref[...].astype(BF16)


def _fmix_pos(cn, sn, a, b):
    bsz, n, d = a.shape
    tmr = min(1024, n)
    tk = min(512, n)
    return pl.pallas_call(
        _fmix_pos_kernel,
        grid=(n // tmr, bsz, n // tk),
        in_specs=[
            pl.BlockSpec((tmr, tk), lambda i, bi, k: (i, k)),
            pl.BlockSpec((tmr, tk), lambda i, bi, k: (i, k)),
            pl.BlockSpec((1, tk, d), lambda i, bi, k: (bi, k, 0)),
            pl.BlockSpec((1, tk, d), lambda i, bi, k: (bi, k, 0)),
        ],
        out_specs=pl.BlockSpec((1, tmr, d), lambda i, bi, k: (bi, i, 0)),
        out_shape=jax.ShapeDtypeStruct((bsz, n, d), BF16),
        scratch_shapes=[pltpu.VMEM((tmr, d), F32)],
        compiler_params=_params(("arbitrary", "arbitrary", "arbitrary")),
        name="fmix_pos",
    )(cn, sn, a, b)


def _rope_tables(n):
    half = ROPE_AXIS_DIM // 2
    rows = n // GRID_W
    inv = ROPE_THETA ** (-jnp.arange(half, dtype=F32) / half)
    pos = jnp.arange(n, dtype=jnp.int32)
    ang_r = (pos // GRID_W).astype(F32)[:, None] * inv
    ang_c = (pos % GRID_W).astype(F32)[:, None] * inv
    del rows
    cos = jnp.concatenate([jnp.cos(ang_r)] * 2 + [jnp.cos(ang_c)] * 2, axis=-1)
    sin = jnp.concatenate([-jnp.sin(ang_r), jnp.sin(ang_r), -jnp.sin(ang_c), jnp.sin(ang_c)], axis=-1)
    return jnp.tile(cos, (1, LANES // HEAD_DIM)), jnp.tile(sin, (1, LANES // HEAD_DIM))


def _dft_mats(n, scale):
    r = 1
    while r * r < n:
        r *= 2
    c = n // r
    k = jnp.arange(n, dtype=jnp.int32)[:, None]
    ang_a = ((k * jnp.arange(r, dtype=jnp.int32)[None, :] * c) % n).astype(F32) * (2.0 * math.pi / n)
    ang_b = ((k * jnp.arange(c, dtype=jnp.int32)[None, :]) % n).astype(F32) * (2.0 * math.pi / n)
    ca, sa, cb, sb = jnp.cos(ang_a), jnp.sin(ang_a), jnp.cos(ang_b), jnp.sin(ang_b)
    cosm = (ca[:, :, None] * cb[:, None, :] - sa[:, :, None] * sb[:, None, :]).reshape(n, n)
    sinm = (sa[:, :, None] * cb[:, None, :] + ca[:, :, None] * sb[:, None, :]).reshape(n, n)
    return (cosm * scale).astype(BF16), (sinm * scale).astype(BF16)


def _moe(a, w_o, x, g1, g2, gain, shift, scale, w_rg, b_rg, w_re, b_re, w1, w3, w2, fgain, *,
         layer, tokens_per_batch, final):
    t, d = x.shape
    pad = LANES - N_GROUPS - N_EXPERTS
    w_r = jnp.concatenate([w_rg, w_re, jnp.zeros((d, pad), F32)], axis=1)
    b_r = jnp.concatenate([b_rg, b_re, jnp.zeros((pad,), F32)])[None, :]
    wr_hi = w_r.astype(BF16)
    wr_lo = (w_r - wr_hi.astype(F32)).astype(BF16)
    x1, s_tiles, gp, pc = _proj_moe(a, w_o, x, g1, gain, shift, scale, wr_hi, wr_lo, b_r,
                                    tokens_per_batch=tokens_per_batch)
    tm = min(MOE_TILE, tokens_per_batch)
    nt = t // tm
    sr = _sorted_rows(tm)
    bmu = FFN_BLOCK // RUN_ALIGN
    i32 = jnp.int32
    run_len = pc.reshape(nt, 8, LANES)[:, 0, :N_EXPERTS]
    seg_start = jnp.concatenate([jnp.zeros((1, N_EXPERTS), i32), jnp.cumsum(run_len, axis=0)])
    seg_len = seg_start[-1]
    seg_pad = (seg_len + bmu - 1) // bmu * bmu
    exp_end = jnp.cumsum(seg_pad)
    exp_start = exp_end - seg_pad
    run_start = jnp.cumsum(run_len, axis=1) - run_len
    tile_rows = run_start[:, -1] + run_len[:, -1]
    max_units = (2 * t) // RUN_ALIGN + nt * N_EXPERTS + N_EXPERTS * (bmu - 1)
    nblk = (max_units + bmu - 1) // bmu
    blk_start = jnp.arange(nblk, dtype=i32) * bmu
    blk_e = jnp.minimum(jnp.sum((exp_end[None, :] <= blk_start[:, None]).astype(i32), axis=1),
                        N_EXPERTS - 1)
    blk_b0 = blk_start - exp_start[blk_e]
    blk_n = jnp.clip(seg_len[blk_e] - blk_b0, 0, bmu)
    seg_e = seg_start.T[blk_e]
    blk_tlo = jnp.sum((seg_e[:, 1:] <= blk_b0[:, None]).astype(i32), axis=1)
    blk_thi = jnp.sum((seg_e[:, :-1] < (blk_b0 + bmu)[:, None]).astype(i32), axis=1)
    nused = (exp_end[-1] // bmu)[None]
    flat = lambda v: v.reshape(-1).astype(i32)
    yb = _ffn((blk_e, blk_b0, blk_n, blk_tlo, blk_thi, flat(seg_start), flat(run_start), nused),
              s_tiles, w1, w3, w2, layer=layer, nblk=nblk, sr=sr)
    return _combine((flat(run_len), flat(run_start), flat(seg_start), exp_start, tile_rows),
                    yb, x1, g2, gp, fgain, tokens_per_batch=tokens_per_batch, final=final)


def kernel(x, c, ctx, c_ctx, w_mod, b_mod, norm_mix, norm_ffn, attn_w_qkv, attn_q_norm, attn_k_norm,
           attn_w_o, fourier_w_o, moe_w_rg, moe_b_rg, moe_w_re, moe_b_re, moe_w1, moe_w3, moe_w2,
           final_norm):
    bsz, n, d = x.shape
    t = bsz * n
    q_dim = N_HEADS * HEAD_DIM
    kv_dim = N_KV_HEADS * HEAD_DIM

    cond = jnp.zeros((16, d), F32).at[:bsz].set(c).at[bsz].set(c_ctx)
    mods = _ada(cond, w_mod, b_mod)

    def lat(layer, j):
        return mods[layer, :bsz, j * d:(j + 1) * d][:, None, :]

    def ctxm(layer, j):
        return mods[layer, bsz:bsz + 1, j * d:(j + 1) * d][:, None, :]

    w_qkv = attn_w_qkv[0].astype(BF16)
    qscale = (HEAD_DIM ** -0.5) * math.log2(math.e)
    head_gain = jnp.stack([jnp.tile(attn_q_norm[0] * qscale, LANES // HEAD_DIM),
                           jnp.tile(attn_k_norm[0], LANES // HEAD_DIM)])
    lane = jnp.arange(LANES)
    ebd = jnp.where((lane[:, None] // HEAD_DIM) == (lane[None, :] // HEAD_DIM),
                    1.0 / HEAD_DIM, 0.0).astype(BF16)
    cos, sin = _rope_tables(n)
    gain0 = norm_mix[0][None, :]
    q, k_l, v_l = _qkv(x, lat(0, 0), lat(0, 1), gain0, w_qkv, head_gain, ebd, cos, sin,
                       with_q=True, per_batch_mods=True)
    n_ctx = ctx.shape[1]
    k_c, v_c = _qkv(ctx, ctxm(0, 0), ctxm(0, 1), gain0, w_qkv[:, q_dim:], head_gain, ebd,
                    jnp.ones((n_ctx, LANES), F32), jnp.zeros((n_ctx, LANES), F32),
                    with_q=False, per_batch_mods=False)
    del kv_dim
    o = _attention(q, k_l, v_l, k_c, v_c).reshape(t, q_dim)

    xf = x.reshape(t, d)
    moe_w = lambda i: (moe_w_rg[i], moe_b_rg[i], moe_w_re[i], moe_b_re[i],
                       moe_w1, moe_w3, moe_w2)
    fgain = final_norm[None, :]
    x2 = _moe(o, attn_w_o[0].astype(BF16), xf, lat(0, 2), lat(0, 5), norm_ffn[0][None, :],
              lat(0, 3), lat(0, 4), *moe_w(0), fgain, layer=0, tokens_per_batch=n, final=False)

    gw = d // FOURIER_GROUPS
    cc, sc = _dft_mats(gw, gw ** -0.5)
    cn, sn = _dft_mats(n, n ** -0.5)
    a, b = _fmix_ch(x2, lat(1, 0), lat(1, 1), norm_mix[1][None, :], cc, sc, tokens_per_batch=n)
    f = _fmix_pos(cn, sn, a.reshape(bsz, n, d), b.reshape(bsz, n, d)).reshape(t, d)
    out = _moe(f, fourier_w_o[0].astype(BF16), x2, lat(1, 2), lat(1, 5), norm_ffn[1][None, :],
               lat(1, 3), lat(1, 4), *moe_w(1), fgain, layer=1, tokens_per_batch=n, final=True)
    return out.reshape(bsz, n, d)
```

```python
import functools
import math

import jax
import jax.numpy as jnp
from jax import lax
from jax.experimental import pallas as pl
from jax.experimental.pallas import tpu as pltpu

F32 = jnp.float32
BF16 = jnp.bfloat16

N_HEADS = 16
N_KV_HEADS = 4
HEAD_DIM = 64
GQA_GROUP = N_HEADS // N_KV_HEADS
GRID_W = 64
ROPE_AXIS_DIM = HEAD_DIM // 2
ROPE_THETA = 10000.0
FOURIER_GROUPS = 4
N_GROUPS = 4
EXPERTS_PER_GROUP = 8
N_EXPERTS = N_GROUPS * EXPERTS_PER_GROUP
EPS = 1e-6
LANES = 128
NEG_BIG = -1e30
VMEM_LIMIT = 48 * 1024 * 1024

FFN_BLOCK = 512
MOE_TILE = 512
RUN_ALIGN = 8


def _sorted_rows(tile_tokens):
    return 2 * tile_tokens + N_EXPERTS * RUN_ALIGN


def _dot(a, b):
    return jnp.dot(a, b, preferred_element_type=F32)


def _split_bf16(a):
    hi = a.astype(BF16)
    lo = (a - hi.astype(F32)).astype(BF16)
    return hi, lo


def _params(sem):
    return pltpu.CompilerParams(dimension_semantics=sem, vmem_limit_bytes=VMEM_LIMIT)


def _ada_kernel(c_ref, w_ref, b_ref, o_ref):
    c = c_ref[...]
    s = c / (1.0 + jnp.exp(-c))
    s_hi, s_lo = _split_bf16(s)
    w_hi, w_lo = _split_bf16(w_ref[0])
    o_ref[0] = _dot(s_hi, w_hi) + _dot(s_hi, w_lo) + _dot(s_lo, w_hi) + b_ref[0]


def _ada(cond, w_mod, b_mod):
    depth, d, d6 = w_mod.shape
    rows = cond.shape[0]
    tn = 1536
    return pl.pallas_call(
        _ada_kernel,
        grid=(depth, d6 // tn),
        in_specs=[
            pl.BlockSpec((rows, d), lambda l, j: (0, 0)),
            pl.BlockSpec((1, d, tn), lambda l, j: (l, 0, j)),
            pl.BlockSpec((1, 1, tn), lambda l, j: (l, 0, j)),
        ],
        out_specs=pl.BlockSpec((1, rows, tn), lambda l, j: (l, 0, j)),
        out_shape=jax.ShapeDtypeStruct((depth, rows, d6), F32),
        compiler_params=_params(("arbitrary", "arbitrary")),
        name="ada",
    )(cond, w_mod, b_mod.reshape(depth, 1, d6))


def _modulate(xf, gain, shift, scale):
    ms = jnp.mean(xf * xf, axis=-1, keepdims=True)
    return (xf * lax.rsqrt(ms + EPS)) * gain * (1.0 + scale) + shift


def _qkv_kernel(x_ref, sh_ref, sc_ref, gain_ref, w_ref, hg_ref, ebd_ref, cos_ref, sin_ref,
                *out_refs, with_q):
    h = _modulate(x_ref[0], gain_ref[...], sh_ref[0], sc_ref[0])
    qkv = _dot(h.astype(BF16), w_ref[...])
    tm = qkv.shape[0]
    lane = lax.broadcasted_iota(jnp.int32, (tm, LANES), 1)
    first_half = (lane & 16) == 0
    low = lane < HEAD_DIM
    if with_q:
        q_ref, k_ref, v_ref = out_refs
        nq = N_HEADS * HEAD_DIM // LANES
    else:
        k_ref, v_ref = out_refs
        nq = 0
    nk = N_KV_HEADS * HEAD_DIM // LANES
    cos = cos_ref[...]
    sin = sin_ref[...]

    def norm_rope(c, gain_row):
        ms = _dot((c * c).astype(BF16), ebd_ref[...])
        cn = c * lax.rsqrt(ms + EPS) * gain_row
        partner = jnp.where(first_half, pltpu.roll(cn, LANES - 16, 1), pltpu.roll(cn, 16, 1))
        return cn * cos + partner * sin

    for j in range(nq):
        out = norm_rope(qkv[:, j * LANES:(j + 1) * LANES], hg_ref[0:1, :]).astype(BF16)
        q_ref[0, 2 * j] = out[:, :HEAD_DIM]
        q_ref[0, 2 * j + 1] = out[:, HEAD_DIM:]
    for j in range(nk):
        out = norm_rope(qkv[:, (nq + j) * LANES:(nq + j + 1) * LANES], hg_ref[1:2, :]).astype(BF16)
        k_ref[0, 2 * j] = out[:, :HEAD_DIM]
        k_ref[0, 2 * j + 1] = out[:, HEAD_DIM:]
    for j in range(nk):
        c = qkv[:, (nq + nk + j) * LANES:(nq + nk + j + 1) * LANES]
        v_ref[0, 2 * j] = jnp.where(low, c, 1.0).astype(BF16)
        v_ref[0, 2 * j + 1] = jnp.where(low, pltpu.roll(c, HEAD_DIM, 1), 1.0).astype(BF16)


def _qkv(x, shift, scale, gain, w, head_gain, ebd, cos, sin, *, with_q, per_batch_mods):
    b, n, d = x.shape
    tm = min(512, n)
    ncol = w.shape[1]
    mod_idx = (lambda bi, i: (bi, 0, 0)) if per_batch_mods else (lambda bi, i: (0, 0, 0))
    out_shape = []
    out_specs = []
    if with_q:
        out_shape.append(jax.ShapeDtypeStruct((b, N_HEADS, n, HEAD_DIM), BF16))
        out_specs.append(pl.BlockSpec((1, N_HEADS, tm, HEAD_DIM), lambda bi, i: (bi, 0, i, 0)))
    out_shape.append(jax.ShapeDtypeStruct((b, N_KV_HEADS, n, HEAD_DIM), BF16))
    out_specs.append(pl.BlockSpec((1, N_KV_HEADS, tm, HEAD_DIM), lambda bi, i: (bi, 0, i, 0)))
    out_shape.append(jax.ShapeDtypeStruct((b, N_KV_HEADS, n, LANES), BF16))
    out_specs.append(pl.BlockSpec((1, N_KV_HEADS, tm, LANES), lambda bi, i: (bi, 0, i, 0)))
    return pl.pallas_call(
        functools.partial(_qkv_kernel, with_q=with_q),
        grid=(b, n // tm),
        in_specs=[
            pl.BlockSpec((1, tm, d), lambda bi, i: (bi, i, 0)),
            pl.BlockSpec((1, 1, d), mod_idx),
            pl.BlockSpec((1, 1, d), mod_idx),
            pl.BlockSpec((1, d), lambda bi, i: (0, 0)),
            pl.BlockSpec((d, ncol), lambda bi, i: (0, 0)),
            pl.BlockSpec((2, LANES), lambda bi, i: (0, 0)),
            pl.BlockSpec((LANES, LANES), lambda bi, i: (0, 0)),
            pl.BlockSpec((tm, LANES), lambda bi, i: (i, 0)),
            pl.BlockSpec((tm, LANES), lambda bi, i: (i, 0)),
        ],
        out_specs=out_specs,
        out_shape=out_shape,
        compiler_params=_params(("arbitrary", "arbitrary")),
        name="qkv" if with_q else "ctx_kv",
    )(x, shift, scale, gain, w, head_gain, ebd, cos, sin)


def _attn_kernel(q_ref, k_ref, v_ref, kc_ref, vc_ref, o_ref, *, tq, tk):
    m_rows = GQA_GROUP * tq
    q = q_ref[0].reshape(m_rows, HEAD_DIM)
    m = jnp.full((m_rows, 1), -jnp.inf, F32)
    acc = jnp.zeros((m_rows, LANES), F32)
    chunks = []
    for kr, vr in ((k_ref, v_ref), (kc_ref, vc_ref)):
        size = min(tk, kr.shape[2])
        chunks += [(kr, vr, c, size) for c in range(kr.shape[2] // size)]
    for kr, vr, c, size in chunks:
        kc = kr[0, 0, c * size:(c + 1) * size, :]
        vc = vr[0, 0, c * size:(c + 1) * size, :]
        s = lax.dot_general(q, kc, (((1,), (1,)), ((), ())), preferred_element_type=F32)
        m_new = jnp.maximum(m, jnp.max(s, axis=-1, keepdims=True))
        alpha = jnp.exp2(m - m_new)
        p = jnp.exp2(s - m_new)
        acc = alpha * acc + _dot(p.astype(BF16), vc)
        m = m_new
    o = acc * (1.0 / pltpu.roll(acc, HEAD_DIM, 1))
    o_ref[0] = jnp.concatenate(
        [o[g * tq:(g + 1) * tq, :HEAD_DIM] for g in range(GQA_GROUP)], axis=-1).astype(BF16)


def _attention(q, k, v, k_ctx, v_ctx):
    b, _, n, _ = q.shape
    n_ctx = k_ctx.shape[2]
    tq = min(256, n)
    tk = 256
    assert n % min(tk, n) == 0 and n_ctx % min(tk, n_ctx) == 0
    kv_spec = lambda rows, width: pl.BlockSpec((1, 1, rows, width), lambda bi, kh, qi: (bi, kh, 0, 0))
    return pl.pallas_call(
        functools.partial(_attn_kernel, tq=tq, tk=tk),
        grid=(b, N_KV_HEADS, n // tq),
        in_specs=[
            pl.BlockSpec((1, GQA_GROUP, tq, HEAD_DIM), lambda bi, kh, qi: (bi, kh, qi, 0)),
            kv_spec(n, HEAD_DIM), kv_spec(n, LANES), kv_spec(n_ctx, HEAD_DIM), kv_spec(n_ctx, LANES),
        ],
        out_specs=pl.BlockSpec((1, tq, GQA_GROUP * HEAD_DIM), lambda bi, kh, qi: (bi, qi, kh)),
        out_shape=jax.ShapeDtypeStruct((b, n, N_HEADS * HEAD_DIM), BF16),
        compiler_params=_params(("arbitrary", "arbitrary", "arbitrary")),
        name="attn",
    )(q, k, v, k_ctx, v_ctx)


def _proj_moe_kernel(*refs, tiles_per_batch, folded):
    if folded:
        dif_ref, sum_ref, sumb_ref, *refs, a_scr = refs
        j = pl.program_id(0) % tiles_per_batch

        @pl.when(j < tiles_per_batch // 2)
        def _():
            a_scr[...] = dif_ref[0]

        @pl.when(j >= tiles_per_batch // 2)
        def _():
            rows = _dot(_reversal(a_scr.shape[0]), sum_ref[0])
            row = lax.broadcasted_iota(jnp.int32, rows.shape, 0)
            a_scr[...] = jnp.where(row == 0, sumb_ref[0, 0:1, :].astype(F32), rows).astype(BF16)

        a = a_scr[...]
    else:
        a_ref, *refs = refs
        a = a_ref[...]
    (wo_ref, x_ref, g1_ref, gain_ref, sh_ref, sc_ref, wrh_ref, wrl_ref, br_ref,
     x1_ref, s_ref, gp_ref, pc_ref) = refs
    x1 = x_ref[...] + g1_ref[0] * _dot(a, wo_ref[...])
    x1_ref[...] = x1
    h2 = _modulate(x1, gain_ref[...], sh_ref[0], sc_ref[0])
    h_hi, h_lo = _split_bf16(h2)
    logits = (_dot(h_hi, wrh_ref[...]) + _dot(h_lo, wrh_ref[...]) + _dot(h_hi, wrl_ref[...])
              + br_ref[...])
    tm = logits.shape[0]
    lane = lax.broadcasted_iota(jnp.int32, (tm, LANES), 1)
    lane_f = lane.astype(F32)

    def top(vals):
        best = jnp.max(vals, axis=-1, keepdims=True)
        idx = jnp.min(jnp.where(vals == best, lane_f, float(LANES)), axis=-1, keepdims=True)
        return best, idx

    lg = jnp.where(lane < N_GROUPS, logits, NEG_BIG)
    g_max, g_sel = top(lg)
    p_grp = 1.0 / jnp.sum(jnp.where(lane < N_GROUPS, jnp.exp(lg - g_max), 0.0), axis=-1, keepdims=True)
    lo = N_GROUPS + EXPERTS_PER_GROUP * g_sel
    le = jnp.where((lane_f >= lo) & (lane_f < lo + EXPERTS_PER_GROUP), logits, NEG_BIG)
    t1, i1 = top(le)
    t2, i2 = top(jnp.where(lane_f == i1, NEG_BIG, le))
    e21 = jnp.exp(t2 - t1)
    w1 = 1.0 / (1.0 + e21)
    gate0 = p_grp * w1
    gate1 = p_grp * (e21 * w1)
    e0 = i1 - N_GROUPS
    e1 = i2 - N_GROUPS

    onehot = jnp.where((lane_f == e0) | (lane_f == e1), 1.0, 0.0)
    row = lax.broadcasted_iota(jnp.int32, (tm, tm), 0)
    col = lax.broadcasted_iota(jnp.int32, (tm, tm), 1)
    lower = jnp.where(col < row, 1.0, 0.0).astype(BF16)
    before = _dot(lower, onehot.astype(BF16))
    cnt = jnp.sum(onehot, axis=0, keepdims=True)
    pc = jnp.floor((cnt + (RUN_ALIGN - 1)) * (1.0 / RUN_ALIGN))
    urow = lax.broadcasted_iota(jnp.int32, (LANES, LANES), 0)
    ucol = lax.broadcasted_iota(jnp.int32, (LANES, LANES), 1)
    upper = jnp.where(urow < ucol, 1.0, 0.0).astype(BF16)
    run_start = _dot(jnp.broadcast_to(pc, (8, LANES)).astype(BF16), upper)[0:1] * RUN_ALIGN
    base = run_start + before
    pos0 = jnp.sum(jnp.where(lane_f == e0, base, 0.0), axis=-1, keepdims=True)
    pos1 = jnp.sum(jnp.where(lane_f == e1, base, 0.0), axis=-1, keepdims=True)
    gp = jnp.where(lane == 0, gate0, jnp.where(lane == 1, gate1, jnp.where(lane == 2, pos0,
                                                                       jnp.where(lane == 3, pos1, 0.0))))
    gp_ref[...] = gp
    pc_ref[...] = jnp.broadcast_to(pc, (8, LANES)).astype(jnp.int32)

    pos_t = jnp.where(lane == 0, pos0, jnp.where(lane == 1, pos1, -1.0)).T
    sr = s_ref.shape[0]
    srow = lax.broadcasted_iota(jnp.int32, (sr, tm), 0).astype(F32)
    perm = jnp.where((srow == pos_t[0:1, :]) | (srow == pos_t[1:2, :]), 1.0, 0.0).astype(BF16)
    s_ref[...] = _dot(perm, h_hi)


def _proj_moe(a, w_o, x, g1, gain, shift, scale, wr_hi, wr_lo, b_r, *, tokens_per_batch):
    t, d = x.shape
    tm = min(MOE_TILE, tokens_per_batch)
    nt = t // tm
    sr = _sorted_rows(tm)
    per_b = tokens_per_batch // tm
    mod_idx = lambda i: (i // per_b, 0, 0)
    row_spec = lambda w: pl.BlockSpec((tm, w), lambda i: (i, 0))
    full = lambda s: pl.BlockSpec(s, lambda i: (0,) * len(s))
    folded = isinstance(a, tuple)
    if folded:
        assert per_b % 2 == 0
        half = per_b // 2
        a_inputs = (a[0], a[1], a[1])
        a_specs = [
            pl.BlockSpec((1, tm, d), lambda i: (i // per_b, jnp.minimum(i % per_b, half - 1), 0)),
            pl.BlockSpec((1, tm, d), lambda i: (i // per_b, jnp.clip(per_b - 1 - i % per_b, 0, half - 1), 0)),
            pl.BlockSpec((1, 8, d), lambda i: (i // per_b, (tm // 8) * jnp.clip(per_b - i % per_b, 1, half), 0)),
        ]
        scratch = [pltpu.VMEM((tm, d), BF16)]
    else:
        a_inputs = (a,)
        a_specs = [row_spec(d)]
        scratch = []
    return pl.pallas_call(
        functools.partial(_proj_moe_kernel, tiles_per_batch=per_b, folded=folded),
        grid=(nt,),
        in_specs=a_specs + [
            full((d, d)), row_spec(d),
            pl.BlockSpec((1, 1, d), mod_idx), full((1, d)),
            pl.BlockSpec((1, 1, d), mod_idx), pl.BlockSpec((1, 1, d), mod_idx),
            full((d, LANES)), full((d, LANES)), full((1, LANES)),
        ],
        out_specs=[row_spec(d), pl.BlockSpec((sr, d), lambda i: (i, 0)), row_spec(LANES),
                   pl.BlockSpec((8, LANES), lambda i: (i, 0))],
        out_shape=[
            jax.ShapeDtypeStruct((t, d), F32), jax.ShapeDtypeStruct((nt * sr, d), F32),
            jax.ShapeDtypeStruct((t, LANES), F32), jax.ShapeDtypeStruct((nt * 8, LANES), jnp.int32),
        ],
        scratch_shapes=scratch,
        compiler_params=_params(("arbitrary",)),
        name="proj_moe",
    )(*a_inputs, w_o, x, g1, gain, shift, scale, wr_hi, wr_lo, b_r)


def _rows(ref, start_units, n_units):
    return ref.at[pl.ds(pl.multiple_of(start_units * RUN_ALIGN, RUN_ALIGN), n_units * RUN_ALIGN)]


def _ffn_kernel(blk_e, blk_b0, blk_n, blk_tlo, blk_thi, seg_start, run_start, nused_ref,
                s_hbm, w1_ref, w3_ref, w2_ref, y_ref, xg, w1b, w3b, w2b, sem, *, sr_units):
    i = pl.program_id(0)
    nused = nused_ref[0]
    bm_units = xg.shape[1] // RUN_ALIGN

    def gather(b, slot):
        e = blk_e[b]
        b0 = blk_b0[b]

        def body(ti, carry):
            rs = seg_start[ti * N_EXPERTS + e]
            re = seg_start[(ti + 1) * N_EXPERTS + e]
            lo = jnp.maximum(rs, b0)
            n = jnp.minimum(re, b0 + bm_units) - lo

            @pl.when(n > 0)
            def _():
                src = ti * sr_units + run_start[ti * N_EXPERTS + e] + (lo - rs)
                pltpu.make_async_copy(_rows(s_hbm, src, n), _rows(xg.at[slot], lo - b0, n),
                                      sem.at[slot]).start()

            return carry

        lax.fori_loop(blk_tlo[b], blk_thi[b], body, 0)

    @pl.when(i == 0)
    def _():
        xg[...] = jnp.zeros_like(xg)
        gather(0, 0)

    @pl.when(i + 1 < nused)
    def _():
        gather(i + 1, (i + 1) % 2)

    @pl.when(i < nused)
    def _():
        slot = i % 2
        pltpu.make_async_copy(_rows(s_hbm, 0, blk_n[i]), _rows(xg.at[slot], 0, blk_n[i]),
                              sem.at[slot]).wait()

        @pl.when((i == 0) | (blk_e[i] != blk_e[jnp.maximum(i - 1, 0)]))
        def _():
            w1b[...] = w1_ref[0, 0].astype(BF16)
            w3b[...] = w3_ref[0, 0].astype(BF16)
            w2b[...] = w2_ref[0, 0].astype(BF16)

        xb = xg[slot].astype(BF16)
        a = _dot(xb, w1b[...])
        b = _dot(xb, w3b[...])
        mid = (a / (1.0 + jnp.exp(-a))) * b
        y_ref[...] = _dot(mid.astype(BF16), w2b[...])

    @pl.when(i >= nused)
    def _():
        y_ref[...] = jnp.zeros_like(y_ref)


def _ffn(tables, s_tiles, w1, w3, w2, *, layer, nblk, sr):
    bm = FFN_BLOCK
    d = s_tiles.shape[1]
    de = w1.shape[3]
    nsp = len(tables)
    grid_spec = pltpu.PrefetchScalarGridSpec(
        num_scalar_prefetch=nsp,
        grid=(nblk,),
        in_specs=[
            pl.BlockSpec(memory_space=pl.ANY),
            pl.BlockSpec((1, 1, d, de), lambda i, be, *_: (layer, be[i], 0, 0)),
            pl.BlockSpec((1, 1, d, de), lambda i, be, *_: (layer, be[i], 0, 0)),
            pl.BlockSpec((1, 1, de, d), lambda i, be, *_: (layer, be[i], 0, 0)),
        ],
        out_specs=pl.BlockSpec((bm, d), lambda i, *_: (i, 0)),
        scratch_shapes=[pltpu.VMEM((2, bm, d), F32), pltpu.VMEM((d, de), BF16), pltpu.VMEM((d, de), BF16),
                        pltpu.VMEM((de, d), BF16), pltpu.SemaphoreType.DMA((2,))],
    )
    return pl.pallas_call(
        functools.partial(_ffn_kernel, sr_units=sr // RUN_ALIGN),
        grid_spec=grid_spec,
        out_shape=jax.ShapeDtypeStruct((nblk * bm, d), F32),
        compiler_params=_params(("arbitrary",)),
        name="ffn",
    )(*tables, s_tiles, w1, w3, w2)


def _combine_kernel(run_len, run_start, seg_start, exp_start, tile_rows,
                    y_hbm, x_ref, g2_ref, gp_ref, fgain_ref, o_ref, ybuf, sem, *, nsteps, final):
    i = pl.program_id(0)

    def gather(ti, slot):
        for e in range(N_EXPERTS):
            n = run_len[ti * N_EXPERTS + e]

            @pl.when(n > 0)
            def _():
                src = exp_start[e] + seg_start[ti * N_EXPERTS + e]
                pltpu.make_async_copy(_rows(y_hbm, src, n),
                                      _rows(ybuf.at[slot], run_start[ti * N_EXPERTS + e], n),
                                      sem.at[slot]).start()

    @pl.when(i == 0)
    def _():
        ybuf[...] = jnp.zeros_like(ybuf)
        gather(0, 0)

    @pl.when(i + 1 < nsteps)
    def _():
        gather(i + 1, (i + 1) % 2)

    slot = i % 2
    pltpu.make_async_copy(_rows(y_hbm, 0, tile_rows[i]), _rows(ybuf.at[slot], 0, tile_rows[i]),
                          sem.at[slot]).wait()
    gp = gp_ref[...]
    tm = gp.shape[0]
    sr = ybuf.shape[1]
    col = lax.broadcasted_iota(jnp.int32, (tm, sr), 1).astype(F32)
    weights = (jnp.where(col == gp[:, 2:3], gp[:, 0:1], 0.0)
               + jnp.where(col == gp[:, 3:4], gp[:, 1:2], 0.0)).astype(BF16)
    y = _dot(weights, ybuf[slot].astype(BF16))
    x2 = x_ref[...] + g2_ref[0] * y
    if final:
        ms = jnp.mean(x2 * x2, axis=-1, keepdims=True)
        x2 = (x2 * lax.rsqrt(ms + EPS)) * fgain_ref[...]
    o_ref[...] = x2


def _combine(tables, yb, x1, g2, gp, fgain, *, tokens_per_batch, final):
    t, d = x1.shape
    tm = min(MOE_TILE, tokens_per_batch)
    nsteps = t // tm
    per_b = tokens_per_batch // tm
    sr = _sorted_rows(tm)
    grid_spec = pltpu.PrefetchScalarGridSpec(
        num_scalar_prefetch=len(tables),
        grid=(nsteps,),
        in_specs=[
            pl.BlockSpec(memory_space=pl.ANY),
            pl.BlockSpec((tm, d), lambda i, *_: (i, 0)),
            pl.BlockSpec((1, 1, d), lambda i, *_: (i // per_b, 0, 0)),
            pl.BlockSpec((tm, LANES), lambda i, *_: (i, 0)),
            pl.BlockSpec((1, d), lambda i, *_: (0, 0)),
        ],
        out_specs=pl.BlockSpec((tm, d), lambda i, *_: (i, 0)),
        scratch_shapes=[pltpu.VMEM((2, sr, d), F32), pltpu.SemaphoreType.DMA((2,))],
    )
    return pl.pallas_call(
        functools.partial(_combine_kernel, nsteps=nsteps, final=final),
        grid_spec=grid_spec,
        out_shape=jax.ShapeDtypeStruct((t, d), F32),
        compiler_params=_params(("arbitrary",)),
        name="combine_final" if final else "combine",
    )(*tables, yb, x1, g2, gp, fgain)


def _reversal(size):
    row = lax.broadcasted_iota(jnp.int32, (size, size), 0)
    col = lax.broadcasted_iota(jnp.int32, (size, size), 1)
    return jnp.where(col == size - row, 1.0, 0.0).astype(BF16)


def _fmix_fold_kernel(x_ref, xm_ref, xb_ref, xc_ref, sh_ref, sc_ref, gain_ref, cc_ref, sc_mat_ref,
                      a_ref, b_ref, amid_ref):
    mod = lambda v: _modulate(v, gain_ref[...], sh_ref[0], sc_ref[0])
    h = mod(x_ref[...])
    ft = h.shape[0]
    m_hi, m_lo = _split_bf16(mod(xm_ref[...]))
    rev = _reversal(ft)
    row = lax.broadcasted_iota(jnp.int32, h.shape, 0)
    hr = jnp.where(row == 0, mod(xb_ref[...])[0:1], _dot(rev, m_hi) + _dot(rev, m_lo))
    he = (h + hr).astype(BF16)
    ho = (h - hr).astype(BF16)
    hc = mod(xc_ref[...]).astype(BF16)
    gw = cc_ref.shape[0]
    for g in range(FOURIER_GROUPS):
        cols = slice(g * gw, (g + 1) * gw)
        a_ref[:, cols] = _dot(he[:, cols], cc_ref[...]).astype(BF16)
        b_ref[:, cols] = _dot(ho[:, cols], sc_mat_ref[...]).astype(BF16)
        amid_ref[:, cols] = _dot(hc[:, cols], cc_ref[...])


def _fmix_fold(x, shift, scale, gain, cc, sc, *, bsz, n):
    t, d = x.shape
    hn = n // 2
    ft = min(512, hn)
    tpb = n // ft
    half = tpb // 2
    gw = d // FOURIER_GROUPS
    mod_idx = lambda b, j: (b, 0, 0)
    return pl.pallas_call(
        _fmix_fold_kernel,
        grid=(bsz, half),
        in_specs=[
            pl.BlockSpec((ft, d), lambda b, j: (b * tpb + j, 0)),
            pl.BlockSpec((ft, d), lambda b, j: (b * tpb + tpb - 1 - j, 0)),
            pl.BlockSpec((8, d), lambda b, j: (b * (n // 8) + (ft // 8) * ((tpb - j) % tpb), 0)),
            pl.BlockSpec((8, d), lambda b, j: (b * (n // 8) + hn // 8, 0)),
            pl.BlockSpec((1, 1, d), mod_idx), pl.BlockSpec((1, 1, d), mod_idx),
            pl.BlockSpec((1, d), lambda b, j: (0, 0)),
            pl.BlockSpec((gw, gw), lambda b, j: (0, 0)), pl.BlockSpec((gw, gw), lambda b, j: (0, 0)),
        ],
        out_specs=[pl.BlockSpec((ft, d), lambda b, j: (b * half + j, 0)),
                   pl.BlockSpec((ft, d), lambda b, j: (b * half + j, 0)),
                   pl.BlockSpec((8, d), lambda b, j: (b, 0))],
        out_shape=[jax.ShapeDtypeStruct((bsz * hn, d), BF16), jax.ShapeDtypeStruct((bsz * hn, d), BF16),
                   jax.ShapeDtypeStruct((bsz * 8, d), F32)],
        compiler_params=_params(("arbitrary", "arbitrary")),
        name="fmix_fold",
    )(x, x, x, x, shift, scale, gain, cc, sc)


def _fmix_pos_kernel(ch_ref, sh_ref, sign_ref, a_ref, b_ref, amid_ref, dif_ref, sum_ref, p_acc, q_acc):
    k = pl.program_id(2)

    @pl.when(k == 0)
    def _():
        p_acc[...] = jnp.zeros_like(p_acc)
        q_acc[...] = jnp.zeros_like(q_acc)

    p_acc[...] += _dot(ch_ref[...], a_ref[0])
    q_acc[...] += _dot(sh_ref[...], b_ref[0])

    @pl.when(k == pl.num_programs(2) - 1)
    def _():
        p = p_acc[...] + sign_ref[...] * amid_ref[0, 0:1, :]
        dif_ref[0] = (p - q_acc[...]).astype(BF16)
        sum_ref[0] = (p + q_acc[...]).astype(BF16)


def _fmix_pos(ch, sh, sign, a, b, amid):
    bsz, hn, d = a.shape
    mp = ch.shape[0]
    tmr = mp // 2
    tk = min(512, hn)
    out = jax.ShapeDtypeStruct((bsz, mp, d), BF16)
    return pl.pallas_call(
        _fmix_pos_kernel,
        grid=(mp // tmr, bsz, hn // tk),
        in_specs=[
            pl.BlockSpec((tmr, tk), lambda i, bi, k: (i, k)),
            pl.BlockSpec((tmr, tk), lambda i, bi, k: (i, k)),
            pl.BlockSpec((tmr, 1), lambda i, bi, k: (i, 0)),
            pl.BlockSpec((1, tk, d), lambda i, bi, k: (bi, k, 0)),
            pl.BlockSpec((1, tk, d), lambda i, bi, k: (bi, k, 0)),
            pl.BlockSpec((1, 8, d), lambda i, bi, k: (bi, 0, 0)),
        ],
        out_specs=[pl.BlockSpec((1, tmr, d), lambda i, bi, k: (bi, i, 0)),
                   pl.BlockSpec((1, tmr, d), lambda i, bi, k: (bi, i, 0))],
        out_shape=[out, out],
        scratch_shapes=[pltpu.VMEM((tmr, d), F32), pltpu.VMEM((tmr, d), F32)],
        compiler_params=_params(("arbitrary", "arbitrary", "arbitrary")),
        name="fmix_pos",
    )(ch, sh, sign, a, b, amid)


def _rope_tables(n):
    half = ROPE_AXIS_DIM // 2
    rows = n // GRID_W
    inv = ROPE_THETA ** (-jnp.arange(half, dtype=F32) / half)
    pos = jnp.arange(n, dtype=jnp.int32)
    ang_r = (pos // GRID_W).astype(F32)[:, None] * inv
    ang_c = (pos % GRID_W).astype(F32)[:, None] * inv
    del rows
    cos = jnp.concatenate([jnp.cos(ang_r)] * 2 + [jnp.cos(ang_c)] * 2, axis=-1)
    sin = jnp.concatenate([-jnp.sin(ang_r), jnp.sin(ang_r), -jnp.sin(ang_c), jnp.sin(ang_c)], axis=-1)
    return jnp.tile(cos, (1, LANES // HEAD_DIM)), jnp.tile(sin, (1, LANES // HEAD_DIM))


def _dft_mats(n, scale):
    r = 1
    while r * r < n:
        r *= 2
    c = n // r
    k = jnp.arange(n, dtype=jnp.int32)[:, None]
    ang_a = ((k * jnp.arange(r, dtype=jnp.int32)[None, :] * c) % n).astype(F32) * (2.0 * math.pi / n)
    ang_b = ((k * jnp.arange(c, dtype=jnp.int32)[None, :]) % n).astype(F32) * (2.0 * math.pi / n)
    ca, sa, cb, sb = jnp.cos(ang_a), jnp.sin(ang_a), jnp.cos(ang_b), jnp.sin(ang_b)
    cosm = (ca[:, :, None] * cb[:, None, :] - sa[:, :, None] * sb[:, None, :]).reshape(n, n)
    sinm = (sa[:, :, None] * cb[:, None, :] + ca[:, :, None] * sb[:, None, :]).reshape(n, n)
    return (cosm * scale).astype(BF16), (sinm * scale).astype(BF16)


DFT_ROW_PAD = 32


def _dft_half(n):
    hn = n // 2
    mp = hn + DFT_ROW_PAD
    r = 1
    while r * r < hn:
        r *= 2
    c = hn // r
    k = jnp.arange(mp, dtype=jnp.int32)[:, None]
    valid = k <= hn
    w = 2.0 * math.pi / n
    ang_a = ((k * (jnp.arange(r, dtype=jnp.int32)[None, :] * c)) % n).astype(F32) * w
    ang_b = ((k * jnp.arange(c, dtype=jnp.int32)[None, :]) % n).astype(F32) * w
    ca, sa, cb, sb = jnp.cos(ang_a), jnp.sin(ang_a), jnp.cos(ang_b), jnp.sin(ang_b)
    cosm = (ca[:, :, None] * cb[:, None, :] - sa[:, :, None] * sb[:, None, :]).reshape(mp, hn)
    sinm = (sa[:, :, None] * cb[:, None, :] + ca[:, :, None] * sb[:, None, :]).reshape(mp, hn)
    scale = n ** -0.5
    col_w = jnp.where(jnp.arange(hn)[None, :] == 0, 0.5 * scale, scale)
    cosm = jnp.where(valid, cosm * col_w, 0.0).astype(BF16)
    sinm = jnp.where(valid, sinm * scale, 0.0).astype(BF16)
    sign = jnp.where(valid, jnp.where(k % 2 == 0, scale, -scale), 0.0).astype(F32)
    return cosm, sinm, sign


def _moe(a, w_o, x, g1, g2, gain, shift, scale, w_rg, b_rg, w_re, b_re, w1, w3, w2, fgain, *,
         layer, tokens_per_batch, final):
    t, d = x.shape
    pad = LANES - N_GROUPS - N_EXPERTS
    w_r = jnp.concatenate([w_rg, w_re, jnp.zeros((d, pad), F32)], axis=1)
    b_r = jnp.concatenate([b_rg, b_re, jnp.zeros((pad,), F32)])[None, :]
    wr_hi = w_r.astype(BF16)
    wr_lo = (w_r - wr_hi.astype(F32)).astype(BF16)
    x1, s_tiles, gp, pc = _proj_moe(a, w_o, x, g1, gain, shift, scale, wr_hi, wr_lo, b_r,
                                    tokens_per_batch=tokens_per_batch)
    tm = min(MOE_TILE, tokens_per_batch)
    nt = t // tm
    sr = _sorted_rows(tm)
    bmu = FFN_BLOCK // RUN_ALIGN
    i32 = jnp.int32
    run_len = pc.reshape(nt, 8, LANES)[:, 0, :N_EXPERTS]
    seg_start = jnp.concatenate([jnp.zeros((1, N_EXPERTS), i32), jnp.cumsum(run_len, axis=0)])
    seg_len = seg_start[-1]
    seg_pad = (seg_len + bmu - 1) // bmu * bmu
    exp_end = jnp.cumsum(seg_pad)
    exp_start = exp_end - seg_pad
    run_start = jnp.cumsum(run_len, axis=1) - run_len
    tile_rows = run_start[:, -1] + run_len[:, -1]
    max_units = (2 * t) // RUN_ALIGN + nt * N_EXPERTS + N_EXPERTS * (bmu - 1)
    nblk = (max_units + bmu - 1) // bmu
    blk_start = jnp.arange(nblk, dtype=i32) * bmu
    blk_e = jnp.minimum(jnp.sum((exp_end[None, :] <= blk_start[:, None]).astype(i32), axis=1),
                        N_EXPERTS - 1)
    blk_b0 = blk_start - exp_start[blk_e]
    blk_n = jnp.clip(seg_len[blk_e] - blk_b0, 0, bmu)
    seg_e = seg_start.T[blk_e]
    blk_tlo = jnp.sum((seg_e[:, 1:] <= blk_b0[:, None]).astype(i32), axis=1)
    blk_thi = jnp.sum((seg_e[:, :-1] < (blk_b0 + bmu)[:, None]).astype(i32), axis=1)
    nused = (exp_end[-1] // bmu)[None]
    flat = lambda v: v.reshape(-1).astype(i32)
    yb = _ffn((blk_e, blk_b0, blk_n, blk_tlo, blk_thi, flat(seg_start), flat(run_start), nused),
              s_tiles, w1, w3, w2, layer=layer, nblk=nblk, sr=sr)
    return _combine((flat(run_len), flat(run_start), flat(seg_start), exp_start, tile_rows),
                    yb, x1, g2, gp, fgain, tokens_per_batch=tokens_per_batch, final=final)


def kernel(x, c, ctx, c_ctx, w_mod, b_mod, norm_mix, norm_ffn, attn_w_qkv, attn_q_norm, attn_k_norm,
           attn_w_o, fourier_w_o, moe_w_rg, moe_b_rg, moe_w_re, moe_b_re, moe_w1, moe_w3, moe_w2,
           final_norm):
    bsz, n, d = x.shape
    t = bsz * n
    q_dim = N_HEADS * HEAD_DIM
    kv_dim = N_KV_HEADS * HEAD_DIM

    cond = jnp.zeros((16, d), F32).at[:bsz].set(c).at[bsz].set(c_ctx)
    mods = _ada(cond, w_mod, b_mod)

    def lat(layer, j):
        return mods[layer, :bsz, j * d:(j + 1) * d][:, None, :]

    def ctxm(layer, j):
        return mods[layer, bsz:bsz + 1, j * d:(j + 1) * d][:, None, :]

    w_qkv = attn_w_qkv[0].astype(BF16)
    qscale = (HEAD_DIM ** -0.5) * math.log2(math.e)
    head_gain = jnp.stack([jnp.tile(attn_q_norm[0] * qscale, LANES // HEAD_DIM),
                           jnp.tile(attn_k_norm[0], LANES // HEAD_DIM)])
    lane = jnp.arange(LANES)
    ebd = jnp.where((lane[:, None] // HEAD_DIM) == (lane[None, :] // HEAD_DIM),
                    1.0 / HEAD_DIM, 0.0).astype(BF16)
    cos, sin = _rope_tables(n)
    gain0 = norm_mix[0][None, :]
    q, k_l, v_l = _qkv(x, lat(0, 0), lat(0, 1), gain0, w_qkv, head_gain, ebd, cos, sin,
                       with_q=True, per_batch_mods=True)
    n_ctx = ctx.shape[1]
    k_c, v_c = _qkv(ctx, ctxm(0, 0), ctxm(0, 1), gain0, w_qkv[:, q_dim:], head_gain, ebd,
                    jnp.ones((n_ctx, LANES), F32), jnp.zeros((n_ctx, LANES), F32),
                    with_q=False, per_batch_mods=False)
    del kv_dim
    o = _attention(q, k_l, v_l, k_c, v_c).reshape(t, q_dim)

    xf = x.reshape(t, d)
    moe_w = lambda i: (moe_w_rg[i], moe_b_rg[i], moe_w_re[i], moe_b_re[i],
                       moe_w1, moe_w3, moe_w2)
    fgain = final_norm[None, :]
    x2 = _moe(o, attn_w_o[0].astype(BF16), xf, lat(0, 2), lat(0, 5), norm_ffn[0][None, :],
              lat(0, 3), lat(0, 4), *moe_w(0), fgain, layer=0, tokens_per_batch=n, final=False)

    gw = d // FOURIER_GROUPS
    cc, sc = _dft_mats(gw, gw ** -0.5)
    ch, sh, sign = _dft_half(n)
    a_e, b_o, a_mid = _fmix_fold(x2, lat(1, 0), lat(1, 1), norm_mix[1][None, :], cc, sc, bsz=bsz, n=n)
    f = tuple(_fmix_pos(ch, sh, sign, a_e.reshape(bsz, n // 2, d), b_o.reshape(bsz, n // 2, d),
                        a_mid.reshape(bsz, 8, d)))
    out = _moe(f, fourier_w_o[0].astype(BF16), x2, lat(1, 2), lat(1, 5), norm_ffn[1][None, :],
               lat(1, 3), lat(1, 4), *moe_w(1), fgain, layer=1, tokens_per_batch=n, final=True)
    return out.reshape(bsz, n, d)
```

```python
import functools
import math

import jax
import jax.numpy as jnp
from jax import lax
from jax.experimental import pallas as pl
from jax.experimental.pallas import tpu as pltpu

F32 = jnp.float32
BF16 = jnp.bfloat16

N_HEADS = 16
N_KV_HEADS = 4
HEAD_DIM = 64
GQA_GROUP = N_HEADS // N_KV_HEADS
GRID_W = 64
ROPE_AXIS_DIM = HEAD_DIM // 2
ROPE_THETA = 10000.0
FOURIER_GROUPS = 4
N_GROUPS = 4
EXPERTS_PER_GROUP = 8
N_EXPERTS = N_GROUPS * EXPERTS_PER_GROUP
EPS = 1e-6
LANES = 128
NEG_BIG = -1e30
VMEM_LIMIT = 48 * 1024 * 1024

FFN_BLOCK = 512
MOE_TILE = 512
RUN_ALIGN = 8


def _sorted_rows(tile_tokens):
    return 2 * tile_tokens + N_EXPERTS * RUN_ALIGN


def _dot(a, b):
    return jnp.dot(a, b, preferred_element_type=F32)


def _split_bf16(a):
    hi = a.astype(BF16)
    lo = (a - hi.astype(F32)).astype(BF16)
    return hi, lo


def _params(sem):
    return pltpu.CompilerParams(dimension_semantics=sem, vmem_limit_bytes=VMEM_LIMIT)


def _ada_kernel(c_ref, w_ref, b_ref, o_ref):
    c = c_ref[...]
    s = c / (1.0 + jnp.exp(-c))
    s_hi, s_lo = _split_bf16(s)
    w_hi, w_lo = _split_bf16(w_ref[0])
    o_ref[0] = _dot(s_hi, w_hi) + _dot(s_hi, w_lo) + _dot(s_lo, w_hi) + b_ref[0]


def _ada(cond, w_mod, b_mod):
    depth, d, d6 = w_mod.shape
    rows = cond.shape[0]
    tn = 1536
    return pl.pallas_call(
        _ada_kernel,
        grid=(depth, d6 // tn),
        in_specs=[
            pl.BlockSpec((rows, d), lambda l, j: (0, 0)),
            pl.BlockSpec((1, d, tn), lambda l, j: (l, 0, j)),
            pl.BlockSpec((1, 1, tn), lambda l, j: (l, 0, j)),
        ],
        out_specs=pl.BlockSpec((1, rows, tn), lambda l, j: (l, 0, j)),
        out_shape=jax.ShapeDtypeStruct((depth, rows, d6), F32),
        compiler_params=_params(("arbitrary", "arbitrary")),
        name="ada",
    )(cond, w_mod, b_mod.reshape(depth, 1, d6))


def _modulate(xf, gain, shift, scale):
    ms = jnp.mean(xf * xf, axis=-1, keepdims=True)
    return (xf * lax.rsqrt(ms + EPS)) * gain * (1.0 + scale) + shift


def _qkv_kernel(x_ref, sh_ref, sc_ref, gain_ref, w_ref, hg_ref, ebd_ref, cos_ref, sin_ref,
                *out_refs, with_q):
    h = _modulate(x_ref[0], gain_ref[...], sh_ref[0], sc_ref[0])
    qkv = _dot(h.astype(BF16), w_ref[...])
    tm = qkv.shape[0]
    lane = lax.broadcasted_iota(jnp.int32, (tm, LANES), 1)
    first_half = (lane & 16) == 0
    low = lane < HEAD_DIM
    if with_q:
        q_ref, k_ref, v_ref = out_refs
        nq = N_HEADS * HEAD_DIM // LANES
    else:
        k_ref, v_ref = out_refs
        nq = 0
    nk = N_KV_HEADS * HEAD_DIM // LANES
    cos = cos_ref[...]
    sin = sin_ref[...]

    def norm_rope(c, gain_row):
        ms = _dot((c * c).astype(BF16), ebd_ref[...])
        cn = c * lax.rsqrt(ms + EPS) * gain_row
        partner = jnp.where(first_half, pltpu.roll(cn, LANES - 16, 1), pltpu.roll(cn, 16, 1))
        return cn * cos + partner * sin

    for j in range(nq):
        out = norm_rope(qkv[:, j * LANES:(j + 1) * LANES], hg_ref[0:1, :]).astype(BF16)
        q_ref[0, 2 * j] = out[:, :HEAD_DIM]
        q_ref[0, 2 * j + 1] = out[:, HEAD_DIM:]
    for j in range(nk):
        out = norm_rope(qkv[:, (nq + j) * LANES:(nq + j + 1) * LANES], hg_ref[1:2, :]).astype(BF16)
        k_ref[0, 2 * j] = out[:, :HEAD_DIM]
        k_ref[0, 2 * j + 1] = out[:, HEAD_DIM:]
    for j in range(nk):
        c = qkv[:, (nq + nk + j) * LANES:(nq + nk + j + 1) * LANES]
        v_ref[0, 2 * j] = jnp.where(low, c, 1.0).astype(BF16)
        v_ref[0, 2 * j + 1] = jnp.where(low, pltpu.roll(c, HEAD_DIM, 1), 1.0).astype(BF16)


def _qkv(x, shift, scale, gain, w, head_gain, ebd, cos, sin, *, with_q, per_batch_mods):
    b, n, d = x.shape
    tm = min(512, n)
    ncol = w.shape[1]
    mod_idx = (lambda bi, i: (bi, 0, 0)) if per_batch_mods else (lambda bi, i: (0, 0, 0))
    out_shape = []
    out_specs = []
    if with_q:
        out_shape.append(jax.ShapeDtypeStruct((b, N_HEADS, n, HEAD_DIM), BF16))
        out_specs.append(pl.BlockSpec((1, N_HEADS, tm, HEAD_DIM), lambda bi, i: (bi, 0, i, 0)))
    out_shape.append(jax.ShapeDtypeStruct((b, N_KV_HEADS, n, HEAD_DIM), BF16))
    out_specs.append(pl.BlockSpec((1, N_KV_HEADS, tm, HEAD_DIM), lambda bi, i: (bi, 0, i, 0)))
    out_shape.append(jax.ShapeDtypeStruct((b, N_KV_HEADS, n, LANES), BF16))
    out_specs.append(pl.BlockSpec((1, N_KV_HEADS, tm, LANES), lambda bi, i: (bi, 0, i, 0)))
    return pl.pallas_call(
        functools.partial(_qkv_kernel, with_q=with_q),
        grid=(b, n // tm),
        in_specs=[
            pl.BlockSpec((1, tm, d), lambda bi, i: (bi, i, 0)),
            pl.BlockSpec((1, 1, d), mod_idx),
            pl.BlockSpec((1, 1, d), mod_idx),
            pl.BlockSpec((1, d), lambda bi, i: (0, 0)),
            pl.BlockSpec((d, ncol), lambda bi, i: (0, 0)),
            pl.BlockSpec((2, LANES), lambda bi, i: (0, 0)),
            pl.BlockSpec((LANES, LANES), lambda bi, i: (0, 0)),
            pl.BlockSpec((tm, LANES), lambda bi, i: (i, 0)),
            pl.BlockSpec((tm, LANES), lambda bi, i: (i, 0)),
        ],
        out_specs=out_specs,
        out_shape=out_shape,
        compiler_params=_params(("arbitrary", "arbitrary")),
        name="qkv" if with_q else "ctx_kv",
    )(x, shift, scale, gain, w, head_gain, ebd, cos, sin)


def _attn_kernel(q_ref, k_ref, v_ref, kc_ref, vc_ref, o_ref, *, tq, tk):
    m_rows = GQA_GROUP * tq
    q = q_ref[0].reshape(m_rows, HEAD_DIM)
    m = jnp.full((m_rows, 1), -jnp.inf, F32)
    acc = jnp.zeros((m_rows, LANES), F32)
    chunks = []
    for kr, vr in ((k_ref, v_ref), (kc_ref, vc_ref)):
        size = min(tk, kr.shape[2])
        chunks += [(kr, vr, c, size) for c in range(kr.shape[2] // size)]
    for kr, vr, c, size in chunks:
        kc = kr[0, 0, c * size:(c + 1) * size, :]
        vc = vr[0, 0, c * size:(c + 1) * size, :]
        s = lax.dot_general(q, kc, (((1,), (1,)), ((), ())), preferred_element_type=F32)
        m_new = jnp.maximum(m, jnp.max(s, axis=-1, keepdims=True))
        alpha = jnp.exp2(m - m_new)
        p = jnp.exp2(s - m_new)
        acc = alpha * acc + _dot(p.astype(BF16), vc)
        m = m_new
    o = acc * (1.0 / pltpu.roll(acc, HEAD_DIM, 1))
    o_ref[0] = jnp.concatenate(
        [o[g * tq:(g + 1) * tq, :HEAD_DIM] for g in range(GQA_GROUP)], axis=-1).astype(BF16)


def _attention(q, k, v, k_ctx, v_ctx):
    b, _, n, _ = q.shape
    n_ctx = k_ctx.shape[2]
    tq = min(256, n)
    tk = 256
    assert n % min(tk, n) == 0 and n_ctx % min(tk, n_ctx) == 0
    kv_spec = lambda rows, width: pl.BlockSpec((1, 1, rows, width), lambda bi, kh, qi: (bi, kh, 0, 0))
    return pl.pallas_call(
        functools.partial(_attn_kernel, tq=tq, tk=tk),
        grid=(b, N_KV_HEADS, n // tq),
        in_specs=[
            pl.BlockSpec((1, GQA_GROUP, tq, HEAD_DIM), lambda bi, kh, qi: (bi, kh, qi, 0)),
            kv_spec(n, HEAD_DIM), kv_spec(n, LANES), kv_spec(n_ctx, HEAD_DIM), kv_spec(n_ctx, LANES),
        ],
        out_specs=pl.BlockSpec((1, tq, GQA_GROUP * HEAD_DIM), lambda bi, kh, qi: (bi, qi, kh)),
        out_shape=jax.ShapeDtypeStruct((b, n, N_HEADS * HEAD_DIM), BF16),
        compiler_params=_params(("arbitrary", "arbitrary", "arbitrary")),
        name="attn",
    )(q, k, v, k_ctx, v_ctx)


def _proj_moe_kernel(*refs, tiles_per_batch, folded):
    if folded:
        dif_ref, sum_ref, sumb_ref, *refs, a_scr = refs
        j = pl.program_id(0) % tiles_per_batch

        @pl.when(j < tiles_per_batch // 2)
        def _():
            a_scr[...] = dif_ref[0]

        @pl.when(j >= tiles_per_batch // 2)
        def _():
            rows = _dot(_reversal(a_scr.shape[0]), sum_ref[0])
            row = lax.broadcasted_iota(jnp.int32, rows.shape, 0)
            a_scr[...] = jnp.where(row == 0, sumb_ref[0, 0:1, :].astype(F32), rows).astype(BF16)

        a = a_scr[...]
    else:
        a_ref, *refs = refs
        a = a_ref[...]
    (wo_ref, x_ref, g1_ref, gain_ref, sh_ref, sc_ref, wrh_ref, wrl_ref, br_ref,
     x1_ref, s_ref, gp_ref, pc_ref) = refs
    x1 = x_ref[...] + g1_ref[0] * _dot(a, wo_ref[...])
    x1_ref[...] = x1
    h2 = _modulate(x1, gain_ref[...], sh_ref[0], sc_ref[0])
    h_hi, h_lo = _split_bf16(h2)
    logits = (_dot(h_hi, wrh_ref[...]) + _dot(h_lo, wrh_ref[...]) + _dot(h_hi, wrl_ref[...])
              + br_ref[...])
    tm = logits.shape[0]
    lane = lax.broadcasted_iota(jnp.int32, (tm, LANES), 1)
    lane_f = lane.astype(F32)

    def top(vals):
        best = jnp.max(vals, axis=-1, keepdims=True)
        idx = jnp.min(jnp.where(vals == best, lane_f, float(LANES)), axis=-1, keepdims=True)
        return best, idx

    lg = jnp.where(lane < N_GROUPS, logits, NEG_BIG)
    g_max, g_sel = top(lg)
    p_grp = 1.0 / jnp.sum(jnp.where(lane < N_GROUPS, jnp.exp(lg - g_max), 0.0), axis=-1, keepdims=True)
    lo = N_GROUPS + EXPERTS_PER_GROUP * g_sel
    le = jnp.where((lane_f >= lo) & (lane_f < lo + EXPERTS_PER_GROUP), logits, NEG_BIG)
    t1, i1 = top(le)
    t2, i2 = top(jnp.where(lane_f == i1, NEG_BIG, le))
    e21 = jnp.exp(t2 - t1)
    w1 = 1.0 / (1.0 + e21)
    gate0 = p_grp * w1
    gate1 = p_grp * (e21 * w1)
    e0 = i1 - N_GROUPS
    e1 = i2 - N_GROUPS

    onehot = jnp.where((lane_f == e0) | (lane_f == e1), 1.0, 0.0)
    row = lax.broadcasted_iota(jnp.int32, (tm, tm), 0)
    col = lax.broadcasted_iota(jnp.int32, (tm, tm), 1)
    lower = jnp.where(col < row, 1.0, 0.0).astype(BF16)
    before = _dot(lower, onehot.astype(BF16))
    cnt = jnp.sum(onehot, axis=0, keepdims=True)
    pc = jnp.floor((cnt + (RUN_ALIGN - 1)) * (1.0 / RUN_ALIGN))
    urow = lax.broadcasted_iota(jnp.int32, (LANES, LANES), 0)
    ucol = lax.broadcasted_iota(jnp.int32, (LANES, LANES), 1)
    upper = jnp.where(urow < ucol, 1.0, 0.0).astype(BF16)
    run_start = _dot(jnp.broadcast_to(pc, (8, LANES)).astype(BF16), upper)[0:1] * RUN_ALIGN
    base = run_start + before
    pos0 = jnp.sum(jnp.where(lane_f == e0, base, 0.0), axis=-1, keepdims=True)
    pos1 = jnp.sum(jnp.where(lane_f == e1, base, 0.0), axis=-1, keepdims=True)
    gp = jnp.where(lane == 0, gate0, jnp.where(lane == 1, gate1, jnp.where(lane == 2, pos0,
                                                                       jnp.where(lane == 3, pos1, 0.0))))
    gp_ref[...] = gp
    pc_ref[...] = jnp.broadcast_to(pc, (8, LANES)).astype(jnp.int32)

    pos_t = jnp.where(lane == 0, pos0, jnp.where(lane == 1, pos1, -1.0)).T
    sr = s_ref.shape[0]
    srow = lax.broadcasted_iota(jnp.int32, (sr, tm), 0).astype(F32)
    perm = jnp.where((srow == pos_t[0:1, :]) | (srow == pos_t[1:2, :]), 1.0, 0.0).astype(BF16)
    s_ref[...] = _dot(perm, h_hi)


def _proj_moe(a, w_o, x, g1, gain, shift, scale, wr_hi, wr_lo, b_r, *, tokens_per_batch):
    t, d = x.shape
    tm = min(MOE_TILE, tokens_per_batch)
    nt = t // tm
    sr = _sorted_rows(tm)
    per_b = tokens_per_batch // tm
    mod_idx = lambda i: (i // per_b, 0, 0)
    row_spec = lambda w: pl.BlockSpec((tm, w), lambda i: (i, 0))
    full = lambda s: pl.BlockSpec(s, lambda i: (0,) * len(s))
    folded = isinstance(a, tuple)
    if folded:
        assert per_b % 2 == 0
        half = per_b // 2
        a_inputs = (a[0], a[1], a[1])
        a_specs = [
            pl.BlockSpec((1, tm, d), lambda i: (i // per_b, jnp.minimum(i % per_b, half - 1), 0)),
            pl.BlockSpec((1, tm, d), lambda i: (i // per_b, jnp.clip(per_b - 1 - i % per_b, 0, half - 1), 0)),
            pl.BlockSpec((1, 8, d), lambda i: (i // per_b, (tm // 8) * jnp.clip(per_b - i % per_b, 1, half), 0)),
        ]
        scratch = [pltpu.VMEM((tm, d), BF16)]
    else:
        a_inputs = (a,)
        a_specs = [row_spec(d)]
        scratch = []
    return pl.pallas_call(
        functools.partial(_proj_moe_kernel, tiles_per_batch=per_b, folded=folded),
        grid=(nt,),
        in_specs=a_specs + [
            full((d, d)), row_spec(d),
            pl.BlockSpec((1, 1, d), mod_idx), full((1, d)),
            pl.BlockSpec((1, 1, d), mod_idx), pl.BlockSpec((1, 1, d), mod_idx),
            full((d, LANES)), full((d, LANES)), full((1, LANES)),
        ],
        out_specs=[row_spec(d), pl.BlockSpec((sr, d), lambda i: (i, 0)), row_spec(LANES),
                   pl.BlockSpec((8, LANES), lambda i: (i, 0))],
        out_shape=[
            jax.ShapeDtypeStruct((t, d), F32), jax.ShapeDtypeStruct((nt * sr, d), F32),
            jax.ShapeDtypeStruct((t, LANES), F32), jax.ShapeDtypeStruct((nt * 8, LANES), jnp.int32),
        ],
        scratch_shapes=scratch,
        compiler_params=_params(("arbitrary",)),
        name="proj_moe",
    )(*a_inputs, w_o, x, g1, gain, shift, scale, wr_hi, wr_lo, b_r)


def _rows(ref, start_units, n_units):
    return ref.at[pl.ds(pl.multiple_of(start_units * RUN_ALIGN, RUN_ALIGN), n_units * RUN_ALIGN)]


def _ffn_kernel(exp_nblk, exp_blk0, blk_e, blk_b0, blk_n, blk_tlo, blk_thi, seg_start, run_start,
                nused_ref, s_hbm, w1_ref, w3_ref, w2_ref, y_hbm, xg, yo, w1b, w3b, w2b, sem, osem,
                *, sr_units, nblk):
    e_step = pl.program_id(0)
    nused = nused_ref[0]
    bm = xg.shape[1]
    bm_units = bm // RUN_ALIGN

    def gather(b, slot):
        e = blk_e[b]
        b0 = blk_b0[b]

        def body(ti, carry):
            rs = seg_start[ti * N_EXPERTS + e]
            re = seg_start[(ti + 1) * N_EXPERTS + e]
            lo = jnp.maximum(rs, b0)
            n = jnp.minimum(re, b0 + bm_units) - lo

            @pl.when(n > 0)
            def _():
                src = ti * sr_units + run_start[ti * N_EXPERTS + e] + (lo - rs)
                pltpu.make_async_copy(_rows(s_hbm, src, n), _rows(xg.at[slot], lo - b0, n),
                                      sem.at[slot]).start()

            return carry

        lax.fori_loop(blk_tlo[b], blk_thi[b], body, 0)

    def write_back(g, slot):
        return pltpu.make_async_copy(yo.at[slot], y_hbm.at[pl.ds(pl.multiple_of(g * bm, bm), bm)],
                                     osem.at[slot])

    def emit(g, fill):
        slot = g % 2

        @pl.when(g >= 2)
        def _():
            write_back(g - 2, slot).wait()

        fill(slot)
        write_back(g, slot).start()

    @pl.when(e_step == 0)
    def _():
        xg[...] = jnp.zeros_like(xg)
        gather(0, 0)

    w1b[...] = w1_ref[0, 0].astype(BF16)
    w3b[...] = w3_ref[0, 0].astype(BF16)
    w2b[...] = w2_ref[0, 0].astype(BF16)

    def block(jb, carry):
        g = exp_blk0[e_step] + jb

        @pl.when(g + 1 < nused)
        def _():
            gather(g + 1, (g + 1) % 2)

        def fill(slot):
            pltpu.make_async_copy(_rows(s_hbm, 0, blk_n[g]), _rows(xg.at[slot], 0, blk_n[g]),
                                  sem.at[slot]).wait()
            xb = xg[slot].astype(BF16)
            a = _dot(xb, w1b[...])
            b = _dot(xb, w3b[...])
            mid = (a / (1.0 + jnp.exp(-a))) * b
            yo[slot] = _dot(mid.astype(BF16), w2b[...])

        emit(g, fill)
        return carry

    lax.fori_loop(0, exp_nblk[e_step], block, 0)

    @pl.when(e_step == pl.num_programs(0) - 1)
    def _():
        def tail(g, carry):
            def fill(slot):
                yo[slot] = jnp.zeros(yo.shape[1:], yo.dtype)

            emit(g, fill)
            return carry

        lax.fori_loop(nused, nblk, tail, 0)
        write_back(nblk - 2, nblk % 2).wait()
        write_back(nblk - 1, (nblk - 1) % 2).wait()


def _ffn(tables, s_tiles, w1, w3, w2, *, layer, nblk, sr):
    bm = FFN_BLOCK
    d = s_tiles.shape[1]
    de = w1.shape[3]
    nsp = len(tables)
    assert nblk >= 2
    grid_spec = pltpu.PrefetchScalarGridSpec(
        num_scalar_prefetch=nsp,
        grid=(N_EXPERTS,),
        in_specs=[
            pl.BlockSpec(memory_space=pl.ANY),
            pl.BlockSpec((1, 1, d, de), lambda e, *_: (layer, e, 0, 0)),
            pl.BlockSpec((1, 1, d, de), lambda e, *_: (layer, e, 0, 0)),
            pl.BlockSpec((1, 1, de, d), lambda e, *_: (layer, e, 0, 0)),
        ],
        out_specs=pl.BlockSpec(memory_space=pl.ANY),
        scratch_shapes=[pltpu.VMEM((2, bm, d), F32), pltpu.VMEM((2, bm, d), F32),
                        pltpu.VMEM((d, de), BF16), pltpu.VMEM((d, de), BF16), pltpu.VMEM((de, d), BF16),
                        pltpu.SemaphoreType.DMA((2,)), pltpu.SemaphoreType.DMA((2,))],
    )
    return pl.pallas_call(
        functools.partial(_ffn_kernel, sr_units=sr // RUN_ALIGN, nblk=nblk),
        grid_spec=grid_spec,
        out_shape=jax.ShapeDtypeStruct((nblk * bm, d), F32),
        compiler_params=_params(("arbitrary",)),
        name="ffn",
    )(*tables, s_tiles, w1, w3, w2)


def _combine_kernel(run_len, run_start, seg_start, exp_start, tile_rows,
                    y_hbm, x_ref, g2_ref, gp_ref, fgain_ref, o_ref, ybuf, sem, *, nsteps, final):
    i = pl.program_id(0)

    def gather(ti, slot):
        for e in range(N_EXPERTS):
            n = run_len[ti * N_EXPERTS + e]

            @pl.when(n > 0)
            def _():
                src = exp_start[e] + seg_start[ti * N_EXPERTS + e]
                pltpu.make_async_copy(_rows(y_hbm, src, n),
                                      _rows(ybuf.at[slot], run_start[ti * N_EXPERTS + e], n),
                                      sem.at[slot]).start()

    @pl.when(i == 0)
    def _():
        ybuf[...] = jnp.zeros_like(ybuf)
        gather(0, 0)

    @pl.when(i + 1 < nsteps)
    def _():
        gather(i + 1, (i + 1) % 2)

    slot = i % 2
    pltpu.make_async_copy(_rows(y_hbm, 0, tile_rows[i]), _rows(ybuf.at[slot], 0, tile_rows[i]),
                          sem.at[slot]).wait()
    gp = gp_ref[...]
    tm = gp.shape[0]
    sr = ybuf.shape[1]
    col = lax.broadcasted_iota(jnp.int32, (tm, sr), 1).astype(F32)
    weights = (jnp.where(col == gp[:, 2:3], gp[:, 0:1], 0.0)
               + jnp.where(col == gp[:, 3:4], gp[:, 1:2], 0.0)).astype(BF16)
    y = _dot(weights, ybuf[slot].astype(BF16))
    x2 = x_ref[...] + g2_ref[0] * y
    if final:
        ms = jnp.mean(x2 * x2, axis=-1, keepdims=True)
        x2 = (x2 * lax.rsqrt(ms + EPS)) * fgain_ref[...]
    o_ref[...] = x2


def _combine(tables, yb, x1, g2, gp, fgain, *, tokens_per_batch, final):
    t, d = x1.shape
    tm = min(MOE_TILE, tokens_per_batch)
    nsteps = t // tm
    per_b = tokens_per_batch // tm
    sr = _sorted_rows(tm)
    grid_spec = pltpu.PrefetchScalarGridSpec(
        num_scalar_prefetch=len(tables),
        grid=(nsteps,),
        in_specs=[
            pl.BlockSpec(memory_space=pl.ANY),
            pl.BlockSpec((tm, d), lambda i, *_: (i, 0)),
            pl.BlockSpec((1, 1, d), lambda i, *_: (i // per_b, 0, 0)),
            pl.BlockSpec((tm, LANES), lambda i, *_: (i, 0)),
            pl.BlockSpec((1, d), lambda i, *_: (0, 0)),
        ],
        out_specs=pl.BlockSpec((tm, d), lambda i, *_: (i, 0)),
        scratch_shapes=[pltpu.VMEM((2, sr, d), F32), pltpu.SemaphoreType.DMA((2,))],
    )
    return pl.pallas_call(
        functools.partial(_combine_kernel, nsteps=nsteps, final=final),
        grid_spec=grid_spec,
        out_shape=jax.ShapeDtypeStruct((t, d), F32),
        compiler_params=_params(("arbitrary",)),
        name="combine_final" if final else "combine",
    )(*tables, yb, x1, g2, gp, fgain)


def _reversal(size):
    row = lax.broadcasted_iota(jnp.int32, (size, size), 0)
    col = lax.broadcasted_iota(jnp.int32, (size, size), 1)
    return jnp.where(col == size - row, 1.0, 0.0).astype(BF16)


def _fmix_fold_kernel(x_ref, xm_ref, xb_ref, xc_ref, sh_ref, sc_ref, gain_ref, cc_ref, sc_mat_ref,
                      a_ref, b_ref, amid_ref):
    mod = lambda v: _modulate(v, gain_ref[...], sh_ref[0], sc_ref[0])
    h = mod(x_ref[...])
    ft = h.shape[0]
    m_hi, m_lo = _split_bf16(mod(xm_ref[...]))
    rev = _reversal(ft)
    row = lax.broadcasted_iota(jnp.int32, h.shape, 0)
    hr = jnp.where(row == 0, mod(xb_ref[...])[0:1], _dot(rev, m_hi) + _dot(rev, m_lo))
    he = (h + hr).astype(BF16)
    ho = (h - hr).astype(BF16)
    hc = mod(xc_ref[...]).astype(BF16)
    gw = cc_ref.shape[0]
    for g in range(FOURIER_GROUPS):
        cols = slice(g * gw, (g + 1) * gw)
        a_ref[:, cols] = _dot(he[:, cols], cc_ref[...]).astype(BF16)
        b_ref[:, cols] = _dot(ho[:, cols], sc_mat_ref[...]).astype(BF16)
        amid_ref[:, cols] = _dot(hc[:, cols], cc_ref[...])


def _fmix_fold(x, shift, scale, gain, cc, sc, *, bsz, n):
    t, d = x.shape
    hn = n // 2
    ft = min(512, hn)
    tpb = n // ft
    half = tpb // 2
    gw = d // FOURIER_GROUPS
    mod_idx = lambda b, j: (b, 0, 0)
    return pl.pallas_call(
        _fmix_fold_kernel,
        grid=(bsz, half),
        in_specs=[
            pl.BlockSpec((ft, d), lambda b, j: (b * tpb + j, 0)),
            pl.BlockSpec((ft, d), lambda b, j: (b * tpb + tpb - 1 - j, 0)),
            pl.BlockSpec((8, d), lambda b, j: (b * (n // 8) + (ft // 8) * ((tpb - j) % tpb), 0)),
            pl.BlockSpec((8, d), lambda b, j: (b * (n // 8) + hn // 8, 0)),
            pl.BlockSpec((1, 1, d), mod_idx), pl.BlockSpec((1, 1, d), mod_idx),
            pl.BlockSpec((1, d), lambda b, j: (0, 0)),
            pl.BlockSpec((gw, gw), lambda b, j: (0, 0)), pl.BlockSpec((gw, gw), lambda b, j: (0, 0)),
        ],
        out_specs=[pl.BlockSpec((ft, d), lambda b, j: (b * half + j, 0)),
                   pl.BlockSpec((ft, d), lambda b, j: (b * half + j, 0)),
                   pl.BlockSpec((8, d), lambda b, j: (b, 0))],
        out_shape=[jax.ShapeDtypeStruct((bsz * hn, d), BF16), jax.ShapeDtypeStruct((bsz * hn, d), BF16),
                   jax.ShapeDtypeStruct((bsz * 8, d), F32)],
        compiler_params=_params(("arbitrary", "arbitrary")),
        name="fmix_fold",
    )(x, x, x, x, shift, scale, gain, cc, sc)


def _fmix_pos_kernel(ch_ref, sh_ref, sign_ref, a_ref, b_ref, amid_ref, dif_ref, sum_ref, p_acc, q_acc):
    k = pl.program_id(2)

    @pl.when(k == 0)
    def _():
        p_acc[...] = jnp.zeros_like(p_acc)
        q_acc[...] = jnp.zeros_like(q_acc)

    p_acc[...] += _dot(ch_ref[...], a_ref[0])
    q_acc[...] += _dot(sh_ref[...], b_ref[0])

    @pl.when(k == pl.num_programs(2) - 1)
    def _():
        p = p_acc[...] + sign_ref[...] * amid_ref[0, 0:1, :]
        dif_ref[0] = (p - q_acc[...]).astype(BF16)
        sum_ref[0] = (p + q_acc[...]).astype(BF16)


def _fmix_pos(ch, sh, sign, a, b, amid):
    bsz, hn, d = a.shape
    mp = ch.shape[0]
    tmr = mp // 2
    tk = min(512, hn)
    out = jax.ShapeDtypeStruct((bsz, mp, d), BF16)
    return pl.pallas_call(
        _fmix_pos_kernel,
        grid=(mp // tmr, bsz, hn // tk),
        in_specs=[
            pl.BlockSpec((tmr, tk), lambda i, bi, k: (i, k)),
            pl.BlockSpec((tmr, tk), lambda i, bi, k: (i, k)),
            pl.BlockSpec((tmr, 1), lambda i, bi, k: (i, 0)),
            pl.BlockSpec((1, tk, d), lambda i, bi, k: (bi, k, 0)),
            pl.BlockSpec((1, tk, d), lambda i, bi, k: (bi, k, 0)),
            pl.BlockSpec((1, 8, d), lambda i, bi, k: (bi, 0, 0)),
        ],
        out_specs=[pl.BlockSpec((1, tmr, d), lambda i, bi, k: (bi, i, 0)),
                   pl.BlockSpec((1, tmr, d), lambda i, bi, k: (bi, i, 0))],
        out_shape=[out, out],
        scratch_shapes=[pltpu.VMEM((tmr, d), F32), pltpu.VMEM((tmr, d), F32)],
        compiler_params=_params(("arbitrary", "arbitrary", "arbitrary")),
        name="fmix_pos",
    )(ch, sh, sign, a, b, amid)


def _rope_tables(n):
    half = ROPE_AXIS_DIM // 2
    rows = n // GRID_W
    inv = ROPE_THETA ** (-jnp.arange(half, dtype=F32) / half)
    pos = jnp.arange(n, dtype=jnp.int32)
    ang_r = (pos // GRID_W).astype(F32)[:, None] * inv
    ang_c = (pos % GRID_W).astype(F32)[:, None] * inv
    del rows
    cos = jnp.concatenate([jnp.cos(ang_r)] * 2 + [jnp.cos(ang_c)] * 2, axis=-1)
    sin = jnp.concatenate([-jnp.sin(ang_r), jnp.sin(ang_r), -jnp.sin(ang_c), jnp.sin(ang_c)], axis=-1)
    return jnp.tile(cos, (1, LANES // HEAD_DIM)), jnp.tile(sin, (1, LANES // HEAD_DIM))


def _dft_mats(n, scale):
    r = 1
    while r * r < n:
        r *= 2
    c = n // r
    k = jnp.arange(n, dtype=jnp.int32)[:, None]
    ang_a = ((k * jnp.arange(r, dtype=jnp.int32)[None, :] * c) % n).astype(F32) * (2.0 * math.pi / n)
    ang_b = ((k * jnp.arange(c, dtype=jnp.int32)[None, :]) % n).astype(F32) * (2.0 * math.pi / n)
    ca, sa, cb, sb = jnp.cos(ang_a), jnp.sin(ang_a), jnp.cos(ang_b), jnp.sin(ang_b)
    cosm = (ca[:, :, None] * cb[:, None, :] - sa[:, :, None] * sb[:, None, :]).reshape(n, n)
    sinm = (sa[:, :, None] * cb[:, None, :] + ca[:, :, None] * sb[:, None, :]).reshape(n, n)
    return (cosm * scale).astype(BF16), (sinm * scale).astype(BF16)


DFT_ROW_PAD = 32


def _dft_half(n):
    hn = n // 2
    mp = hn + DFT_ROW_PAD
    r = 1
    while r * r < hn:
        r *= 2
    c = hn // r
    k = jnp.arange(mp, dtype=jnp.int32)[:, None]
    valid = k <= hn
    w = 2.0 * math.pi / n
    ang_a = ((k * (jnp.arange(r, dtype=jnp.int32)[None, :] * c)) % n).astype(F32) * w
    ang_b = ((k * jnp.arange(c, dtype=jnp.int32)[None, :]) % n).astype(F32) * w
    ca, sa, cb, sb = jnp.cos(ang_a), jnp.sin(ang_a), jnp.cos(ang_b), jnp.sin(ang_b)
    cosm = (ca[:, :, None] * cb[:, None, :] - sa[:, :, None] * sb[:, None, :]).reshape(mp, hn)
    sinm = (sa[:, :, None] * cb[:, None, :] + ca[:, :, None] * sb[:, None, :]).reshape(mp, hn)
    scale = n ** -0.5
    col_w = jnp.where(jnp.arange(hn)[None, :] == 0, 0.5 * scale, scale)
    cosm = jnp.where(valid, cosm * col_w, 0.0).astype(BF16)
    sinm = jnp.where(valid, sinm * scale, 0.0).astype(BF16)
    sign = jnp.where(valid, jnp.where(k % 2 == 0, scale, -scale), 0.0).astype(F32)
    return cosm, sinm, sign


def _moe(a, w_o, x, g1, g2, gain, shift, scale, w_rg, b_rg, w_re, b_re, w1, w3, w2, fgain, *,
         layer, tokens_per_batch, final):
    t, d = x.shape
    pad = LANES - N_GROUPS - N_EXPERTS
    w_r = jnp.concatenate([w_rg, w_re, jnp.zeros((d, pad), F32)], axis=1)
    b_r = jnp.concatenate([b_rg, b_re, jnp.zeros((pad,), F32)])[None, :]
    wr_hi = w_r.astype(BF16)
    wr_lo = (w_r - wr_hi.astype(F32)).astype(BF16)
    x1, s_tiles, gp, pc = _proj_moe(a, w_o, x, g1, gain, shift, scale, wr_hi, wr_lo, b_r,
                                    tokens_per_batch=tokens_per_batch)
    tm = min(MOE_TILE, tokens_per_batch)
    nt = t // tm
    sr = _sorted_rows(tm)
    bmu = FFN_BLOCK // RUN_ALIGN
    i32 = jnp.int32
    run_len = pc.reshape(nt, 8, LANES)[:, 0, :N_EXPERTS]
    seg_start = jnp.concatenate([jnp.zeros((1, N_EXPERTS), i32), jnp.cumsum(run_len, axis=0)])
    seg_len = seg_start[-1]
    seg_pad = (seg_len + bmu - 1) // bmu * bmu
    exp_end = jnp.cumsum(seg_pad)
    exp_start = exp_end - seg_pad
    run_start = jnp.cumsum(run_len, axis=1) - run_len
    tile_rows = run_start[:, -1] + run_len[:, -1]
    max_units = (2 * t) // RUN_ALIGN + nt * N_EXPERTS + N_EXPERTS * (bmu - 1)
    nblk = (max_units + bmu - 1) // bmu
    blk_start = jnp.arange(nblk, dtype=i32) * bmu
    blk_e = jnp.minimum(jnp.sum((exp_end[None, :] <= blk_start[:, None]).astype(i32), axis=1),
                        N_EXPERTS - 1)
    blk_b0 = blk_start - exp_start[blk_e]
    blk_n = jnp.clip(seg_len[blk_e] - blk_b0, 0, bmu)
    seg_e = seg_start.T[blk_e]
    blk_tlo = jnp.sum((seg_e[:, 1:] <= blk_b0[:, None]).astype(i32), axis=1)
    blk_thi = jnp.sum((seg_e[:, :-1] < (blk_b0 + bmu)[:, None]).astype(i32), axis=1)
    nused = (exp_end[-1] // bmu)[None]
    flat = lambda v: v.reshape(-1).astype(i32)
    yb = _ffn((seg_pad // bmu, exp_start // bmu, blk_e, blk_b0, blk_n, blk_tlo, blk_thi,
               flat(seg_start), flat(run_start), nused),
              s_tiles, w1, w3, w2, layer=layer, nblk=nblk, sr=sr)
    return _combine((flat(run_len), flat(run_start), flat(seg_start), exp_start, tile_rows),
                    yb, x1, g2, gp, fgain, tokens_per_batch=tokens_per_batch, final=final)


def kernel(x, c, ctx, c_ctx, w_mod, b_mod, norm_mix, norm_ffn, attn_w_qkv, attn_q_norm, attn_k_norm,
           attn_w_o, fourier_w_o, moe_w_rg, moe_b_rg, moe_w_re, moe_b_re, moe_w1, moe_w3, moe_w2,
           final_norm):
    bsz, n, d = x.shape
    t = bsz * n
    q_dim = N_HEADS * HEAD_DIM
    kv_dim = N_KV_HEADS * HEAD_DIM

    cond = jnp.zeros((16, d), F32).at[:bsz].set(c).at[bsz].set(c_ctx)
    mods = _ada(cond, w_mod, b_mod)

    def lat(layer, j):
        return mods[layer, :bsz, j * d:(j + 1) * d][:, None, :]

    def ctxm(layer, j):
        return mods[layer, bsz:bsz + 1, j * d:(j + 1) * d][:, None, :]

    w_qkv = attn_w_qkv[0].astype(BF16)
    qscale = (HEAD_DIM ** -0.5) * math.log2(math.e)
    head_gain = jnp.stack([jnp.tile(attn_q_norm[0] * qscale, LANES // HEAD_DIM),
                           jnp.tile(attn_k_norm[0], LANES // HEAD_DIM)])
    lane = jnp.arange(LANES)
    ebd = jnp.where((lane[:, None] // HEAD_DIM) == (lane[None, :] // HEAD_DIM),
                    1.0 / HEAD_DIM, 0.0).astype(BF16)
    cos, sin = _rope_tables(n)
    gain0 = norm_mix[0][None, :]
    q, k_l, v_l = _qkv(x, lat(0, 0), lat(0, 1), gain0, w_qkv, head_gain, ebd, cos, sin,
                       with_q=True, per_batch_mods=True)
    n_ctx = ctx.shape[1]
    k_c, v_c = _qkv(ctx, ctxm(0, 0), ctxm(0, 1), gain0, w_qkv[:, q_dim:], head_gain, ebd,
                    jnp.ones((n_ctx, LANES), F32), jnp.zeros((n_ctx, LANES), F32),
                    with_q=False, per_batch_mods=False)
    del kv_dim
    o = _attention(q, k_l, v_l, k_c, v_c).reshape(t, q_dim)

    xf = x.reshape(t, d)
    moe_w = lambda i: (moe_w_rg[i], moe_b_rg[i], moe_w_re[i], moe_b_re[i],
                       moe_w1, moe_w3, moe_w2)
    fgain = final_norm[None, :]
    x2 = _moe(o, attn_w_o[0].astype(BF16), xf, lat(0, 2), lat(0, 5), norm_ffn[0][None, :],
              lat(0, 3), lat(0, 4), *moe_w(0), fgain, layer=0, tokens_per_batch=n, final=False)

    gw = d // FOURIER_GROUPS
    cc, sc = _dft_mats(gw, gw ** -0.5)
    ch, sh, sign = _dft_half(n)
    a_e, b_o, a_mid = _fmix_fold(x2, lat(1, 0), lat(1, 1), norm_mix[1][None, :], cc, sc, bsz=bsz, n=n)
    f = tuple(_fmix_pos(ch, sh, sign, a_e.reshape(bsz, n // 2, d), b_o.reshape(bsz, n // 2, d),
                        a_mid.reshape(bsz, 8, d)))
    out = _moe(f, fourier_w_o[0].astype(BF16), x2, lat(1, 2), lat(1, 5), norm_ffn[1][None, :],
               lat(1, 3), lat(1, 4), *moe_w(1), fgain, layer=1, tokens_per_batch=n, final=True)
    return out.reshape(bsz, n, d)
```

```python
import functools
import math

import jax
import jax.numpy as jnp
from jax import lax
from jax.experimental import pallas as pl
from jax.experimental.pallas import tpu as pltpu

F32 = jnp.float32
BF16 = jnp.bfloat16

N_HEADS = 16
N_KV_HEADS = 4
HEAD_DIM = 64
GQA_GROUP = N_HEADS // N_KV_HEADS
GRID_W = 64
ROPE_AXIS_DIM = HEAD_DIM // 2
ROPE_THETA = 10000.0
FOURIER_GROUPS = 4
N_GROUPS = 4
EXPERTS_PER_GROUP = 8
N_EXPERTS = N_GROUPS * EXPERTS_PER_GROUP
EPS = 1e-6
LANES = 128
NEG_BIG = -1e30
VMEM_LIMIT = 48 * 1024 * 1024

FFN_BLOCK = 512
MOE_TILE = 512
RUN_ALIGN = 8


def _sorted_rows(tile_tokens):
    return 2 * tile_tokens + N_EXPERTS * RUN_ALIGN


def _dot(a, b):
    return jnp.dot(a, b, preferred_element_type=F32)


def _split_bf16(a):
    hi = a.astype(BF16)
    lo = (a - hi.astype(F32)).astype(BF16)
    return hi, lo


_HI16 = 0xFFFF0000


def _pack_halves(v):
    w = v.shape[1] // 2
    lo = lax.bitcast_convert_type(v[:, :w], jnp.uint32) >> 16
    hi = lax.bitcast_convert_type(v[:, w:], jnp.uint32) & jnp.uint32(_HI16)
    return hi | lo


def _unpack_halves(p):
    lo = lax.bitcast_convert_type(p << 16, F32).astype(BF16)
    hi = lax.bitcast_convert_type(p & jnp.uint32(_HI16), F32).astype(BF16)
    return lo, hi


def _params(sem):
    return pltpu.CompilerParams(dimension_semantics=sem, vmem_limit_bytes=VMEM_LIMIT)


def _ada_kernel(c_ref, w_ref, b_ref, o_ref):
    c = c_ref[...]
    s = c / (1.0 + jnp.exp(-c))
    s_hi, s_lo = _split_bf16(s)
    w_hi, w_lo = _split_bf16(w_ref[0])
    o_ref[0] = _dot(s_hi, w_hi) + _dot(s_hi, w_lo) + _dot(s_lo, w_hi) + b_ref[0]


def _ada(cond, w_mod, b_mod):
    depth, d, d6 = w_mod.shape
    rows = cond.shape[0]
    tn = 1536
    return pl.pallas_call(
        _ada_kernel,
        grid=(depth, d6 // tn),
        in_specs=[
            pl.BlockSpec((rows, d), lambda l, j: (0, 0)),
            pl.BlockSpec((1, d, tn), lambda l, j: (l, 0, j)),
            pl.BlockSpec((1, 1, tn), lambda l, j: (l, 0, j)),
        ],
        out_specs=pl.BlockSpec((1, rows, tn), lambda l, j: (l, 0, j)),
        out_shape=jax.ShapeDtypeStruct((depth, rows, d6), F32),
        compiler_params=_params(("arbitrary", "arbitrary")),
        name="ada",
    )(cond, w_mod, b_mod.reshape(depth, 1, d6))


def _modulate(xf, gain, shift, scale):
    ms = jnp.mean(xf * xf, axis=-1, keepdims=True)
    return (xf * lax.rsqrt(ms + EPS)) * gain * (1.0 + scale) + shift


def _qkv_kernel(x_ref, sh_ref, sc_ref, gain_ref, w_ref, hg_ref, ebd_ref, cos_ref, sin_ref,
                *out_refs, with_q):
    h = _modulate(x_ref[0], gain_ref[...], sh_ref[0], sc_ref[0])
    qkv = _dot(h.astype(BF16), w_ref[...])
    tm = qkv.shape[0]
    lane = lax.broadcasted_iota(jnp.int32, (tm, LANES), 1)
    first_half = (lane & 16) == 0
    low = lane < HEAD_DIM
    if with_q:
        q_ref, k_ref, v_ref = out_refs
        nq = N_HEADS * HEAD_DIM // LANES
    else:
        k_ref, v_ref = out_refs
        nq = 0
    nk = N_KV_HEADS * HEAD_DIM // LANES
    cos = cos_ref[...]
    sin = sin_ref[...]

    def norm_rope(c, gain_row):
        ms = _dot((c * c).astype(BF16), ebd_ref[...])
        cn = c * lax.rsqrt(ms + EPS) * gain_row
        partner = jnp.where(first_half, pltpu.roll(cn, LANES - 16, 1), pltpu.roll(cn, 16, 1))
        return cn * cos + partner * sin

    for j in range(nq):
        out = norm_rope(qkv[:, j * LANES:(j + 1) * LANES], hg_ref[0:1, :]).astype(BF16)
        q_ref[0, 2 * j] = out[:, :HEAD_DIM]
        q_ref[0, 2 * j + 1] = out[:, HEAD_DIM:]
    for j in range(nk):
        out = norm_rope(qkv[:, (nq + j) * LANES:(nq + j + 1) * LANES], hg_ref[1:2, :]).astype(BF16)
        k_ref[0, 2 * j] = out[:, :HEAD_DIM]
        k_ref[0, 2 * j + 1] = out[:, HEAD_DIM:]
    for j in range(nk):
        c = qkv[:, (nq + nk + j) * LANES:(nq + nk + j + 1) * LANES]
        v_ref[0, 2 * j] = jnp.where(low, c, 1.0).astype(BF16)
        v_ref[0, 2 * j + 1] = jnp.where(low, pltpu.roll(c, HEAD_DIM, 1), 1.0).astype(BF16)


def _qkv(x, shift, scale, gain, w, head_gain, ebd, cos, sin, *, with_q, per_batch_mods):
    b, n, d = x.shape
    tm = min(512, n)
    ncol = w.shape[1]
    mod_idx = (lambda bi, i: (bi, 0, 0)) if per_batch_mods else (lambda bi, i: (0, 0, 0))
    out_shape = []
    out_specs = []
    if with_q:
        out_shape.append(jax.ShapeDtypeStruct((b, N_HEADS, n, HEAD_DIM), BF16))
        out_specs.append(pl.BlockSpec((1, N_HEADS, tm, HEAD_DIM), lambda bi, i: (bi, 0, i, 0)))
    out_shape.append(jax.ShapeDtypeStruct((b, N_KV_HEADS, n, HEAD_DIM), BF16))
    out_specs.append(pl.BlockSpec((1, N_KV_HEADS, tm, HEAD_DIM), lambda bi, i: (bi, 0, i, 0)))
    out_shape.append(jax.ShapeDtypeStruct((b, N_KV_HEADS, n, LANES), BF16))
    out_specs.append(pl.BlockSpec((1, N_KV_HEADS, tm, LANES), lambda bi, i: (bi, 0, i, 0)))
    return pl.pallas_call(
        functools.partial(_qkv_kernel, with_q=with_q),
        grid=(b, n // tm),
        in_specs=[
            pl.BlockSpec((1, tm, d), lambda bi, i: (bi, i, 0)),
            pl.BlockSpec((1, 1, d), mod_idx),
            pl.BlockSpec((1, 1, d), mod_idx),
            pl.BlockSpec((1, d), lambda bi, i: (0, 0)),
            pl.BlockSpec((d, ncol), lambda bi, i: (0, 0)),
            pl.BlockSpec((2, LANES), lambda bi, i: (0, 0)),
            pl.BlockSpec((LANES, LANES), lambda bi, i: (0, 0)),
            pl.BlockSpec((tm, LANES), lambda bi, i: (i, 0)),
            pl.BlockSpec((tm, LANES), lambda bi, i: (i, 0)),
        ],
        out_specs=out_specs,
        out_shape=out_shape,
        compiler_params=_params(("arbitrary", "arbitrary")),
        name="qkv" if with_q else "ctx_kv",
    )(x, shift, scale, gain, w, head_gain, ebd, cos, sin)


def _attn_kernel(q_ref, k_ref, v_ref, kc_ref, vc_ref, o_ref, *, tq, tk):
    m_rows = GQA_GROUP * tq
    q = q_ref[0].reshape(m_rows, HEAD_DIM)
    m = jnp.full((m_rows, 1), -jnp.inf, F32)
    acc = jnp.zeros((m_rows, LANES), F32)
    chunks = []
    for kr, vr in ((k_ref, v_ref), (kc_ref, vc_ref)):
        size = min(tk, kr.shape[2])
        chunks += [(kr, vr, c, size) for c in range(kr.shape[2] // size)]
    for kr, vr, c, size in chunks:
        kc = kr[0, 0, c * size:(c + 1) * size, :]
        vc = vr[0, 0, c * size:(c + 1) * size, :]
        s = lax.dot_general(q, kc, (((1,), (1,)), ((), ())), preferred_element_type=F32)
        m_new = jnp.maximum(m, jnp.max(s, axis=-1, keepdims=True))
        alpha = jnp.exp2(m - m_new)
        p = jnp.exp2(s - m_new)
        acc = alpha * acc + _dot(p.astype(BF16), vc)
        m = m_new
    o = acc * (1.0 / pltpu.roll(acc, HEAD_DIM, 1))
    o_ref[0] = jnp.concatenate(
        [o[g * tq:(g + 1) * tq, :HEAD_DIM] for g in range(GQA_GROUP)], axis=-1).astype(BF16)


def _attention(q, k, v, k_ctx, v_ctx):
    b, _, n, _ = q.shape
    n_ctx = k_ctx.shape[2]
    tq = min(256, n)
    tk = 256
    assert n % min(tk, n) == 0 and n_ctx % min(tk, n_ctx) == 0
    kv_spec = lambda rows, width: pl.BlockSpec((1, 1, rows, width), lambda bi, kh, qi: (bi, kh, 0, 0))
    return pl.pallas_call(
        functools.partial(_attn_kernel, tq=tq, tk=tk),
        grid=(b, N_KV_HEADS, n // tq),
        in_specs=[
            pl.BlockSpec((1, GQA_GROUP, tq, HEAD_DIM), lambda bi, kh, qi: (bi, kh, qi, 0)),
            kv_spec(n, HEAD_DIM), kv_spec(n, LANES), kv_spec(n_ctx, HEAD_DIM), kv_spec(n_ctx, LANES),
        ],
        out_specs=pl.BlockSpec((1, tq, GQA_GROUP * HEAD_DIM), lambda bi, kh, qi: (bi, qi, kh)),
        out_shape=jax.ShapeDtypeStruct((b, n, N_HEADS * HEAD_DIM), BF16),
        compiler_params=_params(("arbitrary", "arbitrary", "arbitrary")),
        name="attn",
    )(q, k, v, k_ctx, v_ctx)


def _proj_moe_kernel(*refs, tiles_per_batch, folded):
    if folded:
        dif_ref, sum_ref, sumb_ref, *refs, a_scr = refs
        j = pl.program_id(0) % tiles_per_batch

        @pl.when(j < tiles_per_batch // 2)
        def _():
            a_scr[...] = dif_ref[0]

        @pl.when(j >= tiles_per_batch // 2)
        def _():
            rows = _dot(_reversal(a_scr.shape[0]), sum_ref[0])
            row = lax.broadcasted_iota(jnp.int32, rows.shape, 0)
            a_scr[...] = jnp.where(row == 0, sumb_ref[0, 0:1, :].astype(F32), rows).astype(BF16)

        a = a_scr[...]
    else:
        a_ref, *refs = refs
        a = a_ref[...]
    (wo_ref, x_ref, g1_ref, gain_ref, sh_ref, sc_ref, wrh_ref, wrl_ref, br_ref,
     x1_ref, s_ref, gp_ref, pc_ref) = refs
    x1 = x_ref[...] + g1_ref[0] * _dot(a, wo_ref[...])
    x1_ref[...] = x1
    h2 = _modulate(x1, gain_ref[...], sh_ref[0], sc_ref[0])
    h_hi, h_lo = _split_bf16(h2)
    logits = (_dot(h_hi, wrh_ref[...]) + _dot(h_lo, wrh_ref[...]) + _dot(h_hi, wrl_ref[...])
              + br_ref[...])
    tm = logits.shape[0]
    lane = lax.broadcasted_iota(jnp.int32, (tm, LANES), 1)
    lane_f = lane.astype(F32)

    def top(vals):
        best = jnp.max(vals, axis=-1, keepdims=True)
        idx = jnp.min(jnp.where(vals == best, lane_f, float(LANES)), axis=-1, keepdims=True)
        return best, idx

    lg = jnp.where(lane < N_GROUPS, logits, NEG_BIG)
    g_max, g_sel = top(lg)
    p_grp = 1.0 / jnp.sum(jnp.where(lane < N_GROUPS, jnp.exp(lg - g_max), 0.0), axis=-1, keepdims=True)
    lo = N_GROUPS + EXPERTS_PER_GROUP * g_sel
    le = jnp.where((lane_f >= lo) & (lane_f < lo + EXPERTS_PER_GROUP), logits, NEG_BIG)
    t1, i1 = top(le)
    t2, i2 = top(jnp.where(lane_f == i1, NEG_BIG, le))
    e21 = jnp.exp(t2 - t1)
    w1 = 1.0 / (1.0 + e21)
    gate0 = p_grp * w1
    gate1 = p_grp * (e21 * w1)
    e0 = i1 - N_GROUPS
    e1 = i2 - N_GROUPS

    onehot = jnp.where((lane_f == e0) | (lane_f == e1), 1.0, 0.0)
    row = lax.broadcasted_iota(jnp.int32, (tm, tm), 0)
    col = lax.broadcasted_iota(jnp.int32, (tm, tm), 1)
    lower = jnp.where(col < row, 1.0, 0.0).astype(BF16)
    before = _dot(lower, onehot.astype(BF16))
    cnt = jnp.sum(onehot, axis=0, keepdims=True)
    pc = jnp.floor((cnt + (RUN_ALIGN - 1)) * (1.0 / RUN_ALIGN))
    urow = lax.broadcasted_iota(jnp.int32, (LANES, LANES), 0)
    ucol = lax.broadcasted_iota(jnp.int32, (LANES, LANES), 1)
    upper = jnp.where(urow < ucol, 1.0, 0.0).astype(BF16)
    run_start = _dot(jnp.broadcast_to(pc, (8, LANES)).astype(BF16), upper)[0:1] * RUN_ALIGN
    base = run_start + before
    pos0 = jnp.sum(jnp.where(lane_f == e0, base, 0.0), axis=-1, keepdims=True)
    pos1 = jnp.sum(jnp.where(lane_f == e1, base, 0.0), axis=-1, keepdims=True)
    gp = jnp.where(lane == 0, gate0, jnp.where(lane == 1, gate1, jnp.where(lane == 2, pos0,
                                                                       jnp.where(lane == 3, pos1, 0.0))))
    gp_ref[...] = gp
    pc_ref[...] = jnp.broadcast_to(pc, (8, LANES)).astype(jnp.int32)

    pos_t = jnp.where(lane == 0, pos0, jnp.where(lane == 1, pos1, -1.0)).T
    sr = s_ref.shape[0]
    srow = lax.broadcasted_iota(jnp.int32, (sr, tm), 0).astype(F32)
    perm = jnp.where((srow == pos_t[0:1, :]) | (srow == pos_t[1:2, :]), 1.0, 0.0).astype(BF16)
    s_ref[...] = _pack_halves(_dot(perm, h_hi))


def _proj_moe(a, w_o, x, g1, gain, shift, scale, wr_hi, wr_lo, b_r, *, tokens_per_batch):
    t, d = x.shape
    tm = min(MOE_TILE, tokens_per_batch)
    nt = t // tm
    sr = _sorted_rows(tm)
    per_b = tokens_per_batch // tm
    mod_idx = lambda i: (i // per_b, 0, 0)
    row_spec = lambda w: pl.BlockSpec((tm, w), lambda i: (i, 0))
    full = lambda s: pl.BlockSpec(s, lambda i: (0,) * len(s))
    folded = isinstance(a, tuple)
    if folded:
        assert per_b % 2 == 0
        half = per_b // 2
        a_inputs = (a[0], a[1], a[1])
        a_specs = [
            pl.BlockSpec((1, tm, d), lambda i: (i // per_b, jnp.minimum(i % per_b, half - 1), 0)),
            pl.BlockSpec((1, tm, d), lambda i: (i // per_b, jnp.clip(per_b - 1 - i % per_b, 0, half - 1), 0)),
            pl.BlockSpec((1, 8, d), lambda i: (i // per_b, (tm // 8) * jnp.clip(per_b - i % per_b, 1, half), 0)),
        ]
        scratch = [pltpu.VMEM((tm, d), BF16)]
    else:
        a_inputs = (a,)
        a_specs = [row_spec(d)]
        scratch = []
    return pl.pallas_call(
        functools.partial(_proj_moe_kernel, tiles_per_batch=per_b, folded=folded),
        grid=(nt,),
        in_specs=a_specs + [
            full((d, d)), row_spec(d),
            pl.BlockSpec((1, 1, d), mod_idx), full((1, d)),
            pl.BlockSpec((1, 1, d), mod_idx), pl.BlockSpec((1, 1, d), mod_idx),
            full((d, LANES)), full((d, LANES)), full((1, LANES)),
        ],
        out_specs=[row_spec(d), pl.BlockSpec((sr, d // 2), lambda i: (i, 0)), row_spec(LANES),
                   pl.BlockSpec((8, LANES), lambda i: (i, 0))],
        out_shape=[
            jax.ShapeDtypeStruct((t, d), F32), jax.ShapeDtypeStruct((nt * sr, d // 2), jnp.uint32),
            jax.ShapeDtypeStruct((t, LANES), F32), jax.ShapeDtypeStruct((nt * 8, LANES), jnp.int32),
        ],
        scratch_shapes=scratch,
        compiler_params=_params(("arbitrary",)),
        name="proj_moe",
    )(*a_inputs, w_o, x, g1, gain, shift, scale, wr_hi, wr_lo, b_r)


def _rows(ref, start_units, n_units):
    return ref.at[pl.ds(pl.multiple_of(start_units * RUN_ALIGN, RUN_ALIGN), n_units * RUN_ALIGN)]


def _ffn_kernel(exp_nblk, exp_blk0, blk_e, blk_b0, blk_n, blk_tlo, blk_thi, seg_start, run_start,
                nused_ref, s_hbm, w1_ref, w3_ref, w2_ref, y_hbm, xg, yo, w1b, w3b, w2b, sem, osem,
                *, sr_units, nblk):
    e_step = pl.program_id(0)
    nused = nused_ref[0]
    bm = xg.shape[1]
    bm_units = bm // RUN_ALIGN

    def gather(b, slot):
        e = blk_e[b]
        b0 = blk_b0[b]

        def body(ti, carry):
            rs = seg_start[ti * N_EXPERTS + e]
            re = seg_start[(ti + 1) * N_EXPERTS + e]
            lo = jnp.maximum(rs, b0)
            n = jnp.minimum(re, b0 + bm_units) - lo

            @pl.when(n > 0)
            def _():
                src = ti * sr_units + run_start[ti * N_EXPERTS + e] + (lo - rs)
                pltpu.make_async_copy(_rows(s_hbm, src, n), _rows(xg.at[slot], lo - b0, n),
                                      sem.at[slot]).start()

            return carry

        lax.fori_loop(blk_tlo[b], blk_thi[b], body, 0)

    def write_back(g, slot):
        return pltpu.make_async_copy(yo.at[slot], y_hbm.at[pl.ds(pl.multiple_of(g * bm, bm), bm)],
                                     osem.at[slot])

    def emit(g, fill):
        slot = g % 2

        @pl.when(g >= 2)
        def _():
            write_back(g - 2, slot).wait()

        fill(slot)
        write_back(g, slot).start()

    @pl.when(e_step == 0)
    def _():
        xg[...] = jnp.zeros_like(xg)
        gather(0, 0)

    w1b[...] = w1_ref[0, 0].astype(BF16)
    w3b[...] = w3_ref[0, 0].astype(BF16)
    w2b[...] = w2_ref[0, 0].astype(BF16)

    def block(jb, carry):
        g = exp_blk0[e_step] + jb

        @pl.when(g + 1 < nused)
        def _():
            gather(g + 1, (g + 1) % 2)

        def fill(slot):
            pltpu.make_async_copy(_rows(s_hbm, 0, blk_n[g]), _rows(xg.at[slot], 0, blk_n[g]),
                                  sem.at[slot]).wait()
            x_lo, x_hi = _unpack_halves(xg[slot])
            half = x_lo.shape[1]
            a = _dot(x_lo, w1b[:half, :]) + _dot(x_hi, w1b[half:, :])
            b = _dot(x_lo, w3b[:half, :]) + _dot(x_hi, w3b[half:, :])
            mid = (a / (1.0 + jnp.exp(-a))) * b
            y = _dot(mid.astype(BF16), w2b[...])
            yo[slot] = _pack_halves(y.astype(BF16).astype(F32))

        emit(g, fill)
        return carry

    lax.fori_loop(0, exp_nblk[e_step], block, 0)

    @pl.when(e_step == pl.num_programs(0) - 1)
    def _():
        def tail(g, carry):
            def fill(slot):
                yo[slot] = jnp.zeros(yo.shape[1:], yo.dtype)

            emit(g, fill)
            return carry

        lax.fori_loop(nused, nblk, tail, 0)
        write_back(nblk - 2, nblk % 2).wait()
        write_back(nblk - 1, (nblk - 1) % 2).wait()


def _ffn(tables, s_tiles, w1, w3, w2, *, layer, nblk, sr):
    bm = FFN_BLOCK
    d = w1.shape[2]
    de = w1.shape[3]
    nsp = len(tables)
    assert nblk >= 2
    grid_spec = pltpu.PrefetchScalarGridSpec(
        num_scalar_prefetch=nsp,
        grid=(N_EXPERTS,),
        in_specs=[
            pl.BlockSpec(memory_space=pl.ANY),
            pl.BlockSpec((1, 1, d, de), lambda e, *_: (layer, e, 0, 0)),
            pl.BlockSpec((1, 1, d, de), lambda e, *_: (layer, e, 0, 0)),
            pl.BlockSpec((1, 1, de, d), lambda e, *_: (layer, e, 0, 0)),
        ],
        out_specs=pl.BlockSpec(memory_space=pl.ANY),
        scratch_shapes=[pltpu.VMEM((2, bm, d // 2), jnp.uint32), pltpu.VMEM((2, bm, d // 2), jnp.uint32),
                        pltpu.VMEM((d, de), BF16), pltpu.VMEM((d, de), BF16), pltpu.VMEM((de, d), BF16),
                        pltpu.SemaphoreType.DMA((2,)), pltpu.SemaphoreType.DMA((2,))],
    )
    return pl.pallas_call(
        functools.partial(_ffn_kernel, sr_units=sr // RUN_ALIGN, nblk=nblk),
        grid_spec=grid_spec,
        out_shape=jax.ShapeDtypeStruct((nblk * bm, d // 2), jnp.uint32),
        compiler_params=_params(("arbitrary",)),
        name="ffn",
    )(*tables, s_tiles, w1, w3, w2)


def _combine_kernel(run_len, run_start, seg_start, exp_start, tile_rows,
                    y_hbm, x_ref, g2_ref, gp_ref, fgain_ref, o_ref, ybuf, sem, *, nsteps, final):
    i = pl.program_id(0)

    def gather(ti, slot):
        for e in range(N_EXPERTS):
            n = run_len[ti * N_EXPERTS + e]

            @pl.when(n > 0)
            def _():
                src = exp_start[e] + seg_start[ti * N_EXPERTS + e]
                pltpu.make_async_copy(_rows(y_hbm, src, n),
                                      _rows(ybuf.at[slot], run_start[ti * N_EXPERTS + e], n),
                                      sem.at[slot]).start()

    @pl.when(i == 0)
    def _():
        ybuf[...] = jnp.zeros_like(ybuf)
        gather(0, 0)

    @pl.when(i + 1 < nsteps)
    def _():
        gather(i + 1, (i + 1) % 2)

    slot = i % 2
    pltpu.make_async_copy(_rows(y_hbm, 0, tile_rows[i]), _rows(ybuf.at[slot], 0, tile_rows[i]),
                          sem.at[slot]).wait()
    gp = gp_ref[...]
    tm = gp.shape[0]
    sr = ybuf.shape[1]
    col = lax.broadcasted_iota(jnp.int32, (tm, sr), 1).astype(F32)
    weights = (jnp.where(col == gp[:, 2:3], gp[:, 0:1], 0.0)
               + jnp.where(col == gp[:, 3:4], gp[:, 1:2], 0.0)).astype(BF16)
    y_lo, y_hi = _unpack_halves(ybuf[slot])
    y = jnp.concatenate([_dot(weights, y_lo), _dot(weights, y_hi)], axis=-1)
    x2 = x_ref[...] + g2_ref[0] * y
    if final:
        ms = jnp.mean(x2 * x2, axis=-1, keepdims=True)
        x2 = (x2 * lax.rsqrt(ms + EPS)) * fgain_ref[...]
    o_ref[...] = x2


def _combine(tables, yb, x1, g2, gp, fgain, *, tokens_per_batch, final):
    t, d = x1.shape
    tm = min(MOE_TILE, tokens_per_batch)
    nsteps = t // tm
    per_b = tokens_per_batch // tm
    sr = _sorted_rows(tm)
    grid_spec = pltpu.PrefetchScalarGridSpec(
        num_scalar_prefetch=len(tables),
        grid=(nsteps,),
        in_specs=[
            pl.BlockSpec(memory_space=pl.ANY),
            pl.BlockSpec((tm, d), lambda i, *_: (i, 0)),
            pl.BlockSpec((1, 1, d), lambda i, *_: (i // per_b, 0, 0)),
            pl.BlockSpec((tm, LANES), lambda i, *_: (i, 0)),
            pl.BlockSpec((1, d), lambda i, *_: (0, 0)),
        ],
        out_specs=pl.BlockSpec((tm, d), lambda i, *_: (i, 0)),
        scratch_shapes=[pltpu.VMEM((2, sr, d // 2), jnp.uint32), pltpu.SemaphoreType.DMA((2,))],
    )
    return pl.pallas_call(
        functools.partial(_combine_kernel, nsteps=nsteps, final=final),
        grid_spec=grid_spec,
        out_shape=jax.ShapeDtypeStruct((t, d), F32),
        compiler_params=_params(("arbitrary",)),
        name="combine_final" if final else "combine",
    )(*tables, yb, x1, g2, gp, fgain)


def _reversal(size):
    row = lax.broadcasted_iota(jnp.int32, (size, size), 0)
    col = lax.broadcasted_iota(jnp.int32, (size, size), 1)
    return jnp.where(col == size - row, 1.0, 0.0).astype(BF16)


def _fmix_fold_kernel(x_ref, xm_ref, xb_ref, xc_ref, sh_ref, sc_ref, gain_ref, cc_ref, sc_mat_ref,
                      a_ref, b_ref, amid_ref):
    mod = lambda v: _modulate(v, gain_ref[...], sh_ref[0], sc_ref[0])
    h = mod(x_ref[...])
    ft = h.shape[0]
    m_hi, m_lo = _split_bf16(mod(xm_ref[...]))
    rev = _reversal(ft)
    row = lax.broadcasted_iota(jnp.int32, h.shape, 0)
    hr = jnp.where(row == 0, mod(xb_ref[...])[0:1], _dot(rev, m_hi) + _dot(rev, m_lo))
    he = (h + hr).astype(BF16)
    ho = (h - hr).astype(BF16)
    hc = mod(xc_ref[...]).astype(BF16)
    gw = cc_ref.shape[0]
    for g in range(FOURIER_GROUPS):
        cols = slice(g * gw, (g + 1) * gw)
        a_ref[:, cols] = _dot(he[:, cols], cc_ref[...]).astype(BF16)
        b_ref[:, cols] = _dot(ho[:, cols], sc_mat_ref[...]).astype(BF16)
        amid_ref[:, cols] = _dot(hc[:, cols], cc_ref[...])


def _fmix_fold(x, shift, scale, gain, cc, sc, *, bsz, n):
    t, d = x.shape
    hn = n // 2
    ft = min(512, hn)
    tpb = n // ft
    half = tpb // 2
    gw = d // FOURIER_GROUPS
    mod_idx = lambda b, j: (b, 0, 0)
    return pl.pallas_call(
        _fmix_fold_kernel,
        grid=(bsz, half),
        in_specs=[
            pl.BlockSpec((ft, d), lambda b, j: (b * tpb + j, 0)),
            pl.BlockSpec((ft, d), lambda b, j: (b * tpb + tpb - 1 - j, 0)),
            pl.BlockSpec((8, d), lambda b, j: (b * (n // 8) + (ft // 8) * ((tpb - j) % tpb), 0)),
            pl.BlockSpec((8, d), lambda b, j: (b * (n // 8) + hn // 8, 0)),
            pl.BlockSpec((1, 1, d), mod_idx), pl.BlockSpec((1, 1, d), mod_idx),
            pl.BlockSpec((1, d), lambda b, j: (0, 0)),
            pl.BlockSpec((gw, gw), lambda b, j: (0, 0)), pl.BlockSpec((gw, gw), lambda b, j: (0, 0)),
        ],
        out_specs=[pl.BlockSpec((ft, d), lambda b, j: (b * half + j, 0)),
                   pl.BlockSpec((ft, d), lambda b, j: (b * half + j, 0)),
                   pl.BlockSpec((8, d), lambda b, j: (b, 0))],
        out_shape=[jax.ShapeDtypeStruct((bsz * hn, d), BF16), jax.ShapeDtypeStruct((bsz * hn, d), BF16),
                   jax.ShapeDtypeStruct((bsz * 8, d), F32)],
        compiler_params=_params(("arbitrary", "arbitrary")),
        name="fmix_fold",
    )(x, x, x, x, shift, scale, gain, cc, sc)


def _fmix_pos_kernel(ch_ref, sh_ref, sign_ref, a_ref, b_ref, amid_ref, dif_ref, sum_ref, p_acc, q_acc):
    k = pl.program_id(2)

    @pl.when(k == 0)
    def _():
        p_acc[...] = jnp.zeros_like(p_acc)
        q_acc[...] = jnp.zeros_like(q_acc)

    p_acc[...] += _dot(ch_ref[...], a_ref[0])
    q_acc[...] += _dot(sh_ref[...], b_ref[0])

    @pl.when(k == pl.num_programs(2) - 1)
    def _():
        p = p_acc[...] + sign_ref[...] * amid_ref[0, 0:1, :]
        dif_ref[0] = (p - q_acc[...]).astype(BF16)
        sum_ref[0] = (p + q_acc[...]).astype(BF16)


def _fmix_pos(ch, sh, sign, a, b, amid):
    bsz, hn, d = a.shape
    mp = ch.shape[0]
    tmr = mp // 2
    tk = min(512, hn)
    out = jax.ShapeDtypeStruct((bsz, mp, d), BF16)
    return pl.pallas_call(
        _fmix_pos_kernel,
        grid=(mp // tmr, bsz, hn // tk),
        in_specs=[
            pl.BlockSpec((tmr, tk), lambda i, bi, k: (i, k)),
            pl.BlockSpec((tmr, tk), lambda i, bi, k: (i, k)),
            pl.BlockSpec((tmr, 1), lambda i, bi, k: (i, 0)),
            pl.BlockSpec((1, tk, d), lambda i, bi, k: (bi, k, 0)),
            pl.BlockSpec((1, tk, d), lambda i, bi, k: (bi, k, 0)),
            pl.BlockSpec((1, 8, d), lambda i, bi, k: (bi, 0, 0)),
        ],
        out_specs=[pl.BlockSpec((1, tmr, d), lambda i, bi, k: (bi, i, 0)),
                   pl.BlockSpec((1, tmr, d), lambda i, bi, k: (bi, i, 0))],
        out_shape=[out, out],
        scratch_shapes=[pltpu.VMEM((tmr, d), F32), pltpu.VMEM((tmr, d), F32)],
        compiler_params=_params(("arbitrary", "arbitrary", "arbitrary")),
        name="fmix_pos",
    )(ch, sh, sign, a, b, amid)


def _rope_tables(n):
    half = ROPE_AXIS_DIM // 2
    rows = n // GRID_W
    inv = ROPE_THETA ** (-jnp.arange(half, dtype=F32) / half)
    pos = jnp.arange(n, dtype=jnp.int32)
    ang_r = (pos // GRID_W).astype(F32)[:, None] * inv
    ang_c = (pos % GRID_W).astype(F32)[:, None] * inv
    del rows
    cos = jnp.concatenate([jnp.cos(ang_r)] * 2 + [jnp.cos(ang_c)] * 2, axis=-1)
    sin = jnp.concatenate([-jnp.sin(ang_r), jnp.sin(ang_r), -jnp.sin(ang_c), jnp.sin(ang_c)], axis=-1)
    return jnp.tile(cos, (1, LANES // HEAD_DIM)), jnp.tile(sin, (1, LANES // HEAD_DIM))


def _dft_mats(n, scale):
    r = 1
    while r * r < n:
        r *= 2
    c = n // r
    k = jnp.arange(n, dtype=jnp.int32)[:, None]
    ang_a = ((k * jnp.arange(r, dtype=jnp.int32)[None, :] * c) % n).astype(F32) * (2.0 * math.pi / n)
    ang_b = ((k * jnp.arange(c, dtype=jnp.int32)[None, :]) % n).astype(F32) * (2.0 * math.pi / n)
    ca, sa, cb, sb = jnp.cos(ang_a), jnp.sin(ang_a), jnp.cos(ang_b), jnp.sin(ang_b)
    cosm = (ca[:, :, None] * cb[:, None, :] - sa[:, :, None] * sb[:, None, :]).reshape(n, n)
    sinm = (sa[:, :, None] * cb[:, None, :] + ca[:, :, None] * sb[:, None, :]).reshape(n, n)
    return (cosm * scale).astype(BF16), (sinm * scale).astype(BF16)


DFT_ROW_PAD = 32


def _dft_half(n):
    hn = n // 2
    mp = hn + DFT_ROW_PAD
    r = 1
    while r * r < hn:
        r *= 2
    c = hn // r
    k = jnp.arange(mp, dtype=jnp.int32)[:, None]
    valid = k <= hn
    w = 2.0 * math.pi / n
    ang_a = ((k * (jnp.arange(r, dtype=jnp.int32)[None, :] * c)) % n).astype(F32) * w
    ang_b = ((k * jnp.arange(c, dtype=jnp.int32)[None, :]) % n).astype(F32) * w
    ca, sa, cb, sb = jnp.cos(ang_a), jnp.sin(ang_a), jnp.cos(ang_b), jnp.sin(ang_b)
    cosm = (ca[:, :, None] * cb[:, None, :] - sa[:, :, None] * sb[:, None, :]).reshape(mp, hn)
    sinm = (sa[:, :, None] * cb[:, None, :] + ca[:, :, None] * sb[:, None, :]).reshape(mp, hn)
    scale = n ** -0.5
    col_w = jnp.where(jnp.arange(hn)[None, :] == 0, 0.5 * scale, scale)
    cosm = jnp.where(valid, cosm * col_w, 0.0).astype(BF16)
    sinm = jnp.where(valid, sinm * scale, 0.0).astype(BF16)
    sign = jnp.where(valid, jnp.where(k % 2 == 0, scale, -scale), 0.0).astype(F32)
    return cosm, sinm, sign


def _moe(a, w_o, x, g1, g2, gain, shift, scale, w_rg, b_rg, w_re, b_re, w1, w3, w2, fgain, *,
         layer, tokens_per_batch, final):
    t, d = x.shape
    pad = LANES - N_GROUPS - N_EXPERTS
    w_r = jnp.concatenate([w_rg, w_re, jnp.zeros((d, pad), F32)], axis=1)
    b_r = jnp.concatenate([b_rg, b_re, jnp.zeros((pad,), F32)])[None, :]
    wr_hi = w_r.astype(BF16)
    wr_lo = (w_r - wr_hi.astype(F32)).astype(BF16)
    x1, s_tiles, gp, pc = _proj_moe(a, w_o, x, g1, gain, shift, scale, wr_hi, wr_lo, b_r,
                                    tokens_per_batch=tokens_per_batch)
    tm = min(MOE_TILE, tokens_per_batch)
    nt = t // tm
    sr = _sorted_rows(tm)
    bmu = FFN_BLOCK // RUN_ALIGN
    i32 = jnp.int32
    run_len = pc.reshape(nt, 8, LANES)[:, 0, :N_EXPERTS]
    seg_start = jnp.concatenate([jnp.zeros((1, N_EXPERTS), i32), jnp.cumsum(run_len, axis=0)])
    seg_len = seg_start[-1]
    seg_pad = (seg_len + bmu - 1) // bmu * bmu
    exp_end = jnp.cumsum(seg_pad)
    exp_start = exp_end - seg_pad
    run_start = jnp.cumsum(run_len, axis=1) - run_len
    tile_rows = run_start[:, -1] + run_len[:, -1]
    max_units = (2 * t) // RUN_ALIGN + nt * N_EXPERTS + N_EXPERTS * (bmu - 1)
    nblk = (max_units + bmu - 1) // bmu
    blk_start = jnp.arange(nblk, dtype=i32) * bmu
    blk_e = jnp.minimum(jnp.sum((exp_end[None, :] <= blk_start[:, None]).astype(i32), axis=1),
                        N_EXPERTS - 1)
    blk_b0 = blk_start - exp_start[blk_e]
    blk_n = jnp.clip(seg_len[blk_e] - blk_b0, 0, bmu)
    seg_e = seg_start.T[blk_e]
    blk_tlo = jnp.sum((seg_e[:, 1:] <= blk_b0[:, None]).astype(i32), axis=1)
    blk_thi = jnp.sum((seg_e[:, :-1] < (blk_b0 + bmu)[:, None]).astype(i32), axis=1)
    nused = (exp_end[-1] // bmu)[None]
    flat = lambda v: v.reshape(-1).astype(i32)
    yb = _ffn((seg_pad // bmu, exp_start // bmu, blk_e, blk_b0, blk_n, blk_tlo, blk_thi,
               flat(seg_start), flat(run_start), nused),
              s_tiles, w1, w3, w2, layer=layer, nblk=nblk, sr=sr)
    return _combine((flat(run_len), flat(run_start), flat(seg_start), exp_start, tile_rows),
                    yb, x1, g2, gp, fgain, tokens_per_batch=tokens_per_batch, final=final)


def kernel(x, c, ctx, c_ctx, w_mod, b_mod, norm_mix, norm_ffn, attn_w_qkv, attn_q_norm, attn_k_norm,
           attn_w_o, fourier_w_o, moe_w_rg, moe_b_rg, moe_w_re, moe_b_re, moe_w1, moe_w3, moe_w2,
           final_norm):
    bsz, n, d = x.shape
    t = bsz * n
    q_dim = N_HEADS * HEAD_DIM
    kv_dim = N_KV_HEADS * HEAD_DIM

    cond = jnp.zeros((16, d), F32).at[:bsz].set(c).at[bsz].set(c_ctx)
    mods = _ada(cond, w_mod, b_mod)

    def lat(layer, j):
        return mods[layer, :bsz, j * d:(j + 1) * d][:, None, :]

    def ctxm(layer, j):
        return mods[layer, bsz:bsz + 1, j * d:(j + 1) * d][:, None, :]

    w_qkv = attn_w_qkv[0].astype(BF16)
    qscale = (HEAD_DIM ** -0.5) * math.log2(math.e)
    head_gain = jnp.stack([jnp.tile(attn_q_norm[0] * qscale, LANES // HEAD_DIM),
                           jnp.tile(attn_k_norm[0], LANES // HEAD_DIM)])
    lane = jnp.arange(LANES)
    ebd = jnp.where((lane[:, None] // HEAD_DIM) == (lane[None, :] // HEAD_DIM),
                    1.0 / HEAD_DIM, 0.0).astype(BF16)
    cos, sin = _rope_tables(n)
    gain0 = norm_mix[0][None, :]
    q, k_l, v_l = _qkv(x, lat(0, 0), lat(0, 1), gain0, w_qkv, head_gain, ebd, cos, sin,
                       with_q=True, per_batch_mods=True)
    n_ctx = ctx.shape[1]
    k_c, v_c = _qkv(ctx, ctxm(0, 0), ctxm(0, 1), gain0, w_qkv[:, q_dim:], head_gain, ebd,
                    jnp.ones((n_ctx, LANES), F32), jnp.zeros((n_ctx, LANES), F32),
                    with_q=False, per_batch_mods=False)
    del kv_dim
    o = _attention(q, k_l, v_l, k_c, v_c).reshape(t, q_dim)

    xf = x.reshape(t, d)
    moe_w = lambda i: (moe_w_rg[i], moe_b_rg[i], moe_w_re[i], moe_b_re[i],
                       moe_w1, moe_w3, moe_w2)
    fgain = final_norm[None, :]
    x2 = _moe(o, attn_w_o[0].astype(BF16), xf, lat(0, 2), lat(0, 5), norm_ffn[0][None, :],
              lat(0, 3), lat(0, 4), *moe_w(0), fgain, layer=0, tokens_per_batch=n, final=False)

    gw = d // FOURIER_GROUPS
    cc, sc = _dft_mats(gw, gw ** -0.5)
    ch, sh, sign = _dft_half(n)
    a_e, b_o, a_mid = _fmix_fold(x2, lat(1, 0), lat(1, 1), norm_mix[1][None, :], cc, sc, bsz=bsz, n=n)
    f = tuple(_fmix_pos(ch, sh, sign, a_e.reshape(bsz, n // 2, d), b_o.reshape(bsz, n // 2, d),
                        a_mid.reshape(bsz, 8, d)))
    out = _moe(f, fourier_w_o[0].astype(BF16), x2, lat(1, 2), lat(1, 5), norm_ffn[1][None, :],
               lat(1, 3), lat(1, 4), *moe_w(1), fgain, layer=1, tokens_per_batch=n, final=True)
    return out.reshape(bsz, n, d)
```

```python
import functools
import math

import jax
import jax.numpy as jnp
from jax import lax
from jax.experimental import pallas as pl
from jax.experimental.pallas import tpu as pltpu

F32 = jnp.float32
BF16 = jnp.bfloat16

N_HEADS = 16
N_KV_HEADS = 4
HEAD_DIM = 64
GQA_GROUP = N_HEADS // N_KV_HEADS
GRID_W = 64
ROPE_AXIS_DIM = HEAD_DIM // 2
ROPE_THETA = 10000.0
FOURIER_GROUPS = 4
N_GROUPS = 4
EXPERTS_PER_GROUP = 8
N_EXPERTS = N_GROUPS * EXPERTS_PER_GROUP
EPS = 1e-6
LANES = 128
NEG_BIG = -1e30
VMEM_LIMIT = 48 * 1024 * 1024

FFN_BLOCK = 512
MOE_TILE = 512
RUN_ALIGN = 8


def _sorted_rows(tile_tokens):
    return 2 * tile_tokens + N_EXPERTS * RUN_ALIGN


def _dot(a, b):
    return jnp.dot(a, b, preferred_element_type=F32)


def _split_bf16(a):
    hi = a.astype(BF16)
    lo = (a - hi.astype(F32)).astype(BF16)
    return hi, lo


_HI16 = 0xFFFF0000


def _pack_halves(v):
    w = v.shape[1] // 2
    lo = lax.bitcast_convert_type(v[:, :w], jnp.uint32) >> 16
    hi = lax.bitcast_convert_type(v[:, w:], jnp.uint32) & jnp.uint32(_HI16)
    return hi | lo


def _unpack_halves(p):
    lo = lax.bitcast_convert_type(p << 16, F32).astype(BF16)
    hi = lax.bitcast_convert_type(p & jnp.uint32(_HI16), F32).astype(BF16)
    return lo, hi


def _params(sem):
    return pltpu.CompilerParams(dimension_semantics=sem, vmem_limit_bytes=VMEM_LIMIT)


def _ada_kernel(c_ref, w_ref, b_ref, o_ref):
    c = c_ref[...]
    s = c / (1.0 + jnp.exp(-c))
    s_hi, s_lo = _split_bf16(s)
    w_hi, w_lo = _split_bf16(w_ref[0])
    o_ref[0] = _dot(s_hi, w_hi) + _dot(s_hi, w_lo) + _dot(s_lo, w_hi) + b_ref[0]


def _ada(cond, w_mod, b_mod):
    depth, d, d6 = w_mod.shape
    rows = cond.shape[0]
    tn = 1536
    return pl.pallas_call(
        _ada_kernel,
        grid=(depth, d6 // tn),
        in_specs=[
            pl.BlockSpec((rows, d), lambda l, j: (0, 0)),
            pl.BlockSpec((1, d, tn), lambda l, j: (l, 0, j)),
            pl.BlockSpec((1, 1, tn), lambda l, j: (l, 0, j)),
        ],
        out_specs=pl.BlockSpec((1, rows, tn), lambda l, j: (l, 0, j)),
        out_shape=jax.ShapeDtypeStruct((depth, rows, d6), F32),
        compiler_params=_params(("arbitrary", "arbitrary")),
        name="ada",
    )(cond, w_mod, b_mod.reshape(depth, 1, d6))


def _modulate(xf, gain, shift, scale):
    ms = jnp.mean(xf * xf, axis=-1, keepdims=True)
    return (xf * lax.rsqrt(ms + EPS)) * (gain * (1.0 + scale)) + shift


def _qkv_kernel(x_ref, sh_ref, sc_ref, gain_ref, w_ref, hg_ref, ebd_ref, cos_ref, sin_ref,
                *out_refs, with_q):
    h = _modulate(x_ref[0], gain_ref[...], sh_ref[0], sc_ref[0])
    qkv = _dot(h.astype(BF16), w_ref[...])
    tm = qkv.shape[0]
    lane = lax.broadcasted_iota(jnp.int32, (tm, LANES), 1)
    first_half = (lane & 16) == 0
    low = lane < HEAD_DIM
    if with_q:
        q_ref, k_ref, v_ref = out_refs
        nq = N_HEADS * HEAD_DIM // LANES
    else:
        k_ref, v_ref = out_refs
        nq = 0
    nk = N_KV_HEADS * HEAD_DIM // LANES
    cos = cos_ref[...]
    sin = sin_ref[...]

    def norm_rope(c, gain_row):
        ms = _dot((c * c).astype(BF16), ebd_ref[...])
        cn = c * lax.rsqrt(ms + EPS) * gain_row
        partner = jnp.where(first_half, pltpu.roll(cn, LANES - 16, 1), pltpu.roll(cn, 16, 1))
        return cn * cos + partner * sin

    for j in range(nq):
        out = norm_rope(qkv[:, j * LANES:(j + 1) * LANES], hg_ref[0:1, :]).astype(BF16)
        q_ref[0, 2 * j] = out[:, :HEAD_DIM]
        q_ref[0, 2 * j + 1] = out[:, HEAD_DIM:]
    for j in range(nk):
        out = norm_rope(qkv[:, (nq + j) * LANES:(nq + j + 1) * LANES], hg_ref[1:2, :]).astype(BF16)
        k_ref[0, 2 * j] = out[:, :HEAD_DIM]
        k_ref[0, 2 * j + 1] = out[:, HEAD_DIM:]
    for j in range(nk):
        c = qkv[:, (nq + nk + j) * LANES:(nq + nk + j + 1) * LANES]
        v_ref[0, 2 * j] = jnp.where(low, c, 1.0).astype(BF16)
        v_ref[0, 2 * j + 1] = jnp.where(low, pltpu.roll(c, HEAD_DIM, 1), 1.0).astype(BF16)


def _qkv(x, shift, scale, gain, w, head_gain, ebd, cos, sin, *, with_q, per_batch_mods):
    b, n, d = x.shape
    tm = min(512, n)
    ncol = w.shape[1]
    mod_idx = (lambda bi, i: (bi, 0, 0)) if per_batch_mods else (lambda bi, i: (0, 0, 0))
    out_shape = []
    out_specs = []
    if with_q:
        out_shape.append(jax.ShapeDtypeStruct((b, N_HEADS, n, HEAD_DIM), BF16))
        out_specs.append(pl.BlockSpec((1, N_HEADS, tm, HEAD_DIM), lambda bi, i: (bi, 0, i, 0)))
    out_shape.append(jax.ShapeDtypeStruct((b, N_KV_HEADS, n, HEAD_DIM), BF16))
    out_specs.append(pl.BlockSpec((1, N_KV_HEADS, tm, HEAD_DIM), lambda bi, i: (bi, 0, i, 0)))
    out_shape.append(jax.ShapeDtypeStruct((b, N_KV_HEADS, n, LANES), BF16))
    out_specs.append(pl.BlockSpec((1, N_KV_HEADS, tm, LANES), lambda bi, i: (bi, 0, i, 0)))
    return pl.pallas_call(
        functools.partial(_qkv_kernel, with_q=with_q),
        grid=(b, n // tm),
        in_specs=[
            pl.BlockSpec((1, tm, d), lambda bi, i: (bi, i, 0)),
            pl.BlockSpec((1, 1, d), mod_idx),
            pl.BlockSpec((1, 1, d), mod_idx),
            pl.BlockSpec((1, d), lambda bi, i: (0, 0)),
            pl.BlockSpec((d, ncol), lambda bi, i: (0, 0)),
            pl.BlockSpec((2, LANES), lambda bi, i: (0, 0)),
            pl.BlockSpec((LANES, LANES), lambda bi, i: (0, 0)),
            pl.BlockSpec((tm, LANES), lambda bi, i: (i, 0)),
            pl.BlockSpec((tm, LANES), lambda bi, i: (i, 0)),
        ],
        out_specs=out_specs,
        out_shape=out_shape,
        compiler_params=_params(("arbitrary", "arbitrary")),
        name="qkv" if with_q else "ctx_kv",
    )(x, shift, scale, gain, w, head_gain, ebd, cos, sin)


def _attn_kernel(q_ref, k_ref, v_ref, kc_ref, vc_ref, o_ref, *, tq, tk):
    m_rows = GQA_GROUP * tq
    q = q_ref[0].reshape(m_rows, HEAD_DIM)
    m = jnp.full((m_rows, 1), -jnp.inf, F32)
    acc = jnp.zeros((m_rows, LANES), F32)
    chunks = []
    for kr, vr in ((k_ref, v_ref), (kc_ref, vc_ref)):
        size = min(tk, kr.shape[2])
        chunks += [(kr, vr, c, size) for c in range(kr.shape[2] // size)]
    for kr, vr, c, size in chunks:
        kc = kr[0, 0, c * size:(c + 1) * size, :]
        vc = vr[0, 0, c * size:(c + 1) * size, :]
        s = lax.dot_general(q, kc, (((1,), (1,)), ((), ())), preferred_element_type=F32)
        m_new = jnp.maximum(m, jnp.max(s, axis=-1, keepdims=True))
        alpha = jnp.exp2(m - m_new)
        p = jnp.exp2(s - m_new)
        acc = alpha * acc + _dot(p.astype(BF16), vc)
        m = m_new
    o = acc * (1.0 / pltpu.roll(acc, HEAD_DIM, 1))
    o_ref[0] = jnp.concatenate(
        [o[g * tq:(g + 1) * tq, :HEAD_DIM] for g in range(GQA_GROUP)], axis=-1).astype(BF16)


def _attention(q, k, v, k_ctx, v_ctx):
    b, _, n, _ = q.shape
    n_ctx = k_ctx.shape[2]
    tq = min(256, n)
    tk = 256
    assert n % min(tk, n) == 0 and n_ctx % min(tk, n_ctx) == 0
    kv_spec = lambda rows, width: pl.BlockSpec((1, 1, rows, width), lambda bi, kh, qi: (bi, kh, 0, 0))
    return pl.pallas_call(
        functools.partial(_attn_kernel, tq=tq, tk=tk),
        grid=(b, N_KV_HEADS, n // tq),
        in_specs=[
            pl.BlockSpec((1, GQA_GROUP, tq, HEAD_DIM), lambda bi, kh, qi: (bi, kh, qi, 0)),
            kv_spec(n, HEAD_DIM), kv_spec(n, LANES), kv_spec(n_ctx, HEAD_DIM), kv_spec(n_ctx, LANES),
        ],
        out_specs=pl.BlockSpec((1, tq, GQA_GROUP * HEAD_DIM), lambda bi, kh, qi: (bi, qi, kh)),
        out_shape=jax.ShapeDtypeStruct((b, n, N_HEADS * HEAD_DIM), BF16),
        compiler_params=_params(("arbitrary", "arbitrary", "arbitrary")),
        name="attn",
    )(q, k, v, k_ctx, v_ctx)


def _proj_moe_kernel(*refs, tiles_per_batch, folded):
    if folded:
        dif_ref, sum_ref, sumb_ref, *refs, a_scr = refs
        j = pl.program_id(0) % tiles_per_batch

        @pl.when(j < tiles_per_batch // 2)
        def _():
            a_scr[...] = dif_ref[0]

        @pl.when(j >= tiles_per_batch // 2)
        def _():
            rows = _dot(_reversal(a_scr.shape[0]), sum_ref[0])
            row = lax.broadcasted_iota(jnp.int32, rows.shape, 0)
            a_scr[...] = jnp.where(row == 0, sumb_ref[0, 0:1, :].astype(F32), rows).astype(BF16)

        a = a_scr[...]
    else:
        a_ref, *refs = refs
        a = a_ref[...]
    (wo_ref, x_ref, g1_ref, gain_ref, sh_ref, sc_ref, wrh_ref, wrl_ref, br_ref,
     x1_ref, s_ref, gp_ref, pc_ref) = refs
    x1 = x_ref[...] + g1_ref[0] * _dot(a, wo_ref[...])
    x1_ref[...] = x1
    h2 = _modulate(x1, gain_ref[...], sh_ref[0], sc_ref[0])
    h_hi, h_lo = _split_bf16(h2)
    r_hi = _dot(h_hi, wrl_ref[...])
    logits = (r_hi[:, :LANES] + _dot(h_lo, wrh_ref[...]) + r_hi[:, LANES:]
              + br_ref[...])
    tm = logits.shape[0]
    lane = lax.broadcasted_iota(jnp.int32, (tm, LANES), 1)
    lane_f = lane.astype(F32)

    def top(vals):
        best = jnp.max(vals, axis=-1, keepdims=True)
        idx = jnp.min(jnp.where(vals == best, lane_f, float(LANES)), axis=-1, keepdims=True)
        return best, idx

    lg = jnp.where(lane < N_GROUPS, logits, NEG_BIG)
    g_max, g_sel = top(lg)
    p_grp = 1.0 / jnp.sum(jnp.where(lane < N_GROUPS, jnp.exp(lg - g_max), 0.0), axis=-1, keepdims=True)
    lo = N_GROUPS + EXPERTS_PER_GROUP * g_sel
    le = jnp.where((lane_f >= lo) & (lane_f < lo + EXPERTS_PER_GROUP), logits, NEG_BIG)
    t1, i1 = top(le)
    t2, i2 = top(jnp.where(lane_f == i1, NEG_BIG, le))
    e21 = jnp.exp(t2 - t1)
    w1 = 1.0 / (1.0 + e21)
    gate0 = p_grp * w1
    gate1 = p_grp * (e21 * w1)
    e0 = i1 - N_GROUPS
    e1 = i2 - N_GROUPS

    onehot = jnp.where((lane_f == e0) | (lane_f == e1), 1.0, 0.0)
    row = lax.broadcasted_iota(jnp.int32, (tm, tm), 0)
    col = lax.broadcasted_iota(jnp.int32, (tm, tm), 1)
    lower = jnp.where(col < row, 1.0, 0.0).astype(BF16)
    before = _dot(lower, onehot.astype(BF16))
    cnt = jnp.sum(onehot, axis=0, keepdims=True)
    pc = jnp.floor((cnt + (RUN_ALIGN - 1)) * (1.0 / RUN_ALIGN))
    urow = lax.broadcasted_iota(jnp.int32, (LANES, LANES), 0)
    ucol = lax.broadcasted_iota(jnp.int32, (LANES, LANES), 1)
    upper = jnp.where(urow < ucol, 1.0, 0.0).astype(BF16)
    run_start = _dot(jnp.broadcast_to(pc, (8, LANES)).astype(BF16), upper)[0:1] * RUN_ALIGN
    base = run_start + before
    pos0 = jnp.sum(jnp.where(lane_f == e0, base, 0.0), axis=-1, keepdims=True)
    pos1 = jnp.sum(jnp.where(lane_f == e1, base, 0.0), axis=-1, keepdims=True)
    gp = jnp.where(lane == 0, gate0, jnp.where(lane == 1, gate1, jnp.where(lane == 2, pos0,
                                                                       jnp.where(lane == 3, pos1, 0.0))))
    gp_ref[...] = gp
    pc_ref[...] = jnp.broadcast_to(pc, (8, LANES)).astype(jnp.int32)

    pos_t = jnp.where(lane == 0, pos0, jnp.where(lane == 1, pos1, -1.0)).T
    sr = s_ref.shape[0]
    srow = lax.broadcasted_iota(jnp.int32, (sr, tm), 0).astype(F32)
    perm = jnp.where((srow == pos_t[0:1, :]) | (srow == pos_t[1:2, :]), 1.0, 0.0).astype(BF16)
    s_ref[...] = _pack_halves(_dot(perm, h_hi))


def _proj_moe(a, w_o, x, g1, gain, shift, scale, wr_hi, wr_lo, b_r, *, tokens_per_batch):
    t, d = x.shape
    tm = min(MOE_TILE, tokens_per_batch)
    nt = t // tm
    sr = _sorted_rows(tm)
    per_b = tokens_per_batch // tm
    mod_idx = lambda i: (i // per_b, 0, 0)
    row_spec = lambda w: pl.BlockSpec((tm, w), lambda i: (i, 0))
    full = lambda s: pl.BlockSpec(s, lambda i: (0,) * len(s))
    folded = isinstance(a, tuple)
    if folded:
        assert per_b % 2 == 0
        half = per_b // 2
        a_inputs = (a[0], a[1], a[1])
        a_specs = [
            pl.BlockSpec((1, tm, d), lambda i: (i // per_b, jnp.minimum(i % per_b, half - 1), 0)),
            pl.BlockSpec((1, tm, d), lambda i: (i // per_b, jnp.clip(per_b - 1 - i % per_b, 0, half - 1), 0)),
            pl.BlockSpec((1, 8, d), lambda i: (i // per_b, (tm // 8) * jnp.clip(per_b - i % per_b, 1, half), 0)),
        ]
        scratch = [pltpu.VMEM((tm, d), BF16)]
    else:
        a_inputs = (a,)
        a_specs = [row_spec(d)]
        scratch = []
    return pl.pallas_call(
        functools.partial(_proj_moe_kernel, tiles_per_batch=per_b, folded=folded),
        grid=(nt,),
        in_specs=a_specs + [
            full((d, d)), row_spec(d),
            pl.BlockSpec((1, 1, d), mod_idx), full((1, d)),
            pl.BlockSpec((1, 1, d), mod_idx), pl.BlockSpec((1, 1, d), mod_idx),
            full((d, LANES)), full((d, 2 * LANES)), full((1, LANES)),
        ],
        out_specs=[row_spec(d), pl.BlockSpec((sr, d // 2), lambda i: (i, 0)), row_spec(LANES),
                   pl.BlockSpec((8, LANES), lambda i: (i, 0))],
        out_shape=[
            jax.ShapeDtypeStruct((t, d), F32), jax.ShapeDtypeStruct((nt * sr, d // 2), jnp.uint32),
            jax.ShapeDtypeStruct((t, LANES), F32), jax.ShapeDtypeStruct((nt * 8, LANES), jnp.int32),
        ],
        scratch_shapes=scratch,
        compiler_params=_params(("arbitrary",)),
        name="proj_moe",
    )(*a_inputs, w_o, x, g1, gain, shift, scale, wr_hi, wr_lo, b_r)


def _rows(ref, start_units, n_units):
    return ref.at[pl.ds(pl.multiple_of(start_units * RUN_ALIGN, RUN_ALIGN), n_units * RUN_ALIGN)]


def _ffn_kernel(exp_nblk, exp_blk0, blk_e, blk_b0, blk_n, blk_tlo, blk_thi, seg_start, run_start,
                nused_ref, s_hbm, w1_ref, w3_ref, w2_ref, y_hbm, xg, yo, w1b, w3b, w2b, sem, osem,
                *, sr_units, nblk):
    e_step = pl.program_id(0)
    nused = nused_ref[0]
    n_slots, bm, _ = xg.shape
    n_ahead = n_slots - 1
    bm_units = bm // RUN_ALIGN

    def gather(b, slot):
        e = blk_e[b]
        b0 = blk_b0[b]

        def body(ti, carry):
            rs = seg_start[ti * N_EXPERTS + e]
            re = seg_start[(ti + 1) * N_EXPERTS + e]
            lo = jnp.maximum(rs, b0)
            n = jnp.minimum(re, b0 + bm_units) - lo

            @pl.when(n > 0)
            def _():
                src = ti * sr_units + run_start[ti * N_EXPERTS + e] + (lo - rs)
                pltpu.make_async_copy(_rows(s_hbm, src, n), _rows(xg.at[slot], lo - b0, n),
                                      sem.at[slot]).start()

            return carry

        lax.fori_loop(blk_tlo[b], blk_thi[b], body, 0)

    def write_back(g, slot):
        return pltpu.make_async_copy(yo.at[slot], y_hbm.at[pl.ds(pl.multiple_of(g * bm, bm), bm)],
                                     osem.at[slot])

    def emit(g, fill):
        slot = g % 2

        @pl.when(g >= 2)
        def _():
            write_back(g - 2, slot).wait()

        fill(slot)
        write_back(g, slot).start()

    @pl.when(e_step == 0)
    def _():
        xg[...] = jnp.zeros_like(xg)
        for g0 in range(n_ahead):

            @pl.when(g0 < nused)
            def _():
                gather(g0, g0)

    w1b[...] = w1_ref[0, 0].astype(BF16)
    w3b[...] = w3_ref[0, 0].astype(BF16)
    w2b[...] = w2_ref[0, 0].astype(BF16)

    def block(jb, carry):
        g = exp_blk0[e_step] + jb

        @pl.when(g + n_ahead < nused)
        def _():
            gather(g + n_ahead, (g + n_ahead) % n_slots)

        def fill(slot):
            gslot = g % n_slots
            pltpu.make_async_copy(_rows(s_hbm, 0, blk_n[g]), _rows(xg.at[gslot], 0, blk_n[g]),
                                  sem.at[gslot]).wait()
            x_lo, x_hi = _unpack_halves(xg[gslot])
            half = x_lo.shape[1]
            a = _dot(x_lo, w1b[:half, :]) + _dot(x_hi, w1b[half:, :])
            b = _dot(x_lo, w3b[:half, :]) + _dot(x_hi, w3b[half:, :])
            mid = (a / (1.0 + jnp.exp(-a))) * b
            y = _dot(mid.astype(BF16), w2b[...])
            yo[slot] = _pack_halves(y.astype(BF16).astype(F32))

        emit(g, fill)
        return carry

    lax.fori_loop(0, exp_nblk[e_step], block, 0)

    @pl.when(e_step == pl.num_programs(0) - 1)
    def _():
        def tail(g, carry):
            def fill(slot):
                yo[slot] = jnp.zeros(yo.shape[1:], yo.dtype)

            emit(g, fill)
            return carry

        lax.fori_loop(nused, nblk, tail, 0)
        write_back(nblk - 2, nblk % 2).wait()
        write_back(nblk - 1, (nblk - 1) % 2).wait()


def _ffn(tables, s_tiles, w1, w3, w2, *, layer, nblk, sr):
    bm = FFN_BLOCK
    d = w1.shape[2]
    de = w1.shape[3]
    nsp = len(tables)
    assert nblk >= 2
    grid_spec = pltpu.PrefetchScalarGridSpec(
        num_scalar_prefetch=nsp,
        grid=(N_EXPERTS,),
        in_specs=[
            pl.BlockSpec(memory_space=pl.ANY),
            pl.BlockSpec((1, 1, d, de), lambda e, *_: (layer, e, 0, 0)),
            pl.BlockSpec((1, 1, d, de), lambda e, *_: (layer, e, 0, 0)),
            pl.BlockSpec((1, 1, de, d), lambda e, *_: (layer, e, 0, 0)),
        ],
        out_specs=pl.BlockSpec(memory_space=pl.ANY),
        scratch_shapes=[pltpu.VMEM((3, bm, d // 2), jnp.uint32), pltpu.VMEM((2, bm, d // 2), jnp.uint32),
                        pltpu.VMEM((d, de), BF16), pltpu.VMEM((d, de), BF16), pltpu.VMEM((de, d), BF16),
                        pltpu.SemaphoreType.DMA((3,)), pltpu.SemaphoreType.DMA((2,))],
    )
    return pl.pallas_call(
        functools.partial(_ffn_kernel, sr_units=sr // RUN_ALIGN, nblk=nblk),
        grid_spec=grid_spec,
        out_shape=jax.ShapeDtypeStruct((nblk * bm, d // 2), jnp.uint32),
        compiler_params=_params(("arbitrary",)),
        name="ffn",
    )(*tables, s_tiles, w1, w3, w2)


def _combine_kernel(run_len, run_start, seg_start, exp_start, tile_rows,
                    y_hbm, x_ref, g2_ref, gp_ref, fgain_ref, o_ref, ybuf, sem, *, nsteps, final):
    i = pl.program_id(0)

    def gather(ti, slot):
        for e in range(N_EXPERTS):
            n = run_len[ti * N_EXPERTS + e]

            @pl.when(n > 0)
            def _():
                src = exp_start[e] + seg_start[ti * N_EXPERTS + e]
                pltpu.make_async_copy(_rows(y_hbm, src, n),
                                      _rows(ybuf.at[slot], run_start[ti * N_EXPERTS + e], n),
                                      sem.at[slot]).start()

    @pl.when(i == 0)
    def _():
        ybuf[...] = jnp.zeros_like(ybuf)
        gather(0, 0)

    @pl.when(i + 1 < nsteps)
    def _():
        gather(i + 1, (i + 1) % 2)

    slot = i % 2
    pltpu.make_async_copy(_rows(y_hbm, 0, tile_rows[i]), _rows(ybuf.at[slot], 0, tile_rows[i]),
                          sem.at[slot]).wait()
    gp = gp_ref[...]
    tm = gp.shape[0]
    sr = ybuf.shape[1]
    col = lax.broadcasted_iota(jnp.int32, (tm, sr), 1).astype(F32)
    weights = (jnp.where(col == gp[:, 2:3], gp[:, 0:1], 0.0)
               + jnp.where(col == gp[:, 3:4], gp[:, 1:2], 0.0)).astype(BF16)
    y_lo, y_hi = _unpack_halves(ybuf[slot])
    y = jnp.concatenate([_dot(weights, y_lo), _dot(weights, y_hi)], axis=-1)
    x2 = x_ref[...] + g2_ref[0] * y
    if final:
        ms = jnp.mean(x2 * x2, axis=-1, keepdims=True)
        x2 = (x2 * lax.rsqrt(ms + EPS)) * fgain_ref[...]
    o_ref[...] = x2


def _combine(tables, yb, x1, g2, gp, fgain, *, tokens_per_batch, final):
    t, d = x1.shape
    tm = min(MOE_TILE, tokens_per_batch)
    nsteps = t // tm
    per_b = tokens_per_batch // tm
    sr = _sorted_rows(tm)
    grid_spec = pltpu.PrefetchScalarGridSpec(
        num_scalar_prefetch=len(tables),
        grid=(nsteps,),
        in_specs=[
            pl.BlockSpec(memory_space=pl.ANY),
            pl.BlockSpec((tm, d), lambda i, *_: (i, 0)),
            pl.BlockSpec((1, 1, d), lambda i, *_: (i // per_b, 0, 0)),
            pl.BlockSpec((tm, LANES), lambda i, *_: (i, 0)),
            pl.BlockSpec((1, d), lambda i, *_: (0, 0)),
        ],
        out_specs=pl.BlockSpec((tm, d), lambda i, *_: (i, 0)),
        scratch_shapes=[pltpu.VMEM((2, sr, d // 2), jnp.uint32), pltpu.SemaphoreType.DMA((2,))],
    )
    return pl.pallas_call(
        functools.partial(_combine_kernel, nsteps=nsteps, final=final),
        grid_spec=grid_spec,
        out_shape=jax.ShapeDtypeStruct((t, d), F32),
        compiler_params=_params(("arbitrary",)),
        name="combine_final" if final else "combine",
    )(*tables, yb, x1, g2, gp, fgain)


def _reversal(size):
    row = lax.broadcasted_iota(jnp.int32, (size, size), 0)
    col = lax.broadcasted_iota(jnp.int32, (size, size), 1)
    return jnp.where(col == size - row, 1.0, 0.0).astype(BF16)


def _fmix_fold_kernel(x_ref, xm_ref, xb_ref, xc_ref, sh_ref, sc_ref, gain_ref, cc_ref, sc_mat_ref,
                      a_ref, b_ref, amid_ref):
    mod = lambda v: _modulate(v, gain_ref[...], sh_ref[0], sc_ref[0])
    h = mod(x_ref[...])
    ft = h.shape[0]
    m_hi, m_lo = _split_bf16(mod(xm_ref[...]))
    rev = _reversal(ft)
    row = lax.broadcasted_iota(jnp.int32, h.shape, 0)
    hr = jnp.where(row == 0, mod(xb_ref[...])[0:1], _dot(rev, m_hi) + _dot(rev, m_lo))
    he = (h + hr).astype(BF16)
    ho = (h - hr).astype(BF16)
    hc = mod(xc_ref[...]).astype(BF16)
    gw = cc_ref.shape[0]
    for g in range(FOURIER_GROUPS):
        cols = slice(g * gw, (g + 1) * gw)
        a_ref[:, cols] = _dot(he[:, cols], cc_ref[...]).astype(BF16)
        b_ref[:, cols] = _dot(ho[:, cols], sc_mat_ref[...]).astype(BF16)
        amid_ref[:, cols] = _dot(hc[:, cols], cc_ref[...])


def _fmix_fold(x, shift, scale, gain, cc, sc, *, bsz, n):
    t, d = x.shape
    hn = n // 2
    ft = min(512, hn)
    tpb = n // ft
    half = tpb // 2
    gw = d // FOURIER_GROUPS
    mod_idx = lambda b, j: (b, 0, 0)
    return pl.pallas_call(
        _fmix_fold_kernel,
        grid=(bsz, half),
        in_specs=[
            pl.BlockSpec((ft, d), lambda b, j: (b * tpb + j, 0)),
            pl.BlockSpec((ft, d), lambda b, j: (b * tpb + tpb - 1 - j, 0)),
            pl.BlockSpec((8, d), lambda b, j: (b * (n // 8) + (ft // 8) * ((tpb - j) % tpb), 0)),
            pl.BlockSpec((8, d), lambda b, j: (b * (n // 8) + hn // 8, 0)),
            pl.BlockSpec((1, 1, d), mod_idx), pl.BlockSpec((1, 1, d), mod_idx),
            pl.BlockSpec((1, d), lambda b, j: (0, 0)),
            pl.BlockSpec((gw, gw), lambda b, j: (0, 0)), pl.BlockSpec((gw, gw), lambda b, j: (0, 0)),
        ],
        out_specs=[pl.BlockSpec((ft, d), lambda b, j: (b * half + j, 0)),
                   pl.BlockSpec((ft, d), lambda b, j: (b * half + j, 0)),
                   pl.BlockSpec((8, d), lambda b, j: (b, 0))],
        out_shape=[jax.ShapeDtypeStruct((bsz * hn, d), BF16), jax.ShapeDtypeStruct((bsz * hn, d), BF16),
                   jax.ShapeDtypeStruct((bsz * 8, d), F32)],
        compiler_params=_params(("arbitrary", "arbitrary")),
        name="fmix_fold",
    )(x, x, x, x, shift, scale, gain, cc, sc)


def _fmix_pos_kernel(ch_ref, sh_ref, sign_ref, a_ref, b_ref, amid_ref, dif_ref, sum_ref, p_acc, q_acc):
    k = pl.program_id(2)

    @pl.when(k == 0)
    def _():
        p_acc[...] = jnp.zeros_like(p_acc)
        q_acc[...] = jnp.zeros_like(q_acc)

    p_acc[...] += _dot(ch_ref[...], a_ref[0])
    q_acc[...] += _dot(sh_ref[...], b_ref[0])

    @pl.when(k == pl.num_programs(2) - 1)
    def _():
        p = p_acc[...] + sign_ref[...] * amid_ref[0, 0:1, :]
        dif_ref[0] = (p - q_acc[...]).astype(BF16)
        sum_ref[0] = (p + q_acc[...]).astype(BF16)


def _fmix_pos(ch, sh, sign, a, b, amid):
    bsz, hn, d = a.shape
    mp = ch.shape[0]
    tmr = mp // 2
    tk = min(512, hn)
    out = jax.ShapeDtypeStruct((bsz, mp, d), BF16)
    return pl.pallas_call(
        _fmix_pos_kernel,
        grid=(mp // tmr, bsz, hn // tk),
        in_specs=[
            pl.BlockSpec((tmr, tk), lambda i, bi, k: (i, k)),
            pl.BlockSpec((tmr, tk), lambda i, bi, k: (i, k)),
            pl.BlockSpec((tmr, 1), lambda i, bi, k: (i, 0)),
            pl.BlockSpec((1, tk, d), lambda i, bi, k: (bi, k, 0)),
            pl.BlockSpec((1, tk, d), lambda i, bi, k: (bi, k, 0)),
            pl.BlockSpec((1, 8, d), lambda i, bi, k: (bi, 0, 0)),
        ],
        out_specs=[pl.BlockSpec((1, tmr, d), lambda i, bi, k: (bi, i, 0)),
                   pl.BlockSpec((1, tmr, d), lambda i, bi, k: (bi, i, 0))],
        out_shape=[out, out],
        scratch_shapes=[pltpu.VMEM((tmr, d), F32), pltpu.VMEM((tmr, d), F32)],
        compiler_params=_params(("arbitrary", "arbitrary", "arbitrary")),
        name="fmix_pos",
    )(ch, sh, sign, a, b, amid)


def _rope_tables(n):
    half = ROPE_AXIS_DIM // 2
    rows = n // GRID_W
    inv = ROPE_THETA ** (-jnp.arange(half, dtype=F32) / half)
    pos = jnp.arange(n, dtype=jnp.int32)
    ang_r = (pos // GRID_W).astype(F32)[:, None] * inv
    ang_c = (pos % GRID_W).astype(F32)[:, None] * inv
    del rows
    cos = jnp.concatenate([jnp.cos(ang_r)] * 2 + [jnp.cos(ang_c)] * 2, axis=-1)
    sin = jnp.concatenate([-jnp.sin(ang_r), jnp.sin(ang_r), -jnp.sin(ang_c), jnp.sin(ang_c)], axis=-1)
    return jnp.tile(cos, (1, LANES // HEAD_DIM)), jnp.tile(sin, (1, LANES // HEAD_DIM))


def _dft_mats(n, scale):
    r = 1
    while r * r < n:
        r *= 2
    c = n // r
    k = jnp.arange(n, dtype=jnp.int32)[:, None]
    ang_a = ((k * jnp.arange(r, dtype=jnp.int32)[None, :] * c) % n).astype(F32) * (2.0 * math.pi / n)
    ang_b = ((k * jnp.arange(c, dtype=jnp.int32)[None, :]) % n).astype(F32) * (2.0 * math.pi / n)
    ca, sa, cb, sb = jnp.cos(ang_a), jnp.sin(ang_a), jnp.cos(ang_b), jnp.sin(ang_b)
    cosm = (ca[:, :, None] * cb[:, None, :] - sa[:, :, None] * sb[:, None, :]).reshape(n, n)
    sinm = (sa[:, :, None] * cb[:, None, :] + ca[:, :, None] * sb[:, None, :]).reshape(n, n)
    return (cosm * scale).astype(BF16), (sinm * scale).astype(BF16)


DFT_ROW_PAD = 32


def _dft_half(n):
    hn = n // 2
    mp = hn + DFT_ROW_PAD
    r = 1
    while r * r < hn:
        r *= 2
    c = hn // r
    k = jnp.arange(mp, dtype=jnp.int32)[:, None]
    valid = k <= hn
    w = 2.0 * math.pi / n
    ang_a = ((k * (jnp.arange(r, dtype=jnp.int32)[None, :] * c)) % n).astype(F32) * w
    ang_b = ((k * jnp.arange(c, dtype=jnp.int32)[None, :]) % n).astype(F32) * w
    ca, sa, cb, sb = jnp.cos(ang_a), jnp.sin(ang_a), jnp.cos(ang_b), jnp.sin(ang_b)
    cosm = (ca[:, :, None] * cb[:, None, :] - sa[:, :, None] * sb[:, None, :]).reshape(mp, hn)
    sinm = (sa[:, :, None] * cb[:, None, :] + ca[:, :, None] * sb[:, None, :]).reshape(mp, hn)
    scale = n ** -0.5
    col_w = jnp.where(jnp.arange(hn)[None, :] == 0, 0.5 * scale, scale)
    cosm = jnp.where(valid, cosm * col_w, 0.0).astype(BF16)
    sinm = jnp.where(valid, sinm * scale, 0.0).astype(BF16)
    sign = jnp.where(valid, jnp.where(k % 2 == 0, scale, -scale), 0.0).astype(F32)
    return cosm, sinm, sign


def _moe(a, w_o, x, g1, g2, gain, shift, scale, w_rg, b_rg, w_re, b_re, w1, w3, w2, fgain, *,
         layer, tokens_per_batch, final):
    t, d = x.shape
    pad = LANES - N_GROUPS - N_EXPERTS
    w_r = jnp.concatenate([w_rg, w_re, jnp.zeros((d, pad), F32)], axis=1)
    b_r = jnp.concatenate([b_rg, b_re, jnp.zeros((pad,), F32)])[None, :]
    wr_hi = w_r.astype(BF16)
    wr_lo = jnp.concatenate([wr_hi, (w_r - wr_hi.astype(F32)).astype(BF16)], axis=1)
    x1, s_tiles, gp, pc = _proj_moe(a, w_o, x, g1, gain, shift, scale, wr_hi, wr_lo, b_r,
                                    tokens_per_batch=tokens_per_batch)
    tm = min(MOE_TILE, tokens_per_batch)
    nt = t // tm
    sr = _sorted_rows(tm)
    bmu = FFN_BLOCK // RUN_ALIGN
    i32 = jnp.int32
    run_len = pc.reshape(nt, 8, LANES)[:, 0, :N_EXPERTS]
    seg_start = jnp.concatenate([jnp.zeros((1, N_EXPERTS), i32), jnp.cumsum(run_len, axis=0)])
    seg_len = seg_start[-1]
    seg_pad = (seg_len + bmu - 1) // bmu * bmu
    exp_end = jnp.cumsum(seg_pad)
    exp_start = exp_end - seg_pad
    run_start = jnp.cumsum(run_len, axis=1) - run_len
    tile_rows = run_start[:, -1] + run_len[:, -1]
    max_units = (2 * t) // RUN_ALIGN + nt * N_EXPERTS + N_EXPERTS * (bmu - 1)
    nblk = (max_units + bmu - 1) // bmu
    blk_start = jnp.arange(nblk, dtype=i32) * bmu
    blk_e = jnp.minimum(jnp.sum((exp_end[None, :] <= blk_start[:, None]).astype(i32), axis=1),
                        N_EXPERTS - 1)
    blk_b0 = blk_start - exp_start[blk_e]
    blk_n = jnp.clip(seg_len[blk_e] - blk_b0, 0, bmu)
    seg_e = seg_start.T[blk_e]
    blk_tlo = jnp.sum((seg_e[:, 1:] <= blk_b0[:, None]).astype(i32), axis=1)
    blk_thi = jnp.sum((seg_e[:, :-1] < (blk_b0 + bmu)[:, None]).astype(i32), axis=1)
    nused = (exp_end[-1] // bmu)[None]
    flat = lambda v: v.reshape(-1).astype(i32)
    yb = _ffn((seg_pad // bmu, exp_start // bmu, blk_e, blk_b0, blk_n, blk_tlo, blk_thi,
               flat(seg_start), flat(run_start), nused),
              s_tiles, w1, w3, w2, layer=layer, nblk=nblk, sr=sr)
    return _combine((flat(run_len), flat(run_start), flat(seg_start), exp_start, tile_rows),
                    yb, x1, g2, gp, fgain, tokens_per_batch=tokens_per_batch, final=final)


def kernel(x, c, ctx, c_ctx, w_mod, b_mod, norm_mix, norm_ffn, attn_w_qkv, attn_q_norm, attn_k_norm,
           attn_w_o, fourier_w_o, moe_w_rg, moe_b_rg, moe_w_re, moe_b_re, moe_w1, moe_w3, moe_w2,
           final_norm):
    bsz, n, d = x.shape
    t = bsz * n
    q_dim = N_HEADS * HEAD_DIM
    kv_dim = N_KV_HEADS * HEAD_DIM

    cond = jnp.zeros((16, d), F32).at[:bsz].set(c).at[bsz].set(c_ctx)
    mods = _ada(cond, w_mod, b_mod)

    def lat(layer, j):
        return mods[layer, :bsz, j * d:(j + 1) * d][:, None, :]

    def ctxm(layer, j):
        return mods[layer, bsz:bsz + 1, j * d:(j + 1) * d][:, None, :]

    w_qkv = attn_w_qkv[0].astype(BF16)
    qscale = (HEAD_DIM ** -0.5) * math.log2(math.e)
    head_gain = jnp.stack([jnp.tile(attn_q_norm[0] * qscale, LANES // HEAD_DIM),
                           jnp.tile(attn_k_norm[0], LANES // HEAD_DIM)])
    lane = jnp.arange(LANES)
    ebd = jnp.where((lane[:, None] // HEAD_DIM) == (lane[None, :] // HEAD_DIM),
                    1.0 / HEAD_DIM, 0.0).astype(BF16)
    cos, sin = _rope_tables(n)
    gain0 = norm_mix[0][None, :]
    q, k_l, v_l = _qkv(x, lat(0, 0), lat(0, 1), gain0, w_qkv, head_gain, ebd, cos, sin,
                       with_q=True, per_batch_mods=True)
    n_ctx = ctx.shape[1]
    k_c, v_c = _qkv(ctx, ctxm(0, 0), ctxm(0, 1), gain0, w_qkv[:, q_dim:], head_gain, ebd,
                    jnp.ones((n_ctx, LANES), F32), jnp.zeros((n_ctx, LANES), F32),
                    with_q=False, per_batch_mods=False)
    del kv_dim
    o = _attention(q, k_l, v_l, k_c, v_c).reshape(t, q_dim)

    xf = x.reshape(t, d)
    moe_w = lambda i: (moe_w_rg[i], moe_b_rg[i], moe_w_re[i], moe_b_re[i],
                       moe_w1, moe_w3, moe_w2)
    fgain = final_norm[None, :]
    x2 = _moe(o, attn_w_o[0].astype(BF16), xf, lat(0, 2), lat(0, 5), norm_ffn[0][None, :],
              lat(0, 3), lat(0, 4), *moe_w(0), fgain, layer=0, tokens_per_batch=n, final=False)

    gw = d // FOURIER_GROUPS
    cc, sc = _dft_mats(gw, gw ** -0.5)
    ch, sh, sign = _dft_half(n)
    a_e, b_o, a_mid = _fmix_fold(x2, lat(1, 0), lat(1, 1), norm_mix[1][None, :], cc, sc, bsz=bsz, n=n)
    f = tuple(_fmix_pos(ch, sh, sign, a_e.reshape(bsz, n // 2, d), b_o.reshape(bsz, n // 2, d),
                        a_mid.reshape(bsz, 8, d)))
    out = _moe(f, fourier_w_o[0].astype(BF16), x2, lat(1, 2), lat(1, 5), norm_ffn[1][None, :],
               lat(1, 3), lat(1, 4), *moe_w(1), fgain, layer=1, tokens_per_batch=n, final=True)
    return out.reshape(bsz, n, d)
```

```python
import functools
import math

import jax
import jax.numpy as jnp
from jax import lax
from jax.experimental import pallas as pl
from jax.experimental.pallas import tpu as pltpu

F32 = jnp.float32
BF16 = jnp.bfloat16

N_HEADS = 16
N_KV_HEADS = 4
HEAD_DIM = 64
GQA_GROUP = N_HEADS // N_KV_HEADS
GRID_W = 64
ROPE_AXIS_DIM = HEAD_DIM // 2
ROPE_THETA = 10000.0
FOURIER_GROUPS = 4
N_GROUPS = 4
EXPERTS_PER_GROUP = 8
N_EXPERTS = N_GROUPS * EXPERTS_PER_GROUP
EPS = 1e-6
LANES = 128
NEG_BIG = -1e30
VMEM_LIMIT = 48 * 1024 * 1024

FFN_BLOCK = 512
MOE_TILE = 512
RUN_ALIGN = 8


def _sorted_rows(tile_tokens):
    return 2 * tile_tokens + N_EXPERTS * RUN_ALIGN


def _dot(a, b):
    return jnp.dot(a, b, preferred_element_type=F32)


def _split_bf16(a):
    hi = a.astype(BF16)
    lo = (a - hi.astype(F32)).astype(BF16)
    return hi, lo


_HI16 = 0xFFFF0000


def _pack_halves(v):
    w = v.shape[1] // 2
    lo = lax.bitcast_convert_type(v[:, :w], jnp.uint32) >> 16
    hi = lax.bitcast_convert_type(v[:, w:], jnp.uint32) & jnp.uint32(_HI16)
    return hi | lo


def _unpack_halves(p):
    lo = lax.bitcast_convert_type(p << 16, F32).astype(BF16)
    hi = lax.bitcast_convert_type(p & jnp.uint32(_HI16), F32).astype(BF16)
    return lo, hi


def _params(sem):
    return pltpu.CompilerParams(dimension_semantics=sem, vmem_limit_bytes=VMEM_LIMIT)


def _ada_kernel(c_ref, w_ref, b_ref, o_ref):
    c = c_ref[...]
    s = c / (1.0 + jnp.exp(-c))
    s_hi, s_lo = _split_bf16(s)
    w_hi, w_lo = _split_bf16(w_ref[0])
    o_ref[0] = _dot(s_hi, w_hi) + _dot(s_hi, w_lo) + _dot(s_lo, w_hi) + b_ref[0]


def _ada(cond, w_mod, b_mod):
    depth, d, d6 = w_mod.shape
    rows = cond.shape[0]
    tn = 1536
    return pl.pallas_call(
        _ada_kernel,
        grid=(depth, d6 // tn),
        in_specs=[
            pl.BlockSpec((rows, d), lambda l, j: (0, 0)),
            pl.BlockSpec((1, d, tn), lambda l, j: (l, 0, j)),
            pl.BlockSpec((1, 1, tn), lambda l, j: (l, 0, j)),
        ],
        out_specs=pl.BlockSpec((1, rows, tn), lambda l, j: (l, 0, j)),
        out_shape=jax.ShapeDtypeStruct((depth, rows, d6), F32),
        compiler_params=_params(("arbitrary", "arbitrary")),
        name="ada",
    )(cond, w_mod, b_mod.reshape(depth, 1, d6))


def _modulate(xf, gain, shift, scale):
    ms = jnp.mean(xf * xf, axis=-1, keepdims=True)
    return (xf * lax.rsqrt(ms + EPS)) * (gain * (1.0 + scale)) + shift


def _qkv_kernel(x_ref, sh_ref, sc_ref, gain_ref, w_ref, hg_ref, ebd_ref, cos_ref, sin_ref,
                *out_refs, with_q):
    h = _modulate(x_ref[0], gain_ref[...], sh_ref[0], sc_ref[0])
    qkv = _dot(h.astype(BF16), w_ref[...])
    tm = qkv.shape[0]
    lane = lax.broadcasted_iota(jnp.int32, (tm, LANES), 1)
    first_half = (lane & 16) == 0
    low = lane < HEAD_DIM
    if with_q:
        q_ref, k_ref, v_ref = out_refs
        nq = N_HEADS * HEAD_DIM // LANES
    else:
        k_ref, v_ref = out_refs
        nq = 0
    nk = N_KV_HEADS * HEAD_DIM // LANES
    cos = cos_ref[...]
    sin = sin_ref[...]

    def norm_rope(c, gain_row):
        ms = _dot((c * c).astype(BF16), ebd_ref[...])
        cn = c * lax.rsqrt(ms + EPS) * gain_row
        partner = jnp.where(first_half, pltpu.roll(cn, LANES - 16, 1), pltpu.roll(cn, 16, 1))
        return cn * cos + partner * sin

    for j in range(nq):
        out = norm_rope(qkv[:, j * LANES:(j + 1) * LANES], hg_ref[0:1, :]).astype(BF16)
        q_ref[0, 2 * j] = out[:, :HEAD_DIM]
        q_ref[0, 2 * j + 1] = out[:, HEAD_DIM:]
    for j in range(nk):
        out = norm_rope(qkv[:, (nq + j) * LANES:(nq + j + 1) * LANES], hg_ref[1:2, :]).astype(BF16)
        k_ref[0, 2 * j] = out[:, :HEAD_DIM]
        k_ref[0, 2 * j + 1] = out[:, HEAD_DIM:]
    for j in range(nk):
        c = qkv[:, (nq + nk + j) * LANES:(nq + nk + j + 1) * LANES]
        v_ref[0, 2 * j] = jnp.where(low, c, 1.0).astype(BF16)
        v_ref[0, 2 * j + 1] = jnp.where(low, pltpu.roll(c, HEAD_DIM, 1), 1.0).astype(BF16)


def _qkv(x, shift, scale, gain, w, head_gain, ebd, cos, sin, *, with_q, per_batch_mods):
    b, n, d = x.shape
    tm = min(512, n)
    ncol = w.shape[1]
    mod_idx = (lambda bi, i: (bi, 0, 0)) if per_batch_mods else (lambda bi, i: (0, 0, 0))
    out_shape = []
    out_specs = []
    if with_q:
        out_shape.append(jax.ShapeDtypeStruct((b, N_HEADS, n, HEAD_DIM), BF16))
        out_specs.append(pl.BlockSpec((1, N_HEADS, tm, HEAD_DIM), lambda bi, i: (bi, 0, i, 0)))
    out_shape.append(jax.ShapeDtypeStruct((b, N_KV_HEADS, n, HEAD_DIM), BF16))
    out_specs.append(pl.BlockSpec((1, N_KV_HEADS, tm, HEAD_DIM), lambda bi, i: (bi, 0, i, 0)))
    out_shape.append(jax.ShapeDtypeStruct((b, N_KV_HEADS, n, LANES), BF16))
    out_specs.append(pl.BlockSpec((1, N_KV_HEADS, tm, LANES), lambda bi, i: (bi, 0, i, 0)))
    return pl.pallas_call(
        functools.partial(_qkv_kernel, with_q=with_q),
        grid=(b, n // tm),
        in_specs=[
            pl.BlockSpec((1, tm, d), lambda bi, i: (bi, i, 0)),
            pl.BlockSpec((1, 1, d), mod_idx),
            pl.BlockSpec((1, 1, d), mod_idx),
            pl.BlockSpec((1, d), lambda bi, i: (0, 0)),
            pl.BlockSpec((d, ncol), lambda bi, i: (0, 0)),
            pl.BlockSpec((2, LANES), lambda bi, i: (0, 0)),
            pl.BlockSpec((LANES, LANES), lambda bi, i: (0, 0)),
            pl.BlockSpec((tm, LANES), lambda bi, i: (i, 0)),
            pl.BlockSpec((tm, LANES), lambda bi, i: (i, 0)),
        ],
        out_specs=out_specs,
        out_shape=out_shape,
        compiler_params=_params(("arbitrary", "arbitrary")),
        name="qkv" if with_q else "ctx_kv",
    )(x, shift, scale, gain, w, head_gain, ebd, cos, sin)


def _attn_kernel(q_ref, k_ref, v_ref, kc_ref, vc_ref, o_ref, *, tq, tk, n_sub):
    chunks = []
    for kr, vr in ((k_ref, v_ref), (kc_ref, vc_ref)):
        size = min(tk, kr.shape[2])
        chunks += [(kr, vr, c, size) for c in range(kr.shape[2] // size)]
    sub = tq // n_sub
    m_rows = GQA_GROUP * sub
    for u in range(n_sub):
        q = q_ref[0, :, u * sub:(u + 1) * sub, :].reshape(m_rows, HEAD_DIM)
        m = jnp.full((m_rows, 1), -jnp.inf, F32)
        acc = jnp.zeros((m_rows, LANES), F32)
        for kr, vr, c, size in chunks:
            kc = kr[0, 0, c * size:(c + 1) * size, :]
            vc = vr[0, 0, c * size:(c + 1) * size, :]
            s = lax.dot_general(q, kc, (((1,), (1,)), ((), ())), preferred_element_type=F32)
            m_new = jnp.maximum(m, jnp.max(s, axis=-1, keepdims=True))
            alpha = jnp.exp2(m - m_new)
            p = jnp.exp2(s - m_new)
            acc = alpha * acc + _dot(p.astype(BF16), vc)
            m = m_new
        o = acc * (1.0 / pltpu.roll(acc, HEAD_DIM, 1))
        o_ref[0, u * sub:(u + 1) * sub, :] = jnp.concatenate(
            [o[g * sub:(g + 1) * sub, :HEAD_DIM] for g in range(GQA_GROUP)], axis=-1).astype(BF16)


def _attention(q, k, v, k_ctx, v_ctx):
    b, _, n, _ = q.shape
    n_ctx = k_ctx.shape[2]
    tq = min(512, n)
    tk = 256
    assert n % min(tk, n) == 0 and n_ctx % min(tk, n_ctx) == 0
    kv_spec = lambda rows, width: pl.BlockSpec((1, 1, rows, width), lambda bi, kh, qi: (bi, kh, 0, 0))
    return pl.pallas_call(
        functools.partial(_attn_kernel, tq=tq, tk=tk, n_sub=tq // min(256, tq)),
        grid=(b, N_KV_HEADS, n // tq),
        in_specs=[
            pl.BlockSpec((1, GQA_GROUP, tq, HEAD_DIM), lambda bi, kh, qi: (bi, kh, qi, 0)),
            kv_spec(n, HEAD_DIM), kv_spec(n, LANES), kv_spec(n_ctx, HEAD_DIM), kv_spec(n_ctx, LANES),
        ],
        out_specs=pl.BlockSpec((1, tq, GQA_GROUP * HEAD_DIM), lambda bi, kh, qi: (bi, qi, kh)),
        out_shape=jax.ShapeDtypeStruct((b, n, N_HEADS * HEAD_DIM), BF16),
        compiler_params=_params(("arbitrary", "arbitrary", "arbitrary")),
        name="attn",
    )(q, k, v, k_ctx, v_ctx)


def _proj_moe_kernel(*refs, tiles_per_batch, folded):
    if folded:
        dif_ref, sum_ref, sumb_ref, *refs, a_scr = refs
        j = pl.program_id(0) % tiles_per_batch

        @pl.when(j < tiles_per_batch // 2)
        def _():
            a_scr[...] = dif_ref[0]

        @pl.when(j >= tiles_per_batch // 2)
        def _():
            rows = _dot(_reversal(a_scr.shape[0]), sum_ref[0])
            row = lax.broadcasted_iota(jnp.int32, rows.shape, 0)
            a_scr[...] = jnp.where(row == 0, sumb_ref[0, 0:1, :].astype(F32), rows).astype(BF16)

        a = a_scr[...]
    else:
        a_ref, *refs = refs
        a = a_ref[...]
    (wo_ref, x_ref, g1_ref, gain_ref, sh_ref, sc_ref, wrh_ref, wrl_ref, br_ref,
     x1_ref, s_ref, gp_ref, pc_ref) = refs
    x1 = x_ref[...] + g1_ref[0] * _dot(a, wo_ref[...])
    x1_ref[...] = x1
    h2 = _modulate(x1, gain_ref[...], sh_ref[0], sc_ref[0])
    h_hi, h_lo = _split_bf16(h2)
    r_hi = _dot(h_hi, wrl_ref[...])
    logits = (r_hi[:, :LANES] + _dot(h_lo, wrh_ref[...]) + r_hi[:, LANES:]
              + br_ref[...])
    tm = logits.shape[0]
    lane = lax.broadcasted_iota(jnp.int32, (tm, LANES), 1)
    lane_f = lane.astype(F32)

    def top(vals):
        best = jnp.max(vals, axis=-1, keepdims=True)
        idx = jnp.min(jnp.where(vals == best, lane_f, float(LANES)), axis=-1, keepdims=True)
        return best, idx

    lg = jnp.where(lane < N_GROUPS, logits, NEG_BIG)
    g_max, g_sel = top(lg)
    p_grp = 1.0 / jnp.sum(jnp.where(lane < N_GROUPS, jnp.exp(lg - g_max), 0.0), axis=-1, keepdims=True)
    lo = N_GROUPS + EXPERTS_PER_GROUP * g_sel
    le = jnp.where((lane_f >= lo) & (lane_f < lo + EXPERTS_PER_GROUP), logits, NEG_BIG)
    t1, i1 = top(le)
    t2, i2 = top(jnp.where(lane_f == i1, NEG_BIG, le))
    e21 = jnp.exp(t2 - t1)
    w1 = 1.0 / (1.0 + e21)
    gate0 = p_grp * w1
    gate1 = p_grp * (e21 * w1)
    e0 = i1 - N_GROUPS
    e1 = i2 - N_GROUPS

    onehot = jnp.where((lane_f == e0) | (lane_f == e1), 1.0, 0.0)
    row = lax.broadcasted_iota(jnp.int32, (tm, tm), 0)
    col = lax.broadcasted_iota(jnp.int32, (tm, tm), 1)
    lower = jnp.where(col < row, 1.0, 0.0).astype(BF16)
    before = _dot(lower, onehot.astype(BF16))
    cnt = jnp.sum(onehot, axis=0, keepdims=True)
    pc = jnp.floor((cnt + (RUN_ALIGN - 1)) * (1.0 / RUN_ALIGN))
    urow = lax.broadcasted_iota(jnp.int32, (LANES, LANES), 0)
    ucol = lax.broadcasted_iota(jnp.int32, (LANES, LANES), 1)
    upper = jnp.where(urow < ucol, 1.0, 0.0).astype(BF16)
    run_start = _dot(jnp.broadcast_to(pc, (8, LANES)).astype(BF16), upper)[0:1] * RUN_ALIGN
    base = run_start + before
    pos0 = jnp.sum(jnp.where(lane_f == e0, base, 0.0), axis=-1, keepdims=True)
    pos1 = jnp.sum(jnp.where(lane_f == e1, base, 0.0), axis=-1, keepdims=True)
    gp = jnp.where(lane == 0, gate0, jnp.where(lane == 1, gate1, jnp.where(lane == 2, pos0,
                                                                       jnp.where(lane == 3, pos1, 0.0))))
    gp_ref[...] = gp
    pc_ref[...] = jnp.broadcast_to(pc, (8, LANES)).astype(jnp.int32)

    pos_t = jnp.where(lane == 0, pos0, jnp.where(lane == 1, pos1, -1.0)).T
    sr = s_ref.shape[0]
    srow = lax.broadcasted_iota(jnp.int32, (sr, tm), 0).astype(F32)
    perm = jnp.where((srow == pos_t[0:1, :]) | (srow == pos_t[1:2, :]), 1.0, 0.0).astype(BF16)
    s_ref[...] = _pack_halves(_dot(perm, h_hi))


def _proj_moe(a, w_o, x, g1, gain, shift, scale, wr_hi, wr_lo, b_r, *, tokens_per_batch):
    t, d = x.shape
    tm = min(MOE_TILE, tokens_per_batch)
    nt = t // tm
    sr = _sorted_rows(tm)
    per_b = tokens_per_batch // tm
    mod_idx = lambda i: (i // per_b, 0, 0)
    row_spec = lambda w: pl.BlockSpec((tm, w), lambda i: (i, 0))
    full = lambda s: pl.BlockSpec(s, lambda i: (0,) * len(s))
    folded = isinstance(a, tuple)
    if folded:
        assert per_b % 2 == 0
        half = per_b // 2
        a_inputs = (a[0], a[1], a[1])
        a_specs = [
            pl.BlockSpec((1, tm, d), lambda i: (i // per_b, jnp.minimum(i % per_b, half - 1), 0)),
            pl.BlockSpec((1, tm, d), lambda i: (i // per_b, jnp.clip(per_b - 1 - i % per_b, 0, half - 1), 0)),
            pl.BlockSpec((1, 8, d), lambda i: (i // per_b, (tm // 8) * jnp.clip(per_b - i % per_b, 1, half), 0)),
        ]
        scratch = [pltpu.VMEM((tm, d), BF16)]
    else:
        a_inputs = (a,)
        a_specs = [row_spec(d)]
        scratch = []
    return pl.pallas_call(
        functools.partial(_proj_moe_kernel, tiles_per_batch=per_b, folded=folded),
        grid=(nt,),
        in_specs=a_specs + [
            full((d, d)), row_spec(d),
            pl.BlockSpec((1, 1, d), mod_idx), full((1, d)),
            pl.BlockSpec((1, 1, d), mod_idx), pl.BlockSpec((1, 1, d), mod_idx),
            full((d, LANES)), full((d, 2 * LANES)), full((1, LANES)),
        ],
        out_specs=[row_spec(d), pl.BlockSpec((sr, d // 2), lambda i: (i, 0)), row_spec(LANES),
                   pl.BlockSpec((8, LANES), lambda i: (i, 0))],
        out_shape=[
            jax.ShapeDtypeStruct((t, d), F32), jax.ShapeDtypeStruct((nt * sr, d // 2), jnp.uint32),
            jax.ShapeDtypeStruct((t, LANES), F32), jax.ShapeDtypeStruct((nt * 8, LANES), jnp.int32),
        ],
        scratch_shapes=scratch,
        compiler_params=_params(("arbitrary",)),
        name="proj_moe",
    )(*a_inputs, w_o, x, g1, gain, shift, scale, wr_hi, wr_lo, b_r)


def _rows(ref, start_units, n_units):
    return ref.at[pl.ds(pl.multiple_of(start_units * RUN_ALIGN, RUN_ALIGN), n_units * RUN_ALIGN)]


def _ffn_kernel(exp_nblk, exp_blk0, blk_e, blk_b0, blk_n, blk_tlo, blk_thi, seg_start, run_start,
                nused_ref, s_hbm, w1_ref, w3_ref, w2_ref, y_hbm, xg, yo, w1b, w3b, w2b, sem, osem,
                *, sr_units, nblk):
    e_step = pl.program_id(0)
    nused = nused_ref[0]
    n_slots, bm, _ = xg.shape
    n_ahead = n_slots - 1
    bm_units = bm // RUN_ALIGN

    def gather(b, slot):
        e = blk_e[b]
        b0 = blk_b0[b]

        def body(ti, carry):
            rs = seg_start[ti * N_EXPERTS + e]
            re = seg_start[(ti + 1) * N_EXPERTS + e]
            lo = jnp.maximum(rs, b0)
            n = jnp.minimum(re, b0 + bm_units) - lo

            @pl.when(n > 0)
            def _():
                src = ti * sr_units + run_start[ti * N_EXPERTS + e] + (lo - rs)
                pltpu.make_async_copy(_rows(s_hbm, src, n), _rows(xg.at[slot], lo - b0, n),
                                      sem.at[slot]).start()

            return carry

        lax.fori_loop(blk_tlo[b], blk_thi[b], body, 0)

    def write_back(g, slot):
        return pltpu.make_async_copy(yo.at[slot], y_hbm.at[pl.ds(pl.multiple_of(g * bm, bm), bm)],
                                     osem.at[slot])

    def emit(g, fill):
        slot = g % 2

        @pl.when(g >= 2)
        def _():
            write_back(g - 2, slot).wait()

        fill(slot)
        write_back(g, slot).start()

    @pl.when(e_step == 0)
    def _():
        xg[...] = jnp.zeros_like(xg)
        for g0 in range(n_ahead):

            @pl.when(g0 < nused)
            def _():
                gather(g0, g0)

    w1b[...] = w1_ref[0, 0].astype(BF16)
    w3b[...] = w3_ref[0, 0].astype(BF16)
    w2b[...] = w2_ref[0, 0].astype(BF16)

    def block(jb, carry):
        g = exp_blk0[e_step] + jb

        @pl.when(g + n_ahead < nused)
        def _():
            gather(g + n_ahead, (g + n_ahead) % n_slots)

        def fill(slot):
            gslot = g % n_slots
            pltpu.make_async_copy(_rows(s_hbm, 0, blk_n[g]), _rows(xg.at[gslot], 0, blk_n[g]),
                                  sem.at[gslot]).wait()
            x_lo, x_hi = _unpack_halves(xg[gslot])
            half = x_lo.shape[1]
            a = _dot(x_lo, w1b[:half, :]) + _dot(x_hi, w1b[half:, :])
            b = _dot(x_lo, w3b[:half, :]) + _dot(x_hi, w3b[half:, :])
            mid = (a / (1.0 + jnp.exp(-a))) * b
            y = _dot(mid.astype(BF16), w2b[...])
            yo[slot] = _pack_halves(y.astype(BF16).astype(F32))

        emit(g, fill)
        return carry

    lax.fori_loop(0, exp_nblk[e_step], block, 0)

    @pl.when(e_step == pl.num_programs(0) - 1)
    def _():
        def tail(g, carry):
            def fill(slot):
                yo[slot] = jnp.zeros(yo.shape[1:], yo.dtype)

            emit(g, fill)
            return carry

        lax.fori_loop(nused, nblk, tail, 0)
        write_back(nblk - 2, nblk % 2).wait()
        write_back(nblk - 1, (nblk - 1) % 2).wait()


def _ffn(tables, s_tiles, w1, w3, w2, *, layer, nblk, sr):
    bm = FFN_BLOCK
    d = w1.shape[2]
    de = w1.shape[3]
    nsp = len(tables)
    assert nblk >= 2
    grid_spec = pltpu.PrefetchScalarGridSpec(
        num_scalar_prefetch=nsp,
        grid=(N_EXPERTS,),
        in_specs=[
            pl.BlockSpec(memory_space=pl.ANY),
            pl.BlockSpec((1, 1, d, de), lambda e, *_: (layer, e, 0, 0)),
            pl.BlockSpec((1, 1, d, de), lambda e, *_: (layer, e, 0, 0)),
            pl.BlockSpec((1, 1, de, d), lambda e, *_: (layer, e, 0, 0)),
        ],
        out_specs=pl.BlockSpec(memory_space=pl.ANY),
        scratch_shapes=[pltpu.VMEM((3, bm, d // 2), jnp.uint32), pltpu.VMEM((2, bm, d // 2), jnp.uint32),
                        pltpu.VMEM((d, de), BF16), pltpu.VMEM((d, de), BF16), pltpu.VMEM((de, d), BF16),
                        pltpu.SemaphoreType.DMA((3,)), pltpu.SemaphoreType.DMA((2,))],
    )
    return pl.pallas_call(
        functools.partial(_ffn_kernel, sr_units=sr // RUN_ALIGN, nblk=nblk),
        grid_spec=grid_spec,
        out_shape=jax.ShapeDtypeStruct((nblk * bm, d // 2), jnp.uint32),
        compiler_params=_params(("arbitrary",)),
        name="ffn",
    )(*tables, s_tiles, w1, w3, w2)


def _combine_kernel(run_len, run_start, seg_start, exp_start, tile_rows,
                    y_hbm, x_ref, g2_ref, gp_ref, fgain_ref, o_ref, ybuf, sem, *, nsteps, final):
    i = pl.program_id(0)

    def gather(ti, slot):
        for e in range(N_EXPERTS):
            n = run_len[ti * N_EXPERTS + e]

            @pl.when(n > 0)
            def _():
                src = exp_start[e] + seg_start[ti * N_EXPERTS + e]
                pltpu.make_async_copy(_rows(y_hbm, src, n),
                                      _rows(ybuf.at[slot], run_start[ti * N_EXPERTS + e], n),
                                      sem.at[slot]).start()

    @pl.when(i == 0)
    def _():
        ybuf[...] = jnp.zeros_like(ybuf)
        gather(0, 0)

    @pl.when(i + 1 < nsteps)
    def _():
        gather(i + 1, (i + 1) % 2)

    slot = i % 2
    pltpu.make_async_copy(_rows(y_hbm, 0, tile_rows[i]), _rows(ybuf.at[slot], 0, tile_rows[i]),
                          sem.at[slot]).wait()
    gp = gp_ref[...]
    tm = gp.shape[0]
    sr = ybuf.shape[1]
    col = lax.broadcasted_iota(jnp.int32, (tm, sr), 1).astype(F32)
    weights = (jnp.where(col == gp[:, 2:3], gp[:, 0:1], 0.0)
               + jnp.where(col == gp[:, 3:4], gp[:, 1:2], 0.0)).astype(BF16)
    y_lo, y_hi = _unpack_halves(ybuf[slot])
    y = jnp.concatenate([_dot(weights, y_lo), _dot(weights, y_hi)], axis=-1)
    x2 = x_ref[...] + g2_ref[0] * y
    if final:
        ms = jnp.mean(x2 * x2, axis=-1, keepdims=True)
        x2 = (x2 * lax.rsqrt(ms + EPS)) * fgain_ref[...]
    o_ref[...] = x2


def _combine(tables, yb, x1, g2, gp, fgain, *, tokens_per_batch, final):
    t, d = x1.shape
    tm = min(MOE_TILE, tokens_per_batch)
    nsteps = t // tm
    per_b = tokens_per_batch // tm
    sr = _sorted_rows(tm)
    grid_spec = pltpu.PrefetchScalarGridSpec(
        num_scalar_prefetch=len(tables),
        grid=(nsteps,),
        in_specs=[
            pl.BlockSpec(memory_space=pl.ANY),
            pl.BlockSpec((tm, d), lambda i, *_: (i, 0)),
            pl.BlockSpec((1, 1, d), lambda i, *_: (i // per_b, 0, 0)),
            pl.BlockSpec((tm, LANES), lambda i, *_: (i, 0)),
            pl.BlockSpec((1, d), lambda i, *_: (0, 0)),
        ],
        out_specs=pl.BlockSpec((tm, d), lambda i, *_: (i, 0)),
        scratch_shapes=[pltpu.VMEM((2, sr, d // 2), jnp.uint32), pltpu.SemaphoreType.DMA((2,))],
    )
    return pl.pallas_call(
        functools.partial(_combine_kernel, nsteps=nsteps, final=final),
        grid_spec=grid_spec,
        out_shape=jax.ShapeDtypeStruct((t, d), F32),
        compiler_params=_params(("arbitrary",)),
        name="combine_final" if final else "combine",
    )(*tables, yb, x1, g2, gp, fgain)


def _reversal(size):
    row = lax.broadcasted_iota(jnp.int32, (size, size), 0)
    col = lax.broadcasted_iota(jnp.int32, (size, size), 1)
    return jnp.where(col == size - row, 1.0, 0.0).astype(BF16)


def _fmix_fold_kernel(x_ref, xm_ref, xb_ref, xc_ref, sh_ref, sc_ref, gain_ref, cc_ref, sc_mat_ref,
                      a_ref, b_ref, amid_ref):
    mod = lambda v: _modulate(v, gain_ref[...], sh_ref[0], sc_ref[0])
    h = mod(x_ref[...])
    ft = h.shape[0]
    m_hi, m_lo = _split_bf16(mod(xm_ref[...]))
    rev = _reversal(ft)
    row = lax.broadcasted_iota(jnp.int32, h.shape, 0)
    hr = jnp.where(row == 0, mod(xb_ref[...])[0:1], _dot(rev, m_hi) + _dot(rev, m_lo))
    he = (h + hr).astype(BF16)
    ho = (h - hr).astype(BF16)
    hc = mod(xc_ref[...]).astype(BF16)
    gw = cc_ref.shape[0]
    for g in range(FOURIER_GROUPS):
        cols = slice(g * gw, (g + 1) * gw)
        a_ref[:, cols] = _dot(he[:, cols], cc_ref[...]).astype(BF16)
        b_ref[:, cols] = _dot(ho[:, cols], sc_mat_ref[...]).astype(BF16)
        amid_ref[:, cols] = _dot(hc[:, cols], cc_ref[...])


def _fmix_fold(x, shift, scale, gain, cc, sc, *, bsz, n):
    t, d = x.shape
    hn = n // 2
    ft = min(512, hn)
    tpb = n // ft
    half = tpb // 2
    gw = d // FOURIER_GROUPS
    mod_idx = lambda b, j: (b, 0, 0)
    return pl.pallas_call(
        _fmix_fold_kernel,
        grid=(bsz, half),
        in_specs=[
            pl.BlockSpec((ft, d), lambda b, j: (b * tpb + j, 0)),
            pl.BlockSpec((ft, d), lambda b, j: (b * tpb + tpb - 1 - j, 0)),
            pl.BlockSpec((8, d), lambda b, j: (b * (n // 8) + (ft // 8) * ((tpb - j) % tpb), 0)),
            pl.BlockSpec((8, d), lambda b, j: (b * (n // 8) + hn // 8, 0)),
            pl.BlockSpec((1, 1, d), mod_idx), pl.BlockSpec((1, 1, d), mod_idx),
            pl.BlockSpec((1, d), lambda b, j: (0, 0)),
            pl.BlockSpec((gw, gw), lambda b, j: (0, 0)), pl.BlockSpec((gw, gw), lambda b, j: (0, 0)),
        ],
        out_specs=[pl.BlockSpec((ft, d), lambda b, j: (b * half + j, 0)),
                   pl.BlockSpec((ft, d), lambda b, j: (b * half + j, 0)),
                   pl.BlockSpec((8, d), lambda b, j: (b, 0))],
        out_shape=[jax.ShapeDtypeStruct((bsz * hn, d), BF16), jax.ShapeDtypeStruct((bsz * hn, d), BF16),
                   jax.ShapeDtypeStruct((bsz * 8, d), F32)],
        compiler_params=_params(("arbitrary", "arbitrary")),
        name="fmix_fold",
    )(x, x, x, x, shift, scale, gain, cc, sc)


def _fmix_pos_kernel(ch_ref, sh_ref, sign_ref, a_ref, b_ref, amid_ref, dif_ref, sum_ref, p_acc, q_acc):
    k = pl.program_id(2)

    @pl.when(k == 0)
    def _():
        p_acc[...] = jnp.zeros_like(p_acc)
        q_acc[...] = jnp.zeros_like(q_acc)

    p_acc[...] += _dot(ch_ref[...], a_ref[0])
    q_acc[...] += _dot(sh_ref[...], b_ref[0])

    @pl.when(k == pl.num_programs(2) - 1)
    def _():
        p = p_acc[...] + sign_ref[...] * amid_ref[0, 0:1, :]
        dif_ref[0] = (p - q_acc[...]).astype(BF16)
        sum_ref[0] = (p + q_acc[...]).astype(BF16)


def _fmix_pos(ch, sh, sign, a, b, amid):
    bsz, hn, d = a.shape
    mp = ch.shape[0]
    tmr = mp // 2
    tk = min(1024, hn)
    out = jax.ShapeDtypeStruct((bsz, mp, d), BF16)
    return pl.pallas_call(
        _fmix_pos_kernel,
        grid=(mp // tmr, bsz, hn // tk),
        in_specs=[
            pl.BlockSpec((tmr, tk), lambda i, bi, k: (i, k)),
            pl.BlockSpec((tmr, tk), lambda i, bi, k: (i, k)),
            pl.BlockSpec((tmr, 1), lambda i, bi, k: (i, 0)),
            pl.BlockSpec((1, tk, d), lambda i, bi, k: (bi, k, 0)),
            pl.BlockSpec((1, tk, d), lambda i, bi, k: (bi, k, 0)),
            pl.BlockSpec((1, 8, d), lambda i, bi, k: (bi, 0, 0)),
        ],
        out_specs=[pl.BlockSpec((1, tmr, d), lambda i, bi, k: (bi, i, 0)),
                   pl.BlockSpec((1, tmr, d), lambda i, bi, k: (bi, i, 0))],
        out_shape=[out, out],
        scratch_shapes=[pltpu.VMEM((tmr, d), F32), pltpu.VMEM((tmr, d), F32)],
        compiler_params=_params(("arbitrary", "arbitrary", "arbitrary")),
        name="fmix_pos",
    )(ch, sh, sign, a, b, amid)


def _rope_tables(n):
    half = ROPE_AXIS_DIM // 2
    rows = n // GRID_W
    inv = ROPE_THETA ** (-jnp.arange(half, dtype=F32) / half)
    pos = jnp.arange(n, dtype=jnp.int32)
    ang_r = (pos // GRID_W).astype(F32)[:, None] * inv
    ang_c = (pos % GRID_W).astype(F32)[:, None] * inv
    del rows
    cos = jnp.concatenate([jnp.cos(ang_r)] * 2 + [jnp.cos(ang_c)] * 2, axis=-1)
    sin = jnp.concatenate([-jnp.sin(ang_r), jnp.sin(ang_r), -jnp.sin(ang_c), jnp.sin(ang_c)], axis=-1)
    return jnp.tile(cos, (1, LANES // HEAD_DIM)), jnp.tile(sin, (1, LANES // HEAD_DIM))


def _dft_mats(n, scale):
    r = 1
    while r * r < n:
        r *= 2
    c = n // r
    k = jnp.arange(n, dtype=jnp.int32)[:, None]
    ang_a = ((k * jnp.arange(r, dtype=jnp.int32)[None, :] * c) % n).astype(F32) * (2.0 * math.pi / n)
    ang_b = ((k * jnp.arange(c, dtype=jnp.int32)[None, :]) % n).astype(F32) * (2.0 * math.pi / n)
    ca, sa, cb, sb = jnp.cos(ang_a), jnp.sin(ang_a), jnp.cos(ang_b), jnp.sin(ang_b)
    cosm = (ca[:, :, None] * cb[:, None, :] - sa[:, :, None] * sb[:, None, :]).reshape(n, n)
    sinm = (sa[:, :, None] * cb[:, None, :] + ca[:, :, None] * sb[:, None, :]).reshape(n, n)
    return (cosm * scale).astype(BF16), (sinm * scale).astype(BF16)


DFT_ROW_PAD = 32


def _dft_half(n):
    hn = n // 2
    mp = hn + DFT_ROW_PAD
    r = 1
    while r * r < hn:
        r *= 2
    c = hn // r
    k = jnp.arange(mp, dtype=jnp.int32)[:, None]
    valid = k <= hn
    w = 2.0 * math.pi / n
    ang_a = ((k * (jnp.arange(r, dtype=jnp.int32)[None, :] * c)) % n).astype(F32) * w
    ang_b = ((k * jnp.arange(c, dtype=jnp.int32)[None, :]) % n).astype(F32) * w
    ca, sa, cb, sb = jnp.cos(ang_a), jnp.sin(ang_a), jnp.cos(ang_b), jnp.sin(ang_b)
    cosm = (ca[:, :, None] * cb[:, None, :] - sa[:, :, None] * sb[:, None, :]).reshape(mp, hn)
    sinm = (sa[:, :, None] * cb[:, None, :] + ca[:, :, None] * sb[:, None, :]).reshape(mp, hn)
    scale = n ** -0.5
    col_w = jnp.where(jnp.arange(hn)[None, :] == 0, 0.5 * scale, scale)
    cosm = jnp.where(valid, cosm * col_w, 0.0).astype(BF16)
    sinm = jnp.where(valid, sinm * scale, 0.0).astype(BF16)
    sign = jnp.where(valid, jnp.where(k % 2 == 0, scale, -scale), 0.0).astype(F32)
    return cosm, sinm, sign


def _moe(a, w_o, x, g1, g2, gain, shift, scale, w_rg, b_rg, w_re, b_re, w1, w3, w2, fgain, *,
         layer, tokens_per_batch, final):
    t, d = x.shape
    pad = LANES - N_GROUPS - N_EXPERTS
    w_r = jnp.concatenate([w_rg, w_re, jnp.zeros((d, pad), F32)], axis=1)
    b_r = jnp.concatenate([b_rg, b_re, jnp.zeros((pad,), F32)])[None, :]
    wr_hi = w_r.astype(BF16)
    wr_lo = jnp.concatenate([wr_hi, (w_r - wr_hi.astype(F32)).astype(BF16)], axis=1)
    x1, s_tiles, gp, pc = _proj_moe(a, w_o, x, g1, gain, shift, scale, wr_hi, wr_lo, b_r,
                                    tokens_per_batch=tokens_per_batch)
    tm = min(MOE_TILE, tokens_per_batch)
    nt = t // tm
    sr = _sorted_rows(tm)
    bmu = FFN_BLOCK // RUN_ALIGN
    i32 = jnp.int32
    run_len = pc.reshape(nt, 8, LANES)[:, 0, :N_EXPERTS]
    seg_start = jnp.concatenate([jnp.zeros((1, N_EXPERTS), i32), jnp.cumsum(run_len, axis=0)])
    seg_len = seg_start[-1]
    seg_pad = (seg_len + bmu - 1) // bmu * bmu
    exp_end = jnp.cumsum(seg_pad)
    exp_start = exp_end - seg_pad
    run_start = jnp.cumsum(run_len, axis=1) - run_len
    tile_rows = run_start[:, -1] + run_len[:, -1]
    max_units = (2 * t) // RUN_ALIGN + nt * N_EXPERTS + N_EXPERTS * (bmu - 1)
    nblk = (max_units + bmu - 1) // bmu
    blk_start = jnp.arange(nblk, dtype=i32) * bmu
    blk_e = jnp.minimum(jnp.sum((exp_end[None, :] <= blk_start[:, None]).astype(i32), axis=1),
                        N_EXPERTS - 1)
    blk_b0 = blk_start - exp_start[blk_e]
    blk_n = jnp.clip(seg_len[blk_e] - blk_b0, 0, bmu)
    seg_e = seg_start.T[blk_e]
    blk_tlo = jnp.sum((seg_e[:, 1:] <= blk_b0[:, None]).astype(i32), axis=1)
    blk_thi = jnp.sum((seg_e[:, :-1] < (blk_b0 + bmu)[:, None]).astype(i32), axis=1)
    nused = (exp_end[-1] // bmu)[None]
    flat = lambda v: v.reshape(-1).astype(i32)
    yb = _ffn((seg_pad // bmu, exp_start // bmu, blk_e, blk_b0, blk_n, blk_tlo, blk_thi,
               flat(seg_start), flat(run_start), nused),
              s_tiles, w1, w3, w2, layer=layer, nblk=nblk, sr=sr)
    return _combine((flat(run_len), flat(run_start), flat(seg_start), exp_start, tile_rows),
                    yb, x1, g2, gp, fgain, tokens_per_batch=tokens_per_batch, final=final)


def kernel(x, c, ctx, c_ctx, w_mod, b_mod, norm_mix, norm_ffn, attn_w_qkv, attn_q_norm, attn_k_norm,
           attn_w_o, fourier_w_o, moe_w_rg, moe_b_rg, moe_w_re, moe_b_re, moe_w1, moe_w3, moe_w2,
           final_norm):
    bsz, n, d = x.shape
    t = bsz * n
    q_dim = N_HEADS * HEAD_DIM
    kv_dim = N_KV_HEADS * HEAD_DIM

    cond = jnp.zeros((16, d), F32).at[:bsz].set(c).at[bsz].set(c_ctx)
    mods = _ada(cond, w_mod, b_mod)

    def lat(layer, j):
        return mods[layer, :bsz, j * d:(j + 1) * d][:, None, :]

    def ctxm(layer, j):
        return mods[layer, bsz:bsz + 1, j * d:(j + 1) * d][:, None, :]

    w_qkv = attn_w_qkv[0].astype(BF16)
    qscale = (HEAD_DIM ** -0.5) * math.log2(math.e)
    head_gain = jnp.stack([jnp.tile(attn_q_norm[0] * qscale, LANES // HEAD_DIM),
                           jnp.tile(attn_k_norm[0], LANES // HEAD_DIM)])
    lane = jnp.arange(LANES)
    ebd = jnp.where((lane[:, None] // HEAD_DIM) == (lane[None, :] // HEAD_DIM),
                    1.0 / HEAD_DIM, 0.0).astype(BF16)
    cos, sin = _rope_tables(n)
    gain0 = norm_mix[0][None, :]
    q, k_l, v_l = _qkv(x, lat(0, 0), lat(0, 1), gain0, w_qkv, head_gain, ebd, cos, sin,
                       with_q=True, per_batch_mods=True)
    n_ctx = ctx.shape[1]
    k_c, v_c = _qkv(ctx, ctxm(0, 0), ctxm(0, 1), gain0, w_qkv[:, q_dim:], head_gain, ebd,
                    jnp.ones((n_ctx, LANES), F32), jnp.zeros((n_ctx, LANES), F32),
                    with_q=False, per_batch_mods=False)
    del kv_dim
    o = _attention(q, k_l, v_l, k_c, v_c).reshape(t, q_dim)

    xf = x.reshape(t, d)
    moe_w = lambda i: (moe_w_rg[i], moe_b_rg[i], moe_w_re[i], moe_b_re[i],
                       moe_w1, moe_w3, moe_w2)
    fgain = final_norm[None, :]
    x2 = _moe(o, attn_w_o[0].astype(BF16), xf, lat(0, 2), lat(0, 5), norm_ffn[0][None, :],
              lat(0, 3), lat(0, 4), *moe_w(0), fgain, layer=0, tokens_per_batch=n, final=False)

    gw = d // FOURIER_GROUPS
    cc, sc = _dft_mats(gw, gw ** -0.5)
    ch, sh, sign = _dft_half(n)
    a_e, b_o, a_mid = _fmix_fold(x2, lat(1, 0), lat(1, 1), norm_mix[1][None, :], cc, sc, bsz=bsz, n=n)
    f = tuple(_fmix_pos(ch, sh, sign, a_e.reshape(bsz, n // 2, d), b_o.reshape(bsz, n // 2, d),
                        a_mid.reshape(bsz, 8, d)))
    out = _moe(f, fourier_w_o[0].astype(BF16), x2, lat(1, 2), lat(1, 5), norm_ffn[1][None, :],
               lat(1, 3), lat(1, 4), *moe_w(1), fgain, layer=1, tokens_per_batch=n, final=True)
    return out.reshape(bsz, n, d)
```

```python
import functools
import math

import jax
import jax.numpy as jnp
from jax import lax
from jax.experimental import pallas as pl
from jax.experimental.pallas import tpu as pltpu

F32 = jnp.float32
BF16 = jnp.bfloat16

N_HEADS = 16
N_KV_HEADS = 4
HEAD_DIM = 64
GQA_GROUP = N_HEADS // N_KV_HEADS
GRID_W = 64
ROPE_AXIS_DIM = HEAD_DIM // 2
ROPE_THETA = 10000.0
FOURIER_GROUPS = 4
N_GROUPS = 4
EXPERTS_PER_GROUP = 8
N_EXPERTS = N_GROUPS * EXPERTS_PER_GROUP
EPS = 1e-6
LANES = 128
NEG_BIG = -1e30
VMEM_LIMIT = 48 * 1024 * 1024

FFN_BLOCK = 512
MOE_TILE = 512
RUN_ALIGN = 8


def _sorted_rows(tile_tokens):
    return 2 * tile_tokens + N_EXPERTS * RUN_ALIGN


def _dot(a, b):
    return jnp.dot(a, b, preferred_element_type=F32)


def _split_bf16(a):
    hi = a.astype(BF16)
    lo = (a - hi.astype(F32)).astype(BF16)
    return hi, lo


_HI16 = 0xFFFF0000


def _pack_halves(v):
    w = v.shape[1] // 2
    lo = lax.bitcast_convert_type(v[:, :w], jnp.uint32) >> 16
    hi = lax.bitcast_convert_type(v[:, w:], jnp.uint32) & jnp.uint32(_HI16)
    return hi | lo


def _unpack_halves(p):
    lo = lax.bitcast_convert_type(p << 16, F32).astype(BF16)
    hi = lax.bitcast_convert_type(p & jnp.uint32(_HI16), F32).astype(BF16)
    return lo, hi


def _params(sem):
    return pltpu.CompilerParams(dimension_semantics=sem, vmem_limit_bytes=VMEM_LIMIT)


def _ada_kernel(c_ref, w_ref, b_ref, o_ref):
    c = c_ref[...]
    s = c / (1.0 + jnp.exp(-c))
    s_hi, s_lo = _split_bf16(s)
    w_hi, w_lo = _split_bf16(w_ref[0])
    o_ref[0] = _dot(s_hi, w_hi) + _dot(s_hi, w_lo) + _dot(s_lo, w_hi) + b_ref[0]


def _ada(cond, w_mod, b_mod):
    depth, d, d6 = w_mod.shape
    rows = cond.shape[0]
    tn = 1536
    return pl.pallas_call(
        _ada_kernel,
        grid=(depth, d6 // tn),
        in_specs=[
            pl.BlockSpec((rows, d), lambda l, j: (0, 0)),
            pl.BlockSpec((1, d, tn), lambda l, j: (l, 0, j)),
            pl.BlockSpec((1, 1, tn), lambda l, j: (l, 0, j)),
        ],
        out_specs=pl.BlockSpec((1, rows, tn), lambda l, j: (l, 0, j)),
        out_shape=jax.ShapeDtypeStruct((depth, rows, d6), F32),
        compiler_params=_params(("arbitrary", "arbitrary")),
        name="ada",
    )(cond, w_mod, b_mod.reshape(depth, 1, d6))


def _modulate(xf, gain, shift, scale):
    ms = jnp.mean(xf * xf, axis=-1, keepdims=True)
    return (xf * lax.rsqrt(ms + EPS)) * (gain * (1.0 + scale)) + shift


def _qkv_kernel(x_ref, sh_ref, sc_ref, gain_ref, w_ref, hg_ref, ebd_ref, cos_ref, sin_ref,
                *out_refs, with_q):
    h = _modulate(x_ref[0], gain_ref[...], sh_ref[0], sc_ref[0])
    qkv = _dot(h.astype(BF16), w_ref[...])
    tm = qkv.shape[0]
    lane = lax.broadcasted_iota(jnp.int32, (tm, LANES), 1)
    first_half = (lane & 16) == 0
    low = lane < HEAD_DIM
    if with_q:
        q_ref, k_ref, v_ref = out_refs
        nq = N_HEADS * HEAD_DIM // LANES
    else:
        k_ref, v_ref = out_refs
        nq = 0
    nk = N_KV_HEADS * HEAD_DIM // LANES
    cos = cos_ref[...]
    sin = sin_ref[...]

    def norm_rope(c, gain_row):
        ms = _dot((c * c).astype(BF16), ebd_ref[...])
        cn = c * lax.rsqrt(ms + EPS) * gain_row
        partner = jnp.where(first_half, pltpu.roll(cn, LANES - 16, 1), pltpu.roll(cn, 16, 1))
        return cn * cos + partner * sin

    for j in range(nq):
        out = norm_rope(qkv[:, j * LANES:(j + 1) * LANES], hg_ref[0:1, :]).astype(BF16)
        q_ref[0, 2 * j] = out[:, :HEAD_DIM]
        q_ref[0, 2 * j + 1] = out[:, HEAD_DIM:]
    for j in range(nk):
        out = norm_rope(qkv[:, (nq + j) * LANES:(nq + j + 1) * LANES], hg_ref[1:2, :]).astype(BF16)
        k_ref[0, 2 * j] = out[:, :HEAD_DIM]
        k_ref[0, 2 * j + 1] = out[:, HEAD_DIM:]
    for j in range(nk):
        c = qkv[:, (nq + nk + j) * LANES:(nq + nk + j + 1) * LANES]
        v_ref[0, 2 * j] = jnp.where(low, c, 1.0).astype(BF16)
        v_ref[0, 2 * j + 1] = jnp.where(low, pltpu.roll(c, HEAD_DIM, 1), 1.0).astype(BF16)


def _qkv(x, shift, scale, gain, w, head_gain, ebd, cos, sin, *, with_q, per_batch_mods):
    b, n, d = x.shape
    tm = min(512, n)
    ncol = w.shape[1]
    mod_idx = (lambda bi, i: (bi, 0, 0)) if per_batch_mods else (lambda bi, i: (0, 0, 0))
    out_shape = []
    out_specs = []
    if with_q:
        out_shape.append(jax.ShapeDtypeStruct((b, N_HEADS, n, HEAD_DIM), BF16))
        out_specs.append(pl.BlockSpec((1, N_HEADS, tm, HEAD_DIM), lambda bi, i: (bi, 0, i, 0)))
    out_shape.append(jax.ShapeDtypeStruct((b, N_KV_HEADS, n, HEAD_DIM), BF16))
    out_specs.append(pl.BlockSpec((1, N_KV_HEADS, tm, HEAD_DIM), lambda bi, i: (bi, 0, i, 0)))
    out_shape.append(jax.ShapeDtypeStruct((b, N_KV_HEADS, n, LANES), BF16))
    out_specs.append(pl.BlockSpec((1, N_KV_HEADS, tm, LANES), lambda bi, i: (bi, 0, i, 0)))
    return pl.pallas_call(
        functools.partial(_qkv_kernel, with_q=with_q),
        grid=(b, n // tm),
        in_specs=[
            pl.BlockSpec((1, tm, d), lambda bi, i: (bi, i, 0)),
            pl.BlockSpec((1, 1, d), mod_idx),
            pl.BlockSpec((1, 1, d), mod_idx),
            pl.BlockSpec((1, d), lambda bi, i: (0, 0)),
            pl.BlockSpec((d, ncol), lambda bi, i: (0, 0)),
            pl.BlockSpec((2, LANES), lambda bi, i: (0, 0)),
            pl.BlockSpec((LANES, LANES), lambda bi, i: (0, 0)),
            pl.BlockSpec((tm, LANES), lambda bi, i: (i, 0)),
            pl.BlockSpec((tm, LANES), lambda bi, i: (i, 0)),
        ],
        out_specs=out_specs,
        out_shape=out_shape,
        compiler_params=_params(("arbitrary", "arbitrary")),
        name="qkv" if with_q else "ctx_kv",
    )(x, shift, scale, gain, w, head_gain, ebd, cos, sin)


def _attn_kernel(q_ref, k_ref, v_ref, kc_ref, vc_ref, o_ref, *, tq, tk, n_sub):
    chunks = []
    for kr, vr in ((k_ref, v_ref), (kc_ref, vc_ref)):
        size = min(tk, kr.shape[2])
        chunks += [(kr, vr, c, size) for c in range(kr.shape[2] // size)]
    sub = tq // n_sub
    m_rows = GQA_GROUP * sub
    for u in range(n_sub):
        q = q_ref[0, :, u * sub:(u + 1) * sub, :].reshape(m_rows, HEAD_DIM)
        m = jnp.full((m_rows, 1), -jnp.inf, F32)
        acc = jnp.zeros((m_rows, LANES), F32)
        for kr, vr, c, size in chunks:
            kc = kr[0, 0, c * size:(c + 1) * size, :]
            vc = vr[0, 0, c * size:(c + 1) * size, :]
            s = lax.dot_general(q, kc, (((1,), (1,)), ((), ())), preferred_element_type=F32)
            m_new = jnp.maximum(m, jnp.max(s, axis=-1, keepdims=True))
            alpha = jnp.exp2(m - m_new)
            p = jnp.exp2(s - m_new)
            acc = alpha * acc + _dot(p.astype(BF16), vc)
            m = m_new
        o = acc * (1.0 / pltpu.roll(acc, HEAD_DIM, 1))
        o_ref[0, u * sub:(u + 1) * sub, :] = jnp.concatenate(
            [o[g * sub:(g + 1) * sub, :HEAD_DIM] for g in range(GQA_GROUP)], axis=-1).astype(BF16)


def _attention(q, k, v, k_ctx, v_ctx):
    b, _, n, _ = q.shape
    n_ctx = k_ctx.shape[2]
    tq = min(512, n)
    tk = 256
    assert n % min(tk, n) == 0 and n_ctx % min(tk, n_ctx) == 0
    kv_spec = lambda rows, width: pl.BlockSpec((1, 1, rows, width), lambda bi, kh, qi: (bi, kh, 0, 0))
    return pl.pallas_call(
        functools.partial(_attn_kernel, tq=tq, tk=tk, n_sub=tq // min(256, tq)),
        grid=(b, N_KV_HEADS, n // tq),
        in_specs=[
            pl.BlockSpec((1, GQA_GROUP, tq, HEAD_DIM), lambda bi, kh, qi: (bi, kh, qi, 0)),
            kv_spec(n, HEAD_DIM), kv_spec(n, LANES), kv_spec(n_ctx, HEAD_DIM), kv_spec(n_ctx, LANES),
        ],
        out_specs=pl.BlockSpec((1, tq, GQA_GROUP * HEAD_DIM), lambda bi, kh, qi: (bi, qi, kh)),
        out_shape=jax.ShapeDtypeStruct((b, n, N_HEADS * HEAD_DIM), BF16),
        compiler_params=_params(("arbitrary", "arbitrary", "arbitrary")),
        name="attn",
    )(q, k, v, k_ctx, v_ctx)


def _proj_moe_kernel(*refs, tiles_per_batch, folded):
    if folded:
        dif_ref, sum_ref, sumb_ref, *refs, a_scr = refs
        j = pl.program_id(0) % tiles_per_batch

        @pl.when(j < tiles_per_batch // 2)
        def _():
            a_scr[...] = dif_ref[0]

        @pl.when(j >= tiles_per_batch // 2)
        def _():
            rows = _dot(_reversal(a_scr.shape[0]), sum_ref[0])
            row = lax.broadcasted_iota(jnp.int32, rows.shape, 0)
            a_scr[...] = jnp.where(row == 0, sumb_ref[0, 0:1, :].astype(F32), rows).astype(BF16)

        a = a_scr[...]
    else:
        a_ref, *refs = refs
        a = a_ref[...]
    (wo_ref, x_ref, g1_ref, gain_ref, sh_ref, sc_ref, wrh_ref, wrl_ref, br_ref,
     x1_ref, s_ref, gp_ref, pc_ref) = refs
    x1 = x_ref[...] + g1_ref[0] * _dot(a, wo_ref[...])
    x1_ref[...] = x1
    h2 = _modulate(x1, gain_ref[...], sh_ref[0], sc_ref[0])
    h_hi, h_lo = _split_bf16(h2)
    r_hi = _dot(h_hi, wrl_ref[...])
    logits = (r_hi[:, :LANES] + _dot(h_lo, wrh_ref[...]) + r_hi[:, LANES:]
              + br_ref[...])
    tm = logits.shape[0]
    lane = lax.broadcasted_iota(jnp.int32, (tm, LANES), 1)
    lane_f = lane.astype(F32)

    def top(vals):
        best = jnp.max(vals, axis=-1, keepdims=True)
        idx = jnp.min(jnp.where(vals == best, lane_f, float(LANES)), axis=-1, keepdims=True)
        return best, idx

    lg = jnp.where(lane < N_GROUPS, logits, NEG_BIG)
    g_max, g_sel = top(lg)
    p_grp = 1.0 / jnp.sum(jnp.where(lane < N_GROUPS, jnp.exp(lg - g_max), 0.0), axis=-1, keepdims=True)
    lo = N_GROUPS + EXPERTS_PER_GROUP * g_sel
    le = jnp.where((lane_f >= lo) & (lane_f < lo + EXPERTS_PER_GROUP), logits, NEG_BIG)
    t1, i1 = top(le)
    t2, i2 = top(jnp.where(lane_f == i1, NEG_BIG, le))
    e21 = jnp.exp(t2 - t1)
    w1 = 1.0 / (1.0 + e21)
    gate0 = p_grp * w1
    gate1 = p_grp * (e21 * w1)
    e0 = i1 - N_GROUPS
    e1 = i2 - N_GROUPS

    onehot = jnp.where((lane_f == e0) | (lane_f == e1), 1.0, 0.0)
    row = lax.broadcasted_iota(jnp.int32, (tm, tm), 0)
    col = lax.broadcasted_iota(jnp.int32, (tm, tm), 1)
    lower = jnp.where(col < row, 1.0, 0.0).astype(BF16)
    before = _dot(lower, onehot.astype(BF16))
    cnt = jnp.sum(onehot, axis=0, keepdims=True)
    pc = jnp.floor((cnt + (RUN_ALIGN - 1)) * (1.0 / RUN_ALIGN))
    urow = lax.broadcasted_iota(jnp.int32, (LANES, LANES), 0)
    ucol = lax.broadcasted_iota(jnp.int32, (LANES, LANES), 1)
    upper = jnp.where(urow < ucol, 1.0, 0.0).astype(BF16)
    run_start = _dot(jnp.broadcast_to(pc, (8, LANES)).astype(BF16), upper)[0:1] * RUN_ALIGN
    base = run_start + before
    pos0 = jnp.sum(jnp.where(lane_f == e0, base, 0.0), axis=-1, keepdims=True)
    pos1 = jnp.sum(jnp.where(lane_f == e1, base, 0.0), axis=-1, keepdims=True)
    gp = jnp.where(lane == 0, gate0, jnp.where(lane == 1, gate1, jnp.where(lane == 2, pos0,
                                                                       jnp.where(lane == 3, pos1, 0.0))))
    gp_ref[...] = gp
    pc_ref[...] = jnp.broadcast_to(pc, (8, LANES)).astype(jnp.int32)

    pos_t = jnp.where(lane == 0, pos0, jnp.where(lane == 1, pos1, -1.0)).T
    sr = s_ref.shape[0]
    srow = lax.broadcasted_iota(jnp.int32, (sr, tm), 0).astype(F32)
    perm = jnp.where((srow == pos_t[0:1, :]) | (srow == pos_t[1:2, :]), 1.0, 0.0).astype(BF16)
    s_ref[...] = _pack_halves(_dot(perm, h_hi))


def _proj_moe(a, w_o, x, g1, gain, shift, scale, wr_hi, wr_lo, b_r, *, tokens_per_batch):
    t, d = x.shape
    tm = min(MOE_TILE, tokens_per_batch)
    nt = t // tm
    sr = _sorted_rows(tm)
    per_b = tokens_per_batch // tm
    mod_idx = lambda i: (i // per_b, 0, 0)
    row_spec = lambda w: pl.BlockSpec((tm, w), lambda i: (i, 0))
    full = lambda s: pl.BlockSpec(s, lambda i: (0,) * len(s))
    folded = isinstance(a, tuple)
    if folded:
        assert per_b % 2 == 0
        half = per_b // 2
        a_inputs = (a[0], a[1], a[1])
        a_specs = [
            pl.BlockSpec((1, tm, d), lambda i: (i // per_b, jnp.minimum(i % per_b, half - 1), 0)),
            pl.BlockSpec((1, tm, d), lambda i: (i // per_b, jnp.clip(per_b - 1 - i % per_b, 0, half - 1), 0)),
            pl.BlockSpec((1, 8, d), lambda i: (i // per_b, (tm // 8) * jnp.clip(per_b - i % per_b, 1, half), 0)),
        ]
        scratch = [pltpu.VMEM((tm, d), BF16)]
    else:
        a_inputs = (a,)
        a_specs = [row_spec(d)]
        scratch = []
    return pl.pallas_call(
        functools.partial(_proj_moe_kernel, tiles_per_batch=per_b, folded=folded),
        grid=(nt,),
        in_specs=a_specs + [
            full((d, d)), row_spec(d),
            pl.BlockSpec((1, 1, d), mod_idx), full((1, d)),
            pl.BlockSpec((1, 1, d), mod_idx), pl.BlockSpec((1, 1, d), mod_idx),
            full((d, LANES)), full((d, 2 * LANES)), full((1, LANES)),
        ],
        out_specs=[row_spec(d), pl.BlockSpec((sr, d // 2), lambda i: (i, 0)), row_spec(LANES),
                   pl.BlockSpec((8, LANES), lambda i: (i, 0))],
        out_shape=[
            jax.ShapeDtypeStruct((t, d), F32), jax.ShapeDtypeStruct((nt * sr, d // 2), jnp.uint32),
            jax.ShapeDtypeStruct((t, LANES), F32), jax.ShapeDtypeStruct((nt * 8, LANES), jnp.int32),
        ],
        scratch_shapes=scratch,
        compiler_params=_params(("arbitrary",)),
        name="proj_moe",
    )(*a_inputs, w_o, x, g1, gain, shift, scale, wr_hi, wr_lo, b_r)


def _rows(ref, start_units, n_units):
    return ref.at[pl.ds(pl.multiple_of(start_units * RUN_ALIGN, RUN_ALIGN), n_units * RUN_ALIGN)]


def _ffn_kernel(exp_nblk, exp_blk0, blk_e, blk_b0, blk_n, blk_tlo, blk_thi, seg_start, run_start,
                nused_ref, s_hbm, w1_ref, w3_ref, w2_ref, y_hbm, xg, yo, w1b, w3b, w2b, sem, osem,
                *, sr_units, nblk):
    e_step = pl.program_id(0)
    nused = nused_ref[0]
    n_slots, bm, _ = xg.shape
    n_ahead = n_slots - 1
    bm_units = bm // RUN_ALIGN

    def gather(b, slot):
        e = blk_e[b]
        b0 = blk_b0[b]

        def body(ti, carry):
            rs = seg_start[ti * N_EXPERTS + e]
            re = seg_start[(ti + 1) * N_EXPERTS + e]
            lo = jnp.maximum(rs, b0)
            n = jnp.minimum(re, b0 + bm_units) - lo

            @pl.when(n > 0)
            def _():
                src = ti * sr_units + run_start[ti * N_EXPERTS + e] + (lo - rs)
                pltpu.make_async_copy(_rows(s_hbm, src, n), _rows(xg.at[slot], lo - b0, n),
                                      sem.at[slot]).start()

            return carry

        lax.fori_loop(blk_tlo[b], blk_thi[b], body, 0)

    def write_back(g, slot):
        return pltpu.make_async_copy(yo.at[slot], y_hbm.at[pl.ds(pl.multiple_of(g * bm, bm), bm)],
                                     osem.at[slot])

    def emit(g, fill):
        slot = g % 2

        @pl.when(g >= 2)
        def _():
            write_back(g - 2, slot).wait()

        fill(slot)
        write_back(g, slot).start()

    @pl.when(e_step == 0)
    def _():
        xg[...] = jnp.zeros_like(xg)
        for g0 in range(n_ahead):

            @pl.when(g0 < nused)
            def _():
                gather(g0, g0)

    w1b[...] = w1_ref[0, 0].astype(BF16)
    w3b[...] = w3_ref[0, 0].astype(BF16)
    w2b[...] = w2_ref[0, 0].astype(BF16)

    def block(jb, carry):
        g = exp_blk0[e_step] + jb

        @pl.when(g + n_ahead < nused)
        def _():
            gather(g + n_ahead, (g + n_ahead) % n_slots)

        def fill(slot):
            gslot = g % n_slots
            pltpu.make_async_copy(_rows(s_hbm, 0, blk_n[g]), _rows(xg.at[gslot], 0, blk_n[g]),
                                  sem.at[gslot]).wait()
            x_lo, x_hi = _unpack_halves(xg[gslot])
            half = x_lo.shape[1]
            a = _dot(x_lo, w1b[:half, :]) + _dot(x_hi, w1b[half:, :])
            b = _dot(x_lo, w3b[:half, :]) + _dot(x_hi, w3b[half:, :])
            mid = (a / (1.0 + jnp.exp(-a))) * b
            y = _dot(mid.astype(BF16), w2b[...])
            yo[slot] = _pack_halves(y.astype(BF16).astype(F32))

        emit(g, fill)
        return carry

    lax.fori_loop(0, exp_nblk[e_step], block, 0)

    @pl.when(e_step == pl.num_programs(0) - 1)
    def _():
        def tail(g, carry):
            def fill(slot):
                yo[slot] = jnp.zeros(yo.shape[1:], yo.dtype)

            emit(g, fill)
            return carry

        lax.fori_loop(nused, nblk, tail, 0)
        write_back(nblk - 2, nblk % 2).wait()
        write_back(nblk - 1, (nblk - 1) % 2).wait()


def _ffn(tables, s_tiles, w1, w3, w2, *, layer, nblk, sr):
    bm = FFN_BLOCK
    d = w1.shape[2]
    de = w1.shape[3]
    nsp = len(tables)
    assert nblk >= 2
    grid_spec = pltpu.PrefetchScalarGridSpec(
        num_scalar_prefetch=nsp,
        grid=(N_EXPERTS,),
        in_specs=[
            pl.BlockSpec(memory_space=pl.ANY),
            pl.BlockSpec((1, 1, d, de), lambda e, *_: (layer, e, 0, 0)),
            pl.BlockSpec((1, 1, d, de), lambda e, *_: (layer, e, 0, 0)),
            pl.BlockSpec((1, 1, de, d), lambda e, *_: (layer, e, 0, 0)),
        ],
        out_specs=pl.BlockSpec(memory_space=pl.ANY),
        scratch_shapes=[pltpu.VMEM((3, bm, d // 2), jnp.uint32), pltpu.VMEM((2, bm, d // 2), jnp.uint32),
                        pltpu.VMEM((d, de), BF16), pltpu.VMEM((d, de), BF16), pltpu.VMEM((de, d), BF16),
                        pltpu.SemaphoreType.DMA((3,)), pltpu.SemaphoreType.DMA((2,))],
    )
    return pl.pallas_call(
        functools.partial(_ffn_kernel, sr_units=sr // RUN_ALIGN, nblk=nblk),
        grid_spec=grid_spec,
        out_shape=jax.ShapeDtypeStruct((nblk * bm, d // 2), jnp.uint32),
        compiler_params=_params(("arbitrary",)),
        name="ffn",
    )(*tables, s_tiles, w1, w3, w2)


def _combine_kernel(run_len, run_start, seg_start, exp_start, tile_rows,
                    y_hbm, x_ref, g2_ref, gp_ref, fgain_ref, o_ref, ybuf, sem, *, nsteps, final):
    i = pl.program_id(0)

    def gather(ti, slot):
        for e in range(N_EXPERTS):
            n = run_len[ti * N_EXPERTS + e]

            @pl.when(n > 0)
            def _():
                src = exp_start[e] + seg_start[ti * N_EXPERTS + e]
                pltpu.make_async_copy(_rows(y_hbm, src, n),
                                      _rows(ybuf.at[slot], run_start[ti * N_EXPERTS + e], n),
                                      sem.at[slot]).start()

    @pl.when(i == 0)
    def _():
        ybuf[...] = jnp.zeros_like(ybuf)
        gather(0, 0)

    @pl.when(i + 1 < nsteps)
    def _():
        gather(i + 1, (i + 1) % 2)

    slot = i % 2
    pltpu.make_async_copy(_rows(y_hbm, 0, tile_rows[i]), _rows(ybuf.at[slot], 0, tile_rows[i]),
                          sem.at[slot]).wait()
    gp = gp_ref[...]
    tm = gp.shape[0]
    sr = ybuf.shape[1]
    col = lax.broadcasted_iota(jnp.int32, (tm, sr), 1).astype(F32)
    weights = (jnp.where(col == gp[:, 2:3], gp[:, 0:1], 0.0)
               + jnp.where(col == gp[:, 3:4], gp[:, 1:2], 0.0)).astype(BF16)
    y_lo, y_hi = _unpack_halves(ybuf[slot])
    y = jnp.concatenate([_dot(weights, y_lo), _dot(weights, y_hi)], axis=-1)
    x2 = x_ref[...] + g2_ref[0] * y
    if final:
        ms = jnp.mean(x2 * x2, axis=-1, keepdims=True)
        x2 = (x2 * lax.rsqrt(ms + EPS)) * fgain_ref[...]
    o_ref[...] = x2


def _combine(tables, yb, x1, g2, gp, fgain, *, tokens_per_batch, final):
    t, d = x1.shape
    tm = min(MOE_TILE, tokens_per_batch)
    nsteps = t // tm
    per_b = tokens_per_batch // tm
    sr = _sorted_rows(tm)
    grid_spec = pltpu.PrefetchScalarGridSpec(
        num_scalar_prefetch=len(tables),
        grid=(nsteps,),
        in_specs=[
            pl.BlockSpec(memory_space=pl.ANY),
            pl.BlockSpec((tm, d), lambda i, *_: (i, 0)),
            pl.BlockSpec((1, 1, d), lambda i, *_: (i // per_b, 0, 0)),
            pl.BlockSpec((tm, LANES), lambda i, *_: (i, 0)),
            pl.BlockSpec((1, d), lambda i, *_: (0, 0)),
        ],
        out_specs=pl.BlockSpec((tm, d), lambda i, *_: (i, 0)),
        scratch_shapes=[pltpu.VMEM((2, sr, d // 2), jnp.uint32), pltpu.SemaphoreType.DMA((2,))],
    )
    return pl.pallas_call(
        functools.partial(_combine_kernel, nsteps=nsteps, final=final),
        grid_spec=grid_spec,
        out_shape=jax.ShapeDtypeStruct((t, d), F32),
        compiler_params=_params(("arbitrary",)),
        name="combine_final" if final else "combine",
    )(*tables, yb, x1, g2, gp, fgain)


def _reversal(size):
    row = lax.broadcasted_iota(jnp.int32, (size, size), 0)
    col = lax.broadcasted_iota(jnp.int32, (size, size), 1)
    return jnp.where(col == size - row, 1.0, 0.0).astype(BF16)


def _fmix_fold_kernel(x_ref, xm_ref, xb_ref, xc_ref, sh_ref, sc_ref, gain_ref, cc_ref, sc_mat_ref,
                      a_ref, b_ref, amid_ref):
    mod = lambda v: _modulate(v, gain_ref[...], sh_ref[0], sc_ref[0])
    h = mod(x_ref[...])
    ft = h.shape[0]
    m_hi, m_lo = _split_bf16(mod(xm_ref[...]))
    rev = _reversal(ft)
    row = lax.broadcasted_iota(jnp.int32, h.shape, 0)
    hr = jnp.where(row == 0, mod(xb_ref[...])[0:1], _dot(rev, m_hi) + _dot(rev, m_lo))
    he = (h + hr).astype(BF16)
    ho = (h - hr).astype(BF16)
    hc = mod(xc_ref[...]).astype(BF16)
    gw = cc_ref.shape[0]
    for g in range(FOURIER_GROUPS):
        cols = slice(g * gw, (g + 1) * gw)
        a_ref[:, cols] = _dot(he[:, cols], cc_ref[...]).astype(BF16)
        b_ref[:, cols] = _dot(ho[:, cols], sc_mat_ref[...]).astype(BF16)
        amid_ref[:, cols] = _dot(hc[:, cols], cc_ref[...])


def _fmix_fold(x, shift, scale, gain, cc, sc, *, bsz, n):
    t, d = x.shape
    hn = n // 2
    ft = min(512, hn)
    tpb = n // ft
    half = tpb // 2
    gw = d // FOURIER_GROUPS
    mod_idx = lambda b, j: (b, 0, 0)
    return pl.pallas_call(
        _fmix_fold_kernel,
        grid=(bsz, half),
        in_specs=[
            pl.BlockSpec((ft, d), lambda b, j: (b * tpb + j, 0)),
            pl.BlockSpec((ft, d), lambda b, j: (b * tpb + tpb - 1 - j, 0)),
            pl.BlockSpec((8, d), lambda b, j: (b * (n // 8) + (ft // 8) * ((tpb - j) % tpb), 0)),
            pl.BlockSpec((8, d), lambda b, j: (b * (n // 8) + hn // 8, 0)),
            pl.BlockSpec((1, 1, d), mod_idx), pl.BlockSpec((1, 1, d), mod_idx),
            pl.BlockSpec((1, d), lambda b, j: (0, 0)),
            pl.BlockSpec((gw, gw), lambda b, j: (0, 0)), pl.BlockSpec((gw, gw), lambda b, j: (0, 0)),
        ],
        out_specs=[pl.BlockSpec((ft, d), lambda b, j: (b * half + j, 0)),
                   pl.BlockSpec((ft, d), lambda b, j: (b * half + j, 0)),
                   pl.BlockSpec((8, d), lambda b, j: (b, 0))],
        out_shape=[jax.ShapeDtypeStruct((bsz * hn, d), BF16), jax.ShapeDtypeStruct((bsz * hn, d), BF16),
                   jax.ShapeDtypeStruct((bsz * 8, d), F32)],
        compiler_params=_params(("arbitrary", "arbitrary")),
        name="fmix_fold",
    )(x, x, x, x, shift, scale, gain, cc, sc)


def _fmix_pos_kernel(ch_ref, sh_ref, sign_ref, a_ref, b_ref, amid_ref, dif_ref, sum_ref):
    p = _dot(ch_ref[...], a_ref[0]) + sign_ref[...] * amid_ref[0, 0:1, :]
    q = _dot(sh_ref[...], b_ref[0])
    dif_ref[0] = (p - q).astype(BF16)
    sum_ref[0] = (p + q).astype(BF16)


def _fmix_pos(ch, sh, sign, a, b, amid):
    bsz, hn, d = a.shape
    mp = ch.shape[0]
    tmr = mp // DFT_ROW_BLOCKS
    assert tmr % 16 == 0
    out = jax.ShapeDtypeStruct((bsz, mp, d), BF16)
    return pl.pallas_call(
        _fmix_pos_kernel,
        grid=(bsz, DFT_ROW_BLOCKS),
        in_specs=[
            pl.BlockSpec((tmr, hn), lambda bi, i: (i, 0)),
            pl.BlockSpec((tmr, hn), lambda bi, i: (i, 0)),
            pl.BlockSpec((tmr, 1), lambda bi, i: (i, 0)),
            pl.BlockSpec((1, hn, d), lambda bi, i: (bi, 0, 0)),
            pl.BlockSpec((1, hn, d), lambda bi, i: (bi, 0, 0)),
            pl.BlockSpec((1, 8, d), lambda bi, i: (bi, 0, 0)),
        ],
        out_specs=[pl.BlockSpec((1, tmr, d), lambda bi, i: (bi, i, 0)),
                   pl.BlockSpec((1, tmr, d), lambda bi, i: (bi, i, 0))],
        out_shape=[out, out],
        compiler_params=_params(("arbitrary", "arbitrary")),
        name="fmix_pos",
    )(ch, sh, sign, a, b, amid)


def _rope_tables(n):
    half = ROPE_AXIS_DIM // 2
    rows = n // GRID_W
    inv = ROPE_THETA ** (-jnp.arange(half, dtype=F32) / half)
    pos = jnp.arange(n, dtype=jnp.int32)
    ang_r = (pos // GRID_W).astype(F32)[:, None] * inv
    ang_c = (pos % GRID_W).astype(F32)[:, None] * inv
    del rows
    cos = jnp.concatenate([jnp.cos(ang_r)] * 2 + [jnp.cos(ang_c)] * 2, axis=-1)
    sin = jnp.concatenate([-jnp.sin(ang_r), jnp.sin(ang_r), -jnp.sin(ang_c), jnp.sin(ang_c)], axis=-1)
    return jnp.tile(cos, (1, LANES // HEAD_DIM)), jnp.tile(sin, (1, LANES // HEAD_DIM))


def _dft_mats(n, scale):
    r = 1
    while r * r < n:
        r *= 2
    c = n // r
    k = jnp.arange(n, dtype=jnp.int32)[:, None]
    ang_a = ((k * jnp.arange(r, dtype=jnp.int32)[None, :] * c) % n).astype(F32) * (2.0 * math.pi / n)
    ang_b = ((k * jnp.arange(c, dtype=jnp.int32)[None, :]) % n).astype(F32) * (2.0 * math.pi / n)
    ca, sa, cb, sb = jnp.cos(ang_a), jnp.sin(ang_a), jnp.cos(ang_b), jnp.sin(ang_b)
    cosm = (ca[:, :, None] * cb[:, None, :] - sa[:, :, None] * sb[:, None, :]).reshape(n, n)
    sinm = (sa[:, :, None] * cb[:, None, :] + ca[:, :, None] * sb[:, None, :]).reshape(n, n)
    return (cosm * scale).astype(BF16), (sinm * scale).astype(BF16)


DFT_ROW_BLOCKS = 4
DFT_ROW_PAD = 64


def _dft_half(n):
    hn = n // 2
    mp = hn + DFT_ROW_PAD
    kb = 32
    ka = mp // kb
    assert ka * kb == mp
    m = jnp.arange(hn, dtype=jnp.int32)[None, :]
    w = 2.0 * math.pi / n
    ang_a = (((jnp.arange(ka, dtype=jnp.int32)[:, None] * kb) * m) % n).astype(F32) * w
    ang_b = ((jnp.arange(kb, dtype=jnp.int32)[:, None] * m) % n).astype(F32) * w
    ca, sa, cb, sb = jnp.cos(ang_a), jnp.sin(ang_a), jnp.cos(ang_b), jnp.sin(ang_b)
    cosm = (ca[:, None, :] * cb[None, :, :] - sa[:, None, :] * sb[None, :, :]).reshape(mp, hn)
    sinm = (sa[:, None, :] * cb[None, :, :] + ca[:, None, :] * sb[None, :, :]).reshape(mp, hn)
    k = jnp.arange(mp, dtype=jnp.int32)[:, None]
    valid = k <= hn
    scale = n ** -0.5
    col_w = jnp.where(m == 0, 0.5 * scale, scale)
    cosm = jnp.where(valid, cosm * col_w, 0.0).astype(BF16)
    sinm = jnp.where(valid, sinm * scale, 0.0).astype(BF16)
    sign = jnp.where(valid, jnp.where(k % 2 == 0, scale, -scale), 0.0).astype(F32)
    return cosm, sinm, sign


def _moe(a, w_o, x, g1, g2, gain, shift, scale, w_rg, b_rg, w_re, b_re, w1, w3, w2, fgain, *,
         layer, tokens_per_batch, final):
    t, d = x.shape
    pad = LANES - N_GROUPS - N_EXPERTS
    w_r = jnp.concatenate([w_rg, w_re, jnp.zeros((d, pad), F32)], axis=1)
    b_r = jnp.concatenate([b_rg, b_re, jnp.zeros((pad,), F32)])[None, :]
    wr_hi = w_r.astype(BF16)
    wr_lo = jnp.concatenate([wr_hi, (w_r - wr_hi.astype(F32)).astype(BF16)], axis=1)
    x1, s_tiles, gp, pc = _proj_moe(a, w_o, x, g1, gain, shift, scale, wr_hi, wr_lo, b_r,
                                    tokens_per_batch=tokens_per_batch)
    tm = min(MOE_TILE, tokens_per_batch)
    nt = t // tm
    sr = _sorted_rows(tm)
    bmu = FFN_BLOCK // RUN_ALIGN
    i32 = jnp.int32
    run_len = pc.reshape(nt, 8, LANES)[:, 0, :N_EXPERTS]
    seg_start = jnp.concatenate([jnp.zeros((1, N_EXPERTS), i32), jnp.cumsum(run_len, axis=0)])
    seg_len = seg_start[-1]
    seg_pad = (seg_len + bmu - 1) // bmu * bmu
    exp_end = jnp.cumsum(seg_pad)
    exp_start = exp_end - seg_pad
    run_start = jnp.cumsum(run_len, axis=1) - run_len
    tile_rows = run_start[:, -1] + run_len[:, -1]
    max_units = (2 * t) // RUN_ALIGN + nt * N_EXPERTS + N_EXPERTS * (bmu - 1)
    nblk = (max_units + bmu - 1) // bmu
    blk_start = jnp.arange(nblk, dtype=i32) * bmu
    blk_e = jnp.minimum(jnp.sum((exp_end[None, :] <= blk_start[:, None]).astype(i32), axis=1),
                        N_EXPERTS - 1)
    pick = (blk_e[:, None] == jnp.arange(N_EXPERTS, dtype=i32)[None, :]).astype(i32)
    blk_b0 = blk_start - jnp.sum(pick * exp_start[None, :], axis=1)
    blk_n = jnp.clip(jnp.sum(pick * seg_len[None, :], axis=1) - blk_b0, 0, bmu)
    seg_e = jnp.sum(pick[:, None, :] * seg_start[None, :, :], axis=2)
    blk_tlo = jnp.sum((seg_e[:, 1:] <= blk_b0[:, None]).astype(i32), axis=1)
    blk_thi = jnp.sum((seg_e[:, :-1] < (blk_b0 + bmu)[:, None]).astype(i32), axis=1)
    nused = (exp_end[-1] // bmu)[None]
    flat = lambda v: v.reshape(-1).astype(i32)
    yb = _ffn((seg_pad // bmu, exp_start // bmu, blk_e, blk_b0, blk_n, blk_tlo, blk_thi,
               flat(seg_start), flat(run_start), nused),
              s_tiles, w1, w3, w2, layer=layer, nblk=nblk, sr=sr)
    return _combine((flat(run_len), flat(run_start), flat(seg_start), exp_start, tile_rows),
                    yb, x1, g2, gp, fgain, tokens_per_batch=tokens_per_batch, final=final)


def kernel(x, c, ctx, c_ctx, w_mod, b_mod, norm_mix, norm_ffn, attn_w_qkv, attn_q_norm, attn_k_norm,
           attn_w_o, fourier_w_o, moe_w_rg, moe_b_rg, moe_w_re, moe_b_re, moe_w1, moe_w3, moe_w2,
           final_norm):
    bsz, n, d = x.shape
    t = bsz * n
    q_dim = N_HEADS * HEAD_DIM
    kv_dim = N_KV_HEADS * HEAD_DIM

    cond = jnp.zeros((16, d), F32).at[:bsz].set(c).at[bsz].set(c_ctx)
    mods = _ada(cond, w_mod, b_mod)

    def lat(layer, j):
        return mods[layer, :bsz, j * d:(j + 1) * d][:, None, :]

    def ctxm(layer, j):
        return mods[layer, bsz:bsz + 1, j * d:(j + 1) * d][:, None, :]

    w_qkv = attn_w_qkv[0].astype(BF16)
    qscale = (HEAD_DIM ** -0.5) * math.log2(math.e)
    head_gain = jnp.stack([jnp.tile(attn_q_norm[0] * qscale, LANES // HEAD_DIM),
                           jnp.tile(attn_k_norm[0], LANES // HEAD_DIM)])
    lane = jnp.arange(LANES)
    ebd = jnp.where((lane[:, None] // HEAD_DIM) == (lane[None, :] // HEAD_DIM),
                    1.0 / HEAD_DIM, 0.0).astype(BF16)
    cos, sin = _rope_tables(n)
    gain0 = norm_mix[0][None, :]
    q, k_l, v_l = _qkv(x, lat(0, 0), lat(0, 1), gain0, w_qkv, head_gain, ebd, cos, sin,
                       with_q=True, per_batch_mods=True)
    n_ctx = ctx.shape[1]
    k_c, v_c = _qkv(ctx, ctxm(0, 0), ctxm(0, 1), gain0, w_qkv[:, q_dim:], head_gain, ebd,
                    jnp.ones((n_ctx, LANES), F32), jnp.zeros((n_ctx, LANES), F32),
                    with_q=False, per_batch_mods=False)
    del kv_dim
    o = _attention(q, k_l, v_l, k_c, v_c).reshape(t, q_dim)

    xf = x.reshape(t, d)
    moe_w = lambda i: (moe_w_rg[i], moe_b_rg[i], moe_w_re[i], moe_b_re[i],
                       moe_w1, moe_w3, moe_w2)
    fgain = final_norm[None, :]
    x2 = _moe(o, attn_w_o[0].astype(BF16), xf, lat(0, 2), lat(0, 5), norm_ffn[0][None, :],
              lat(0, 3), lat(0, 4), *moe_w(0), fgain, layer=0, tokens_per_batch=n, final=False)

    gw = d // FOURIER_GROUPS
    cc, sc = _dft_mats(gw, gw ** -0.5)
    ch, sh, sign = _dft_half(n)
    a_e, b_o, a_mid = _fmix_fold(x2, lat(1, 0), lat(1, 1), norm_mix[1][None, :], cc, sc, bsz=bsz, n=n)
    f = tuple(_fmix_pos(ch, sh, sign, a_e.reshape(bsz, n // 2, d), b_o.reshape(bsz, n // 2, d),
                        a_mid.reshape(bsz, 8, d)))
    out = _moe(f, fourier_w_o[0].astype(BF16), x2, lat(1, 2), lat(1, 5), norm_ffn[1][None, :],
               lat(1, 3), lat(1, 4), *moe_w(1), fgain, layer=1, tokens_per_batch=n, final=True)
    return out.reshape(bsz, n, d)
```

```python
import functools
import math

import jax
import jax.numpy as jnp
from jax import lax
from jax.experimental import pallas as pl
from jax.experimental.pallas import tpu as pltpu

F32 = jnp.float32
BF16 = jnp.bfloat16

N_HEADS = 16
N_KV_HEADS = 4
HEAD_DIM = 64
GQA_GROUP = N_HEADS // N_KV_HEADS
GRID_W = 64
ROPE_AXIS_DIM = HEAD_DIM // 2
ROPE_THETA = 10000.0
FOURIER_GROUPS = 4
N_GROUPS = 4
EXPERTS_PER_GROUP = 8
N_EXPERTS = N_GROUPS * EXPERTS_PER_GROUP
EPS = 1e-6
LANES = 128
NEG_BIG = -1e30
VMEM_LIMIT = 48 * 1024 * 1024

FFN_BLOCK = 512
FFN_GATHER_SLOTS = 4
MOE_TILE = 512
RUN_ALIGN = 8


def _sorted_rows(tile_tokens):
    return 2 * tile_tokens + N_EXPERTS * RUN_ALIGN


def _dot(a, b):
    return jnp.dot(a, b, preferred_element_type=F32)


def _split_bf16(a):
    hi = a.astype(BF16)
    lo = (a - hi.astype(F32)).astype(BF16)
    return hi, lo


_HI16 = 0xFFFF0000


def _pack_halves(v):
    w = v.shape[1] // 2
    lo = lax.bitcast_convert_type(v[:, :w], jnp.uint32) >> 16
    hi = lax.bitcast_convert_type(v[:, w:], jnp.uint32) & jnp.uint32(_HI16)
    return hi | lo


def _unpack_halves(p):
    lo = lax.bitcast_convert_type(p << 16, F32).astype(BF16)
    hi = lax.bitcast_convert_type(p & jnp.uint32(_HI16), F32).astype(BF16)
    return lo, hi


def _params(sem):
    return pltpu.CompilerParams(dimension_semantics=sem, vmem_limit_bytes=VMEM_LIMIT)


def _ada_kernel(c_ref, w_ref, b_ref, o_ref):
    c = c_ref[...]
    s = c / (1.0 + jnp.exp(-c))
    s_hi, s_lo = _split_bf16(s)
    w_hi, w_lo = _split_bf16(w_ref[0])
    o_ref[0] = _dot(s_hi, w_hi) + _dot(s_hi, w_lo) + _dot(s_lo, w_hi) + b_ref[0]


def _ada(cond, w_mod, b_mod):
    depth, d, d6 = w_mod.shape
    rows = cond.shape[0]
    tn = 1536
    return pl.pallas_call(
        _ada_kernel,
        grid=(depth, d6 // tn),
        in_specs=[
            pl.BlockSpec((rows, d), lambda l, j: (0, 0)),
            pl.BlockSpec((1, d, tn), lambda l, j: (l, 0, j)),
            pl.BlockSpec((1, 1, tn), lambda l, j: (l, 0, j)),
        ],
        out_specs=pl.BlockSpec((1, rows, tn), lambda l, j: (l, 0, j)),
        out_shape=jax.ShapeDtypeStruct((depth, rows, d6), F32),
        compiler_params=_params(("arbitrary", "arbitrary")),
        name="ada",
    )(cond, w_mod, b_mod.reshape(depth, 1, d6))


def _modulate(xf, gain, shift, scale):
    ms = jnp.mean(xf * xf, axis=-1, keepdims=True)
    return (xf * lax.rsqrt(ms + EPS)) * (gain * (1.0 + scale)) + shift


def _qkv_kernel(x_ref, sh_ref, sc_ref, gain_ref, w_ref, hg_ref, ebd_ref, cos_ref, sin_ref,
                *out_refs, with_q):
    h = _modulate(x_ref[0], gain_ref[...], sh_ref[0], sc_ref[0])
    qkv = _dot(h.astype(BF16), w_ref[...])
    tm = qkv.shape[0]
    lane = lax.broadcasted_iota(jnp.int32, (tm, LANES), 1)
    first_half = (lane & 16) == 0
    low = lane < HEAD_DIM
    if with_q:
        q_ref, k_ref, v_ref = out_refs
        nq = N_HEADS * HEAD_DIM // LANES
    else:
        k_ref, v_ref = out_refs
        nq = 0
    nk = N_KV_HEADS * HEAD_DIM // LANES
    cos = cos_ref[...]
    sin = sin_ref[...]
    srow = lax.broadcasted_iota(jnp.int32, (LANES, LANES), 0)
    scol = lax.broadcasted_iota(jnp.int32, (LANES, LANES), 1)
    swap = jnp.where(srow == (scol ^ 16), 1.0, 0.0).astype(BF16)

    def norm_rope(c, gain_row):
        ms = _dot((c * c).astype(BF16), ebd_ref[...])
        cn = c * lax.rsqrt(ms + EPS) * gain_row
        c_hi, c_lo = _split_bf16(cn)
        partner = _dot(c_hi, swap) + _dot(c_lo, swap)
        return cn * cos + partner * sin

    for j in range(nq):
        out = norm_rope(qkv[:, j * LANES:(j + 1) * LANES], hg_ref[0:1, :]).astype(BF16)
        q_ref[0, 2 * j] = out[:, :HEAD_DIM]
        q_ref[0, 2 * j + 1] = out[:, HEAD_DIM:]
    for j in range(nk):
        out = norm_rope(qkv[:, (nq + j) * LANES:(nq + j + 1) * LANES], hg_ref[1:2, :]).astype(BF16)
        k_ref[0, 2 * j] = out[:, :HEAD_DIM]
        k_ref[0, 2 * j + 1] = out[:, HEAD_DIM:]
    for j in range(nk):
        c = qkv[:, (nq + nk + j) * LANES:(nq + nk + j + 1) * LANES]
        v_ref[0, 2 * j] = jnp.where(low, c, 1.0).astype(BF16)
        v_ref[0, 2 * j + 1] = jnp.where(low, pltpu.roll(c, HEAD_DIM, 1), 1.0).astype(BF16)


def _qkv(x, shift, scale, gain, w, head_gain, ebd, cos, sin, *, with_q, per_batch_mods):
    b, n, d = x.shape
    tm = min(512, n)
    ncol = w.shape[1]
    mod_idx = (lambda bi, i: (bi, 0, 0)) if per_batch_mods else (lambda bi, i: (0, 0, 0))
    out_shape = []
    out_specs = []
    if with_q:
        out_shape.append(jax.ShapeDtypeStruct((b, N_HEADS, n, HEAD_DIM), BF16))
        out_specs.append(pl.BlockSpec((1, N_HEADS, tm, HEAD_DIM), lambda bi, i: (bi, 0, i, 0)))
    out_shape.append(jax.ShapeDtypeStruct((b, N_KV_HEADS, n, HEAD_DIM), BF16))
    out_specs.append(pl.BlockSpec((1, N_KV_HEADS, tm, HEAD_DIM), lambda bi, i: (bi, 0, i, 0)))
    out_shape.append(jax.ShapeDtypeStruct((b, N_KV_HEADS, n, LANES), BF16))
    out_specs.append(pl.BlockSpec((1, N_KV_HEADS, tm, LANES), lambda bi, i: (bi, 0, i, 0)))
    return pl.pallas_call(
        functools.partial(_qkv_kernel, with_q=with_q),
        grid=(b, n // tm),
        in_specs=[
            pl.BlockSpec((1, tm, d), lambda bi, i: (bi, i, 0)),
            pl.BlockSpec((1, 1, d), mod_idx),
            pl.BlockSpec((1, 1, d), mod_idx),
            pl.BlockSpec((1, d), lambda bi, i: (0, 0)),
            pl.BlockSpec((d, ncol), lambda bi, i: (0, 0)),
            pl.BlockSpec((2, LANES), lambda bi, i: (0, 0)),
            pl.BlockSpec((LANES, LANES), lambda bi, i: (0, 0)),
            pl.BlockSpec((tm, LANES), lambda bi, i: (i, 0)),
            pl.BlockSpec((tm, LANES), lambda bi, i: (i, 0)),
        ],
        out_specs=out_specs,
        out_shape=out_shape,
        compiler_params=_params(("arbitrary", "arbitrary")),
        name="qkv" if with_q else "ctx_kv",
    )(x, shift, scale, gain, w, head_gain, ebd, cos, sin)


def _attn_kernel(q_ref, k_ref, v_ref, kc_ref, vc_ref, o_ref, *, tq, tk, n_sub):
    chunks = []
    for kr, vr in ((k_ref, v_ref), (kc_ref, vc_ref)):
        size = min(tk, kr.shape[2])
        chunks += [(kr, vr, c, size) for c in range(kr.shape[2] // size)]
    sub = tq // n_sub
    m_rows = GQA_GROUP * sub
    for u in range(n_sub):
        q = q_ref[0, :, u * sub:(u + 1) * sub, :].reshape(m_rows, HEAD_DIM)
        m = jnp.full((m_rows, 1), -jnp.inf, F32)
        acc = jnp.zeros((m_rows, LANES), F32)
        for kr, vr, c, size in chunks:
            kc = kr[0, 0, c * size:(c + 1) * size, :]
            vc = vr[0, 0, c * size:(c + 1) * size, :]
            s = lax.dot_general(q, kc, (((1,), (1,)), ((), ())), preferred_element_type=F32)
            m_new = jnp.maximum(m, jnp.max(s, axis=-1, keepdims=True))
            alpha = jnp.exp2(m - m_new)
            p = jnp.exp2(s - m_new)
            acc = alpha * acc + _dot(p.astype(BF16), vc)
            m = m_new
        o = acc * (1.0 / pltpu.roll(acc, HEAD_DIM, 1))
        o_ref[0, u * sub:(u + 1) * sub, :] = jnp.concatenate(
            [o[g * sub:(g + 1) * sub, :HEAD_DIM] for g in range(GQA_GROUP)], axis=-1).astype(BF16)


def _attention(q, k, v, k_ctx, v_ctx):
    b, _, n, _ = q.shape
    n_ctx = k_ctx.shape[2]
    tq = min(512, n)
    tk = 256
    assert n % min(tk, n) == 0 and n_ctx % min(tk, n_ctx) == 0
    kv_spec = lambda rows, width: pl.BlockSpec((1, 1, rows, width), lambda bi, kh, qi: (bi, kh, 0, 0))
    return pl.pallas_call(
        functools.partial(_attn_kernel, tq=tq, tk=tk, n_sub=tq // min(256, tq)),
        grid=(b, N_KV_HEADS, n // tq),
        in_specs=[
            pl.BlockSpec((1, GQA_GROUP, tq, HEAD_DIM), lambda bi, kh, qi: (bi, kh, qi, 0)),
            kv_spec(n, HEAD_DIM), kv_spec(n, LANES), kv_spec(n_ctx, HEAD_DIM), kv_spec(n_ctx, LANES),
        ],
        out_specs=pl.BlockSpec((1, tq, GQA_GROUP * HEAD_DIM), lambda bi, kh, qi: (bi, qi, kh)),
        out_shape=jax.ShapeDtypeStruct((b, n, N_HEADS * HEAD_DIM), BF16),
        compiler_params=_params(("arbitrary", "arbitrary", "arbitrary")),
        name="attn",
    )(q, k, v, k_ctx, v_ctx)


def _proj_moe_kernel(*refs, tiles_per_batch, folded):
    if folded:
        dif_ref, sum_ref, sumb_ref, *refs, a_scr = refs
        j = pl.program_id(0) % tiles_per_batch

        @pl.when(j < tiles_per_batch // 2)
        def _():
            a_scr[...] = dif_ref[0]

        @pl.when(j >= tiles_per_batch // 2)
        def _():
            rows = _dot(_reversal(a_scr.shape[0]), sum_ref[0])
            row = lax.broadcasted_iota(jnp.int32, rows.shape, 0)
            a_scr[...] = jnp.where(row == 0, sumb_ref[0, 0:1, :].astype(F32), rows).astype(BF16)

        a = a_scr[...]
    else:
        a_ref, *refs = refs
        a = a_ref[...]
    (wo_ref, x_ref, g1_ref, gain_ref, sh_ref, sc_ref, wrh_ref, wrl_ref, br_ref,
     x1_ref, s_ref, gp_ref, pc_ref) = refs
    x1 = x_ref[...] + g1_ref[0] * _dot(a, wo_ref[...])
    x1_ref[...] = x1
    h2 = _modulate(x1, gain_ref[...], sh_ref[0], sc_ref[0])
    h_hi, h_lo = _split_bf16(h2)
    r_hi = _dot(h_hi, wrl_ref[...])
    logits = (r_hi[:, :LANES] + _dot(h_lo, wrh_ref[...]) + r_hi[:, LANES:]
              + br_ref[...])
    tm = logits.shape[0]
    lane = lax.broadcasted_iota(jnp.int32, (tm, LANES), 1)
    lane_f = lane.astype(F32)

    def top(vals):
        best = jnp.max(vals, axis=-1, keepdims=True)
        idx = jnp.min(jnp.where(vals == best, lane_f, float(LANES)), axis=-1, keepdims=True)
        return best, idx

    lg = jnp.where(lane < N_GROUPS, logits, NEG_BIG)
    g_max, g_sel = top(lg)
    p_grp = 1.0 / jnp.sum(jnp.where(lane < N_GROUPS, jnp.exp(lg - g_max), 0.0), axis=-1, keepdims=True)
    lo = N_GROUPS + EXPERTS_PER_GROUP * g_sel
    le = jnp.where((lane_f >= lo) & (lane_f < lo + EXPERTS_PER_GROUP), logits, NEG_BIG)
    t1, i1 = top(le)
    t2, i2 = top(jnp.where(lane_f == i1, NEG_BIG, le))
    e21 = jnp.exp(t2 - t1)
    w1 = 1.0 / (1.0 + e21)
    gate0 = p_grp * w1
    gate1 = p_grp * (e21 * w1)
    e0 = i1 - N_GROUPS
    e1 = i2 - N_GROUPS

    onehot = jnp.where((lane_f == e0) | (lane_f == e1), 1.0, 0.0)
    row = lax.broadcasted_iota(jnp.int32, (tm, tm), 0)
    col = lax.broadcasted_iota(jnp.int32, (tm, tm), 1)
    lower = jnp.where(col < row, 1.0, 0.0).astype(BF16)
    before = _dot(lower, onehot.astype(BF16))
    cnt = jnp.sum(onehot, axis=0, keepdims=True)
    pc = jnp.floor((cnt + (RUN_ALIGN - 1)) * (1.0 / RUN_ALIGN))
    urow = lax.broadcasted_iota(jnp.int32, (LANES, LANES), 0)
    ucol = lax.broadcasted_iota(jnp.int32, (LANES, LANES), 1)
    upper = jnp.where(urow < ucol, 1.0, 0.0).astype(BF16)
    run_start = _dot(jnp.broadcast_to(pc, (8, LANES)).astype(BF16), upper)[0:1] * RUN_ALIGN
    base = run_start + before
    pos0 = jnp.sum(jnp.where(lane_f == e0, base, 0.0), axis=-1, keepdims=True)
    pos1 = jnp.sum(jnp.where(lane_f == e1, base, 0.0), axis=-1, keepdims=True)
    gp = jnp.where(lane == 0, gate0, jnp.where(lane == 1, gate1, jnp.where(lane == 2, pos0,
                                                                       jnp.where(lane == 3, pos1, 0.0))))
    gp_ref[...] = gp
    pc_ref[...] = jnp.broadcast_to(pc, (8, LANES)).astype(jnp.int32)

    pos_t = jnp.where(lane == 0, pos0, jnp.where(lane == 1, pos1, -1.0)).T
    sr = s_ref.shape[0]
    srow = lax.broadcasted_iota(jnp.int32, (sr, tm), 0).astype(F32)
    perm = jnp.where((srow == pos_t[0:1, :]) | (srow == pos_t[1:2, :]), 1.0, 0.0).astype(BF16)
    s_ref[...] = _pack_halves(_dot(perm, h_hi))


def _proj_moe(a, w_o, x, g1, gain, shift, scale, wr_hi, wr_lo, b_r, *, tokens_per_batch):
    t, d = x.shape
    tm = min(MOE_TILE, tokens_per_batch)
    nt = t // tm
    sr = _sorted_rows(tm)
    per_b = tokens_per_batch // tm
    mod_idx = lambda i: (i // per_b, 0, 0)
    row_spec = lambda w: pl.BlockSpec((tm, w), lambda i: (i, 0))
    full = lambda s: pl.BlockSpec(s, lambda i: (0,) * len(s))
    folded = isinstance(a, tuple)
    if folded:
        assert per_b % 2 == 0
        half = per_b // 2
        a_inputs = (a[0], a[1], a[1])
        a_specs = [
            pl.BlockSpec((1, tm, d), lambda i: (i // per_b, jnp.minimum(i % per_b, half - 1), 0)),
            pl.BlockSpec((1, tm, d), lambda i: (i // per_b, jnp.clip(per_b - 1 - i % per_b, 0, half - 1), 0)),
            pl.BlockSpec((1, 8, d), lambda i: (i // per_b, (tm // 8) * jnp.clip(per_b - i % per_b, 1, half), 0)),
        ]
        scratch = [pltpu.VMEM((tm, d), BF16)]
    else:
        a_inputs = (a,)
        a_specs = [row_spec(d)]
        scratch = []
    return pl.pallas_call(
        functools.partial(_proj_moe_kernel, tiles_per_batch=per_b, folded=folded),
        grid=(nt,),
        in_specs=a_specs + [
            full((d, d)), row_spec(d),
            pl.BlockSpec((1, 1, d), mod_idx), full((1, d)),
            pl.BlockSpec((1, 1, d), mod_idx), pl.BlockSpec((1, 1, d), mod_idx),
            full((d, LANES)), full((d, 2 * LANES)), full((1, LANES)),
        ],
        out_specs=[row_spec(d), pl.BlockSpec((sr, d // 2), lambda i: (i, 0)), row_spec(LANES),
                   pl.BlockSpec((8, LANES), lambda i: (i, 0))],
        out_shape=[
            jax.ShapeDtypeStruct((t, d), F32), jax.ShapeDtypeStruct((nt * sr, d // 2), jnp.uint32),
            jax.ShapeDtypeStruct((t, LANES), F32), jax.ShapeDtypeStruct((nt * 8, LANES), jnp.int32),
        ],
        scratch_shapes=scratch,
        compiler_params=_params(("arbitrary",)),
        name="proj_moe",
    )(*a_inputs, w_o, x, g1, gain, shift, scale, wr_hi, wr_lo, b_r)


def _rows(ref, start_units, n_units):
    return ref.at[pl.ds(pl.multiple_of(start_units * RUN_ALIGN, RUN_ALIGN), n_units * RUN_ALIGN)]


def _ffn_kernel(exp_nblk, exp_blk0, blk_e, blk_b0, blk_n, blk_tlo, blk_thi, seg_start, run_start,
                nused_ref, s_hbm, w1_ref, w3_ref, w2_ref, y_hbm, xg, yo, w1b, w3b, w2b, sem, osem,
                *, sr_units, nblk):
    e_step = pl.program_id(0)
    nused = nused_ref[0]
    n_slots, bm, _ = xg.shape
    n_ahead = n_slots - 1
    bm_units = bm // RUN_ALIGN

    def gather(b, slot):
        e = blk_e[b]
        b0 = blk_b0[b]

        def body(ti, carry):
            rs = seg_start[ti * N_EXPERTS + e]
            re = seg_start[(ti + 1) * N_EXPERTS + e]
            lo = jnp.maximum(rs, b0)
            n = jnp.minimum(re, b0 + bm_units) - lo

            @pl.when(n > 0)
            def _():
                src = ti * sr_units + run_start[ti * N_EXPERTS + e] + (lo - rs)
                pltpu.make_async_copy(_rows(s_hbm, src, n), _rows(xg.at[slot], lo - b0, n),
                                      sem.at[slot]).start()

            return carry

        lax.fori_loop(blk_tlo[b], blk_thi[b], body, 0)

    def write_back(g, slot):
        return pltpu.make_async_copy(yo.at[slot], y_hbm.at[pl.ds(pl.multiple_of(g * bm, bm), bm)],
                                     osem.at[slot])

    def emit(g, fill):
        slot = g % 2

        @pl.when(g >= 2)
        def _():
            write_back(g - 2, slot).wait()

        fill(slot)
        write_back(g, slot).start()

    @pl.when(e_step == 0)
    def _():
        xg[...] = jnp.zeros_like(xg)
        for g0 in range(n_ahead):

            @pl.when(g0 < nused)
            def _():
                gather(g0, g0)

    w1b[...] = w1_ref[0, 0].astype(BF16)
    w3b[...] = w3_ref[0, 0].astype(BF16)
    w2b[...] = w2_ref[0, 0].astype(BF16)

    def block(jb, carry):
        g = exp_blk0[e_step] + jb

        @pl.when(g + n_ahead < nused)
        def _():
            gather(g + n_ahead, (g + n_ahead) % n_slots)

        def fill(slot):
            gslot = g % n_slots
            pltpu.make_async_copy(_rows(s_hbm, 0, blk_n[g]), _rows(xg.at[gslot], 0, blk_n[g]),
                                  sem.at[gslot]).wait()
            x_lo, x_hi = _unpack_halves(xg[gslot])
            half = x_lo.shape[1]
            a = _dot(x_lo, w1b[:half, :]) + _dot(x_hi, w1b[half:, :])
            b = _dot(x_lo, w3b[:half, :]) + _dot(x_hi, w3b[half:, :])
            mid = (a / (1.0 + jnp.exp(-a))) * b
            y = _dot(mid.astype(BF16), w2b[...])
            yo[slot] = _pack_halves(y.astype(BF16).astype(F32))

        emit(g, fill)
        return carry

    lax.fori_loop(0, exp_nblk[e_step], block, 0)

    @pl.when(e_step == pl.num_programs(0) - 1)
    def _():
        def tail(g, carry):
            def fill(slot):
                yo[slot] = jnp.zeros(yo.shape[1:], yo.dtype)

            emit(g, fill)
            return carry

        lax.fori_loop(nused, nblk, tail, 0)
        write_back(nblk - 2, nblk % 2).wait()
        write_back(nblk - 1, (nblk - 1) % 2).wait()


def _ffn(tables, s_tiles, w1, w3, w2, *, layer, nblk, sr):
    bm = FFN_BLOCK
    d = w1.shape[2]
    de = w1.shape[3]
    nsp = len(tables)
    assert nblk >= 2
    grid_spec = pltpu.PrefetchScalarGridSpec(
        num_scalar_prefetch=nsp,
        grid=(N_EXPERTS,),
        in_specs=[
            pl.BlockSpec(memory_space=pl.ANY),
            pl.BlockSpec((1, 1, d, de), lambda e, *_: (layer, e, 0, 0)),
            pl.BlockSpec((1, 1, d, de), lambda e, *_: (layer, e, 0, 0)),
            pl.BlockSpec((1, 1, de, d), lambda e, *_: (layer, e, 0, 0)),
        ],
        out_specs=pl.BlockSpec(memory_space=pl.ANY),
        scratch_shapes=[pltpu.VMEM((FFN_GATHER_SLOTS, bm, d // 2), jnp.uint32),
                        pltpu.VMEM((2, bm, d // 2), jnp.uint32),
                        pltpu.VMEM((d, de), BF16), pltpu.VMEM((d, de), BF16), pltpu.VMEM((de, d), BF16),
                        pltpu.SemaphoreType.DMA((FFN_GATHER_SLOTS,)), pltpu.SemaphoreType.DMA((2,))],
    )
    return pl.pallas_call(
        functools.partial(_ffn_kernel, sr_units=sr // RUN_ALIGN, nblk=nblk),
        grid_spec=grid_spec,
        out_shape=jax.ShapeDtypeStruct((nblk * bm, d // 2), jnp.uint32),
        compiler_params=_params(("arbitrary",)),
        name="ffn",
    )(*tables, s_tiles, w1, w3, w2)


def _combine_kernel(run_len, run_start, seg_start, exp_start, tile_rows,
                    y_hbm, x_ref, g2_ref, gp_ref, fgain_ref, o_ref, ybuf, sem, *, nsteps, final):
    i = pl.program_id(0)

    def gather(ti, slot):
        for e in range(N_EXPERTS):
            n = run_len[ti * N_EXPERTS + e]

            @pl.when(n > 0)
            def _():
                src = exp_start[e] + seg_start[ti * N_EXPERTS + e]
                pltpu.make_async_copy(_rows(y_hbm, src, n),
                                      _rows(ybuf.at[slot], run_start[ti * N_EXPERTS + e], n),
                                      sem.at[slot]).start()

    @pl.when(i == 0)
    def _():
        ybuf[...] = jnp.zeros_like(ybuf)
        gather(0, 0)

    @pl.when(i + 1 < nsteps)
    def _():
        gather(i + 1, (i + 1) % 2)

    slot = i % 2
    pltpu.make_async_copy(_rows(y_hbm, 0, tile_rows[i]), _rows(ybuf.at[slot], 0, tile_rows[i]),
                          sem.at[slot]).wait()
    gp = gp_ref[...]
    tm = gp.shape[0]
    sr = ybuf.shape[1]
    col = lax.broadcasted_iota(jnp.int32, (tm, sr), 1).astype(F32)
    weights = (jnp.where(col == gp[:, 2:3], gp[:, 0:1], 0.0)
               + jnp.where(col == gp[:, 3:4], gp[:, 1:2], 0.0)).astype(BF16)
    y_lo, y_hi = _unpack_halves(ybuf[slot])
    y = jnp.concatenate([_dot(weights, y_lo), _dot(weights, y_hi)], axis=-1)
    x2 = x_ref[...] + g2_ref[0] * y
    if final:
        ms = jnp.mean(x2 * x2, axis=-1, keepdims=True)
        x2 = (x2 * lax.rsqrt(ms + EPS)) * fgain_ref[...]
    o_ref[...] = x2


def _combine(tables, yb, x1, g2, gp, fgain, *, tokens_per_batch, final):
    t, d = x1.shape
    tm = min(MOE_TILE, tokens_per_batch)
    nsteps = t // tm
    per_b = tokens_per_batch // tm
    sr = _sorted_rows(tm)
    grid_spec = pltpu.PrefetchScalarGridSpec(
        num_scalar_prefetch=len(tables),
        grid=(nsteps,),
        in_specs=[
            pl.BlockSpec(memory_space=pl.ANY),
            pl.BlockSpec((tm, d), lambda i, *_: (i, 0)),
            pl.BlockSpec((1, 1, d), lambda i, *_: (i // per_b, 0, 0)),
            pl.BlockSpec((tm, LANES), lambda i, *_: (i, 0)),
            pl.BlockSpec((1, d), lambda i, *_: (0, 0)),
        ],
        out_specs=pl.BlockSpec((tm, d), lambda i, *_: (i, 0)),
        scratch_shapes=[pltpu.VMEM((2, sr, d // 2), jnp.uint32), pltpu.SemaphoreType.DMA((2,))],
    )
    return pl.pallas_call(
        functools.partial(_combine_kernel, nsteps=nsteps, final=final),
        grid_spec=grid_spec,
        out_shape=jax.ShapeDtypeStruct((t, d), F32),
        compiler_params=_params(("arbitrary",)),
        name="combine_final" if final else "combine",
    )(*tables, yb, x1, g2, gp, fgain)


def _reversal(size):
    row = lax.broadcasted_iota(jnp.int32, (size, size), 0)
    col = lax.broadcasted_iota(jnp.int32, (size, size), 1)
    return jnp.where(col == size - row, 1.0, 0.0).astype(BF16)


def _fmix_fold_kernel(x_ref, xm_ref, xb_ref, xc_ref, sh_ref, sc_ref, gain_ref, cc_ref, sc_mat_ref,
                      a_ref, b_ref, amid_ref):
    mod = lambda v: _modulate(v, gain_ref[...], sh_ref[0], sc_ref[0])
    h = mod(x_ref[...])
    ft = h.shape[0]
    m_hi, m_lo = _split_bf16(mod(xm_ref[...]))
    rev = _reversal(ft)
    row = lax.broadcasted_iota(jnp.int32, h.shape, 0)
    hr = jnp.where(row == 0, mod(xb_ref[...])[0:1], _dot(rev, m_hi) + _dot(rev, m_lo))
    he = (h + hr).astype(BF16)
    ho = (h - hr).astype(BF16)
    hc = mod(xc_ref[...]).astype(BF16)
    gw = cc_ref.shape[0]
    for g in range(FOURIER_GROUPS):
        cols = slice(g * gw, (g + 1) * gw)
        a_ref[:, cols] = _dot(he[:, cols], cc_ref[...]).astype(BF16)
        b_ref[:, cols] = _dot(ho[:, cols], sc_mat_ref[...]).astype(BF16)
        amid_ref[:, cols] = _dot(hc[:, cols], cc_ref[...])


def _fmix_fold(x, shift, scale, gain, cc, sc, *, bsz, n):
    t, d = x.shape
    hn = n // 2
    ft = min(512, hn)
    tpb = n // ft
    half = tpb // 2
    gw = d // FOURIER_GROUPS
    mod_idx = lambda b, j: (b, 0, 0)
    return pl.pallas_call(
        _fmix_fold_kernel,
        grid=(bsz, half),
        in_specs=[
            pl.BlockSpec((ft, d), lambda b, j: (b * tpb + j, 0)),
            pl.BlockSpec((ft, d), lambda b, j: (b * tpb + tpb - 1 - j, 0)),
            pl.BlockSpec((8, d), lambda b, j: (b * (n // 8) + (ft // 8) * ((tpb - j) % tpb), 0)),
            pl.BlockSpec((8, d), lambda b, j: (b * (n // 8) + hn // 8, 0)),
            pl.BlockSpec((1, 1, d), mod_idx), pl.BlockSpec((1, 1, d), mod_idx),
            pl.BlockSpec((1, d), lambda b, j: (0, 0)),
            pl.BlockSpec((gw, gw), lambda b, j: (0, 0)), pl.BlockSpec((gw, gw), lambda b, j: (0, 0)),
        ],
        out_specs=[pl.BlockSpec((ft, d), lambda b, j: (b * half + j, 0)),
                   pl.BlockSpec((ft, d), lambda b, j: (b * half + j, 0)),
                   pl.BlockSpec((8, d), lambda b, j: (b, 0))],
        out_shape=[jax.ShapeDtypeStruct((bsz * hn, d), BF16), jax.ShapeDtypeStruct((bsz * hn, d), BF16),
                   jax.ShapeDtypeStruct((bsz * 8, d), F32)],
        compiler_params=_params(("arbitrary", "arbitrary")),
        name="fmix_fold",
    )(x, x, x, x, shift, scale, gain, cc, sc)


def _fmix_pos_kernel(ch_ref, sh_ref, sign_ref, a_ref, b_ref, amid_ref, dif_ref, sum_ref):
    p = _dot(ch_ref[...], a_ref[0]) + sign_ref[...] * amid_ref[0, 0:1, :]
    q = _dot(sh_ref[...], b_ref[0])
    dif_ref[0] = (p - q).astype(BF16)
    sum_ref[0] = (p + q).astype(BF16)


def _fmix_pos(ch, sh, sign, a, b, amid):
    bsz, hn, d = a.shape
    mp = ch.shape[0]
    tmr = mp // DFT_ROW_BLOCKS
    assert tmr % 16 == 0
    out = jax.ShapeDtypeStruct((bsz, mp, d), BF16)
    return pl.pallas_call(
        _fmix_pos_kernel,
        grid=(bsz, DFT_ROW_BLOCKS),
        in_specs=[
            pl.BlockSpec((tmr, hn), lambda bi, i: (i, 0)),
            pl.BlockSpec((tmr, hn), lambda bi, i: (i, 0)),
            pl.BlockSpec((tmr, 1), lambda bi, i: (i, 0)),
            pl.BlockSpec((1, hn, d), lambda bi, i: (bi, 0, 0)),
            pl.BlockSpec((1, hn, d), lambda bi, i: (bi, 0, 0)),
            pl.BlockSpec((1, 8, d), lambda bi, i: (bi, 0, 0)),
        ],
        out_specs=[pl.BlockSpec((1, tmr, d), lambda bi, i: (bi, i, 0)),
                   pl.BlockSpec((1, tmr, d), lambda bi, i: (bi, i, 0))],
        out_shape=[out, out],
        compiler_params=_params(("arbitrary", "arbitrary")),
        name="fmix_pos",
    )(ch, sh, sign, a, b, amid)


def _rope_tables(n):
    half = ROPE_AXIS_DIM // 2
    rows = n // GRID_W
    inv = ROPE_THETA ** (-jnp.arange(half, dtype=F32) / half)
    pos = jnp.arange(n, dtype=jnp.int32)
    ang_r = (pos // GRID_W).astype(F32)[:, None] * inv
    ang_c = (pos % GRID_W).astype(F32)[:, None] * inv
    del rows
    cos = jnp.concatenate([jnp.cos(ang_r)] * 2 + [jnp.cos(ang_c)] * 2, axis=-1)
    sin = jnp.concatenate([-jnp.sin(ang_r), jnp.sin(ang_r), -jnp.sin(ang_c), jnp.sin(ang_c)], axis=-1)
    return jnp.tile(cos, (1, LANES // HEAD_DIM)), jnp.tile(sin, (1, LANES // HEAD_DIM))


def _dft_mats(n, scale):
    r = 1
    while r * r < n:
        r *= 2
    c = n // r
    k = jnp.arange(n, dtype=jnp.int32)[:, None]
    ang_a = ((k * jnp.arange(r, dtype=jnp.int32)[None, :] * c) % n).astype(F32) * (2.0 * math.pi / n)
    ang_b = ((k * jnp.arange(c, dtype=jnp.int32)[None, :]) % n).astype(F32) * (2.0 * math.pi / n)
    ca, sa, cb, sb = jnp.cos(ang_a), jnp.sin(ang_a), jnp.cos(ang_b), jnp.sin(ang_b)
    cosm = (ca[:, :, None] * cb[:, None, :] - sa[:, :, None] * sb[:, None, :]).reshape(n, n)
    sinm = (sa[:, :, None] * cb[:, None, :] + ca[:, :, None] * sb[:, None, :]).reshape(n, n)
    return (cosm * scale).astype(BF16), (sinm * scale).astype(BF16)


DFT_ROW_BLOCKS = 4
DFT_ROW_PAD = 64


def _dft_half(n):
    hn = n // 2
    mp = hn + DFT_ROW_PAD
    kb = 32
    ka = mp // kb
    assert ka * kb == mp
    m = jnp.arange(hn, dtype=jnp.int32)[None, :]
    w = 2.0 * math.pi / n
    ang_a = (((jnp.arange(ka, dtype=jnp.int32)[:, None] * kb) * m) % n).astype(F32) * w
    ang_b = ((jnp.arange(kb, dtype=jnp.int32)[:, None] * m) % n).astype(F32) * w
    ca, sa, cb, sb = jnp.cos(ang_a), jnp.sin(ang_a), jnp.cos(ang_b), jnp.sin(ang_b)
    cosm = (ca[:, None, :] * cb[None, :, :] - sa[:, None, :] * sb[None, :, :]).reshape(mp, hn)
    sinm = (sa[:, None, :] * cb[None, :, :] + ca[:, None, :] * sb[None, :, :]).reshape(mp, hn)
    k = jnp.arange(mp, dtype=jnp.int32)[:, None]
    valid = k <= hn
    scale = n ** -0.5
    col_w = jnp.where(m == 0, 0.5 * scale, scale)
    cosm = jnp.where(valid, cosm * col_w, 0.0).astype(BF16)
    sinm = jnp.where(valid, sinm * scale, 0.0).astype(BF16)
    sign = jnp.where(valid, jnp.where(k % 2 == 0, scale, -scale), 0.0).astype(F32)
    return cosm, sinm, sign


def _moe(a, w_o, x, g1, g2, gain, shift, scale, w_rg, b_rg, w_re, b_re, w1, w3, w2, fgain, *,
         layer, tokens_per_batch, final):
    t, d = x.shape
    pad = LANES - N_GROUPS - N_EXPERTS
    w_r = jnp.concatenate([w_rg, w_re, jnp.zeros((d, pad), F32)], axis=1)
    b_r = jnp.concatenate([b_rg, b_re, jnp.zeros((pad,), F32)])[None, :]
    wr_hi = w_r.astype(BF16)
    wr_lo = jnp.concatenate([wr_hi, (w_r - wr_hi.astype(F32)).astype(BF16)], axis=1)
    x1, s_tiles, gp, pc = _proj_moe(a, w_o, x, g1, gain, shift, scale, wr_hi, wr_lo, b_r,
                                    tokens_per_batch=tokens_per_batch)
    tm = min(MOE_TILE, tokens_per_batch)
    nt = t // tm
    sr = _sorted_rows(tm)
    bmu = FFN_BLOCK // RUN_ALIGN
    i32 = jnp.int32
    run_len = pc.reshape(nt, 8, LANES)[:, 0, :N_EXPERTS]
    seg_start = jnp.concatenate([jnp.zeros((1, N_EXPERTS), i32), jnp.cumsum(run_len, axis=0)])
    seg_len = seg_start[-1]
    seg_pad = (seg_len + bmu - 1) // bmu * bmu
    exp_end = jnp.cumsum(seg_pad)
    exp_start = exp_end - seg_pad
    run_start = jnp.cumsum(run_len, axis=1) - run_len
    tile_rows = run_start[:, -1] + run_len[:, -1]
    max_units = (2 * t) // RUN_ALIGN + nt * N_EXPERTS + N_EXPERTS * (bmu - 1)
    nblk = (max_units + bmu - 1) // bmu
    blk_start = jnp.arange(nblk, dtype=i32) * bmu
    blk_e = jnp.minimum(jnp.sum((exp_end[None, :] <= blk_start[:, None]).astype(i32), axis=1),
                        N_EXPERTS - 1)
    pick = (blk_e[:, None] == jnp.arange(N_EXPERTS, dtype=i32)[None, :]).astype(i32)
    blk_b0 = blk_start - jnp.sum(pick * exp_start[None, :], axis=1)
    blk_n = jnp.clip(jnp.sum(pick * seg_len[None, :], axis=1) - blk_b0, 0, bmu)
    seg_e = jnp.sum(pick[:, None, :] * seg_start[None, :, :], axis=2)
    blk_tlo = jnp.sum((seg_e[:, 1:] <= blk_b0[:, None]).astype(i32), axis=1)
    blk_thi = jnp.sum((seg_e[:, :-1] < (blk_b0 + bmu)[:, None]).astype(i32), axis=1)
    nused = (exp_end[-1] // bmu)[None]
    flat = lambda v: v.reshape(-1).astype(i32)
    yb = _ffn((seg_pad // bmu, exp_start // bmu, blk_e, blk_b0, blk_n, blk_tlo, blk_thi,
               flat(seg_start), flat(run_start), nused),
              s_tiles, w1, w3, w2, layer=layer, nblk=nblk, sr=sr)
    return _combine((flat(run_len), flat(run_start), flat(seg_start), exp_start, tile_rows),
                    yb, x1, g2, gp, fgain, tokens_per_batch=tokens_per_batch, final=final)


def kernel(x, c, ctx, c_ctx, w_mod, b_mod, norm_mix, norm_ffn, attn_w_qkv, attn_q_norm, attn_k_norm,
           attn_w_o, fourier_w_o, moe_w_rg, moe_b_rg, moe_w_re, moe_b_re, moe_w1, moe_w3, moe_w2,
           final_norm):
    bsz, n, d = x.shape
    t = bsz * n
    q_dim = N_HEADS * HEAD_DIM
    kv_dim = N_KV_HEADS * HEAD_DIM

    cond = jnp.zeros((16, d), F32).at[:bsz].set(c).at[bsz].set(c_ctx)
    mods = _ada(cond, w_mod, b_mod)

    def lat(layer, j):
        return mods[layer, :bsz, j * d:(j + 1) * d][:, None, :]

    def ctxm(layer, j):
        return mods[layer, bsz:bsz + 1, j * d:(j + 1) * d][:, None, :]

    w_qkv = attn_w_qkv[0].astype(BF16)
    qscale = (HEAD_DIM ** -0.5) * math.log2(math.e)
    head_gain = jnp.stack([jnp.tile(attn_q_norm[0] * qscale, LANES // HEAD_DIM),
                           jnp.tile(attn_k_norm[0], LANES // HEAD_DIM)])
    lane = jnp.arange(LANES)
    ebd = jnp.where((lane[:, None] // HEAD_DIM) == (lane[None, :] // HEAD_DIM),
                    1.0 / HEAD_DIM, 0.0).astype(BF16)
    cos, sin = _rope_tables(n)
    gain0 = norm_mix[0][None, :]
    q, k_l, v_l = _qkv(x, lat(0, 0), lat(0, 1), gain0, w_qkv, head_gain, ebd, cos, sin,
                       with_q=True, per_batch_mods=True)
    n_ctx = ctx.shape[1]
    k_c, v_c = _qkv(ctx, ctxm(0, 0), ctxm(0, 1), gain0, w_qkv[:, q_dim:], head_gain, ebd,
                    jnp.ones((n_ctx, LANES), F32), jnp.zeros((n_ctx, LANES), F32),
                    with_q=False, per_batch_mods=False)
    del kv_dim
    o = _attention(q, k_l, v_l, k_c, v_c).reshape(t, q_dim)

    xf = x.reshape(t, d)
    moe_w = lambda i: (moe_w_rg[i], moe_b_rg[i], moe_w_re[i], moe_b_re[i],
                       moe_w1, moe_w3, moe_w2)
    fgain = final_norm[None, :]
    x2 = _moe(o, attn_w_o[0].astype(BF16), xf, lat(0, 2), lat(0, 5), norm_ffn[0][None, :],
              lat(0, 3), lat(0, 4), *moe_w(0), fgain, layer=0, tokens_per_batch=n, final=False)

    gw = d // FOURIER_GROUPS
    cc, sc = _dft_mats(gw, gw ** -0.5)
    ch, sh, sign = _dft_half(n)
    a_e, b_o, a_mid = _fmix_fold(x2, lat(1, 0), lat(1, 1), norm_mix[1][None, :], cc, sc, bsz=bsz, n=n)
    f = tuple(_fmix_pos(ch, sh, sign, a_e.reshape(bsz, n // 2, d), b_o.reshape(bsz, n // 2, d),
                        a_mid.reshape(bsz, 8, d)))
    out = _moe(f, fourier_w_o[0].astype(BF16), x2, lat(1, 2), lat(1, 5), norm_ffn[1][None, :],
               lat(1, 3), lat(1, 4), *moe_w(1), fgain, layer=1, tokens_per_batch=n, final=True)
    return out.reshape(bsz, n, d)
```

```python
import functools
import math

import jax
import jax.numpy as jnp
from jax import lax
from jax.experimental import pallas as pl
from jax.experimental.pallas import tpu as pltpu

F32 = jnp.float32
BF16 = jnp.bfloat16

N_HEADS = 16
N_KV_HEADS = 4
HEAD_DIM = 64
GQA_GROUP = N_HEADS // N_KV_HEADS
GRID_W = 64
ROPE_AXIS_DIM = HEAD_DIM // 2
ROPE_THETA = 10000.0
FOURIER_GROUPS = 4
N_GROUPS = 4
EXPERTS_PER_GROUP = 8
N_EXPERTS = N_GROUPS * EXPERTS_PER_GROUP
EPS = 1e-6
LANES = 128
NEG_BIG = -1e30
VMEM_LIMIT = 48 * 1024 * 1024

FFN_BLOCK = 512
FFN_GATHER_SLOTS = 4
MOE_TILE = 512
RUN_ALIGN = 8


def _sorted_rows(tile_tokens):
    return 2 * tile_tokens + N_EXPERTS * RUN_ALIGN


def _dot(a, b):
    return jnp.dot(a, b, preferred_element_type=F32)


def _split_bf16(a):
    hi = a.astype(BF16)
    lo = (a - hi.astype(F32)).astype(BF16)
    return hi, lo


_HI16 = 0xFFFF0000


def _pack_halves(v):
    w = v.shape[1] // 2
    lo = lax.bitcast_convert_type(v[:, :w], jnp.uint32) >> 16
    hi = lax.bitcast_convert_type(v[:, w:], jnp.uint32) & jnp.uint32(_HI16)
    return hi | lo


def _unpack_halves(p):
    lo = lax.bitcast_convert_type(p << 16, F32).astype(BF16)
    hi = lax.bitcast_convert_type(p & jnp.uint32(_HI16), F32).astype(BF16)
    return lo, hi


def _params(sem):
    return pltpu.CompilerParams(dimension_semantics=sem, vmem_limit_bytes=VMEM_LIMIT)


def _ada_kernel(c_ref, w_ref, b_ref, o_ref):
    c = c_ref[...]
    s = c / (1.0 + jnp.exp(-c))
    s_hi, s_lo = _split_bf16(s)
    w_hi, w_lo = _split_bf16(w_ref[0])
    o_ref[0] = _dot(s_hi, w_hi) + _dot(s_hi, w_lo) + _dot(s_lo, w_hi) + b_ref[0]


def _ada(cond, w_mod, b_mod):
    depth, d, d6 = w_mod.shape
    rows = cond.shape[0]
    tn = 1536
    return pl.pallas_call(
        _ada_kernel,
        grid=(depth, d6 // tn),
        in_specs=[
            pl.BlockSpec((rows, d), lambda l, j: (0, 0)),
            pl.BlockSpec((1, d, tn), lambda l, j: (l, 0, j)),
            pl.BlockSpec((1, 1, tn), lambda l, j: (l, 0, j)),
        ],
        out_specs=pl.BlockSpec((1, rows, tn), lambda l, j: (l, 0, j)),
        out_shape=jax.ShapeDtypeStruct((depth, rows, d6), F32),
        compiler_params=_params(("arbitrary", "arbitrary")),
        name="ada",
    )(cond, w_mod, b_mod.reshape(depth, 1, d6))


def _modulate(xf, gain, shift, scale):
    ms = jnp.mean(xf * xf, axis=-1, keepdims=True)
    return (xf * lax.rsqrt(ms + EPS)) * (gain * (1.0 + scale)) + shift


def _qkv_kernel(x_ref, sh_ref, sc_ref, gain_ref, w_ref, hg_ref, ebd_ref, cos_ref, sin_ref,
                *out_refs, with_q):
    h = _modulate(x_ref[0], gain_ref[...], sh_ref[0], sc_ref[0])
    qkv = _dot(h.astype(BF16), w_ref[...])
    tm = qkv.shape[0]
    lane = lax.broadcasted_iota(jnp.int32, (tm, LANES), 1)
    first_half = (lane & 16) == 0
    low = lane < HEAD_DIM
    if with_q:
        q_ref, k_ref, v_ref = out_refs
        nq = N_HEADS * HEAD_DIM // LANES
    else:
        k_ref, v_ref = out_refs
        nq = 0
    nk = N_KV_HEADS * HEAD_DIM // LANES
    cos = cos_ref[...]
    sin = sin_ref[...]
    srow = lax.broadcasted_iota(jnp.int32, (LANES, LANES), 0)
    scol = lax.broadcasted_iota(jnp.int32, (LANES, LANES), 1)
    swap = jnp.where(srow == (scol ^ 16), 1.0, 0.0).astype(BF16)

    def norm_rope(c, gain_row):
        ms = _dot((c * c).astype(BF16), ebd_ref[...])
        cn = c * lax.rsqrt(ms + EPS) * gain_row
        c_hi, c_lo = _split_bf16(cn)
        partner = _dot(c_hi, swap) + _dot(c_lo, swap)
        return cn * cos + partner * sin

    for j in range(nq):
        out = norm_rope(qkv[:, j * LANES:(j + 1) * LANES], hg_ref[0:1, :]).astype(BF16)
        q_ref[0, 2 * j] = out[:, :HEAD_DIM]
        q_ref[0, 2 * j + 1] = out[:, HEAD_DIM:]
    for j in range(nk):
        out = norm_rope(qkv[:, (nq + j) * LANES:(nq + j + 1) * LANES], hg_ref[1:2, :]).astype(BF16)
        k_ref[0, 2 * j] = out[:, :HEAD_DIM]
        k_ref[0, 2 * j + 1] = out[:, HEAD_DIM:]
    for j in range(nk):
        c = qkv[:, (nq + nk + j) * LANES:(nq + nk + j + 1) * LANES]
        v_ref[0, 2 * j] = jnp.where(low, c, 1.0).astype(BF16)
        v_ref[0, 2 * j + 1] = jnp.where(low, pltpu.roll(c, HEAD_DIM, 1), 1.0).astype(BF16)


def _qkv(x, shift, scale, gain, w, head_gain, ebd, cos, sin, *, with_q, per_batch_mods):
    b, n, d = x.shape
    tm = min(512, n)
    ncol = w.shape[1]
    mod_idx = (lambda bi, i: (bi, 0, 0)) if per_batch_mods else (lambda bi, i: (0, 0, 0))
    out_shape = []
    out_specs = []
    if with_q:
        out_shape.append(jax.ShapeDtypeStruct((b, N_HEADS, n, HEAD_DIM), BF16))
        out_specs.append(pl.BlockSpec((1, N_HEADS, tm, HEAD_DIM), lambda bi, i: (bi, 0, i, 0)))
    out_shape.append(jax.ShapeDtypeStruct((b, N_KV_HEADS, n, HEAD_DIM), BF16))
    out_specs.append(pl.BlockSpec((1, N_KV_HEADS, tm, HEAD_DIM), lambda bi, i: (bi, 0, i, 0)))
    out_shape.append(jax.ShapeDtypeStruct((b, N_KV_HEADS, n, LANES), BF16))
    out_specs.append(pl.BlockSpec((1, N_KV_HEADS, tm, LANES), lambda bi, i: (bi, 0, i, 0)))
    return pl.pallas_call(
        functools.partial(_qkv_kernel, with_q=with_q),
        grid=(b, n // tm),
        in_specs=[
            pl.BlockSpec((1, tm, d), lambda bi, i: (bi, i, 0)),
            pl.BlockSpec((1, 1, d), mod_idx),
            pl.BlockSpec((1, 1, d), mod_idx),
            pl.BlockSpec((1, d), lambda bi, i: (0, 0)),
            pl.BlockSpec((d, ncol), lambda bi, i: (0, 0)),
            pl.BlockSpec((2, LANES), lambda bi, i: (0, 0)),
            pl.BlockSpec((LANES, LANES), lambda bi, i: (0, 0)),
            pl.BlockSpec((tm, LANES), lambda bi, i: (i, 0)),
            pl.BlockSpec((tm, LANES), lambda bi, i: (i, 0)),
        ],
        out_specs=out_specs,
        out_shape=out_shape,
        compiler_params=_params(("arbitrary", "arbitrary")),
        name="qkv" if with_q else "ctx_kv",
    )(x, shift, scale, gain, w, head_gain, ebd, cos, sin)


def _attn_kernel(q_ref, k_ref, v_ref, kc_ref, vc_ref, o_ref, *, tq, tk, n_sub):
    chunks = []
    for kr, vr in ((k_ref, v_ref), (kc_ref, vc_ref)):
        size = min(tk, kr.shape[2])
        chunks += [(kr, vr, c, size) for c in range(kr.shape[2] // size)]
    sub = tq // n_sub
    m_rows = GQA_GROUP * sub

    def query_block(blk, carry):
        for u in range(n_sub):
            rows = pl.ds(pl.multiple_of(blk * tq + u * sub, sub), sub)
            q = q_ref[0, :, rows, :].reshape(m_rows, HEAD_DIM)
            m = jnp.full((m_rows, 1), -jnp.inf, F32)
            acc = jnp.zeros((m_rows, LANES), F32)
            for kr, vr, c, size in chunks:
                kc = kr[0, 0, c * size:(c + 1) * size, :]
                vc = vr[0, 0, c * size:(c + 1) * size, :]
                s = lax.dot_general(q, kc, (((1,), (1,)), ((), ())), preferred_element_type=F32)
                m_new = jnp.maximum(m, jnp.max(s, axis=-1, keepdims=True))
                alpha = jnp.exp2(m - m_new)
                p = jnp.exp2(s - m_new)
                acc = alpha * acc + _dot(p.astype(BF16), vc)
                m = m_new
            o = acc * (1.0 / pltpu.roll(acc, HEAD_DIM, 1))
            o_ref[0, rows, :] = jnp.concatenate(
                [o[g * sub:(g + 1) * sub, :HEAD_DIM] for g in range(GQA_GROUP)], axis=-1).astype(BF16)
        return carry

    lax.fori_loop(0, q_ref.shape[2] // tq, query_block, 0)


def _attention(q, k, v, k_ctx, v_ctx):
    b, _, n, _ = q.shape
    n_ctx = k_ctx.shape[2]
    tq = min(512, n)
    tk = 256
    assert n % min(tk, n) == 0 and n_ctx % min(tk, n_ctx) == 0
    kv_spec = lambda rows, width: pl.BlockSpec((1, 1, rows, width), lambda bi, kh: (bi, kh, 0, 0))
    return pl.pallas_call(
        functools.partial(_attn_kernel, tq=tq, tk=tk, n_sub=tq // min(256, tq)),
        grid=(b, N_KV_HEADS),
        in_specs=[
            pl.BlockSpec((1, GQA_GROUP, n, HEAD_DIM), lambda bi, kh: (bi, kh, 0, 0)),
            kv_spec(n, HEAD_DIM), kv_spec(n, LANES), kv_spec(n_ctx, HEAD_DIM), kv_spec(n_ctx, LANES),
        ],
        out_specs=pl.BlockSpec((1, n, GQA_GROUP * HEAD_DIM), lambda bi, kh: (bi, 0, kh)),
        out_shape=jax.ShapeDtypeStruct((b, n, N_HEADS * HEAD_DIM), BF16),
        compiler_params=_params(("arbitrary", "arbitrary")),
        name="attn",
    )(q, k, v, k_ctx, v_ctx)


def _proj_moe_kernel(*refs, tiles_per_batch, folded):
    if folded:
        dif_ref, sum_ref, sumb_ref, *refs, a_scr = refs
        j = pl.program_id(0) % tiles_per_batch

        @pl.when(j < tiles_per_batch // 2)
        def _():
            a_scr[...] = dif_ref[0]

        @pl.when(j >= tiles_per_batch // 2)
        def _():
            rows = _dot(_reversal(a_scr.shape[0]), sum_ref[0])
            row = lax.broadcasted_iota(jnp.int32, rows.shape, 0)
            a_scr[...] = jnp.where(row == 0, sumb_ref[0, 0:1, :].astype(F32), rows).astype(BF16)

        a = a_scr[...]
    else:
        a_ref, *refs = refs
        a = a_ref[...]
    (wo_ref, x_ref, g1_ref, gain_ref, sh_ref, sc_ref, wrh_ref, wrl_ref, br_ref,
     x1_ref, s_ref, gp_ref, pc_ref) = refs
    x1 = x_ref[...] + g1_ref[0] * _dot(a, wo_ref[...])
    x1_ref[...] = x1
    h2 = _modulate(x1, gain_ref[...], sh_ref[0], sc_ref[0])
    h_hi, h_lo = _split_bf16(h2)
    r_hi = _dot(h_hi, wrl_ref[...])
    logits = (r_hi[:, :LANES] + _dot(h_lo, wrh_ref[...]) + r_hi[:, LANES:]
              + br_ref[...])
    tm = logits.shape[0]
    lane = lax.broadcasted_iota(jnp.int32, (tm, LANES), 1)
    lane_f = lane.astype(F32)

    def top(vals):
        best = jnp.max(vals, axis=-1, keepdims=True)
        idx = jnp.min(jnp.where(vals == best, lane_f, float(LANES)), axis=-1, keepdims=True)
        return best, idx

    lg = jnp.where(lane < N_GROUPS, logits, NEG_BIG)
    g_max, g_sel = top(lg)
    p_grp = 1.0 / jnp.sum(jnp.where(lane < N_GROUPS, jnp.exp(lg - g_max), 0.0), axis=-1, keepdims=True)
    lo = N_GROUPS + EXPERTS_PER_GROUP * g_sel
    le = jnp.where((lane_f >= lo) & (lane_f < lo + EXPERTS_PER_GROUP), logits, NEG_BIG)
    t1, i1 = top(le)
    t2, i2 = top(jnp.where(lane_f == i1, NEG_BIG, le))
    e21 = jnp.exp(t2 - t1)
    w1 = 1.0 / (1.0 + e21)
    gate0 = p_grp * w1
    gate1 = p_grp * (e21 * w1)
    e0 = i1 - N_GROUPS
    e1 = i2 - N_GROUPS

    onehot = jnp.where((lane_f == e0) | (lane_f == e1), 1.0, 0.0)
    row = lax.broadcasted_iota(jnp.int32, (tm, tm), 0)
    col = lax.broadcasted_iota(jnp.int32, (tm, tm), 1)
    lower = jnp.where(col < row, 1.0, 0.0).astype(BF16)
    before = _dot(lower, onehot.astype(BF16))
    cnt = jnp.sum(onehot, axis=0, keepdims=True)
    pc = jnp.floor((cnt + (RUN_ALIGN - 1)) * (1.0 / RUN_ALIGN))
    urow = lax.broadcasted_iota(jnp.int32, (LANES, LANES), 0)
    ucol = lax.broadcasted_iota(jnp.int32, (LANES, LANES), 1)
    upper = jnp.where(urow < ucol, 1.0, 0.0).astype(BF16)
    run_start = _dot(jnp.broadcast_to(pc, (8, LANES)).astype(BF16), upper)[0:1] * RUN_ALIGN
    base = run_start + before
    pos0 = jnp.sum(jnp.where(lane_f == e0, base, 0.0), axis=-1, keepdims=True)
    pos1 = jnp.sum(jnp.where(lane_f == e1, base, 0.0), axis=-1, keepdims=True)
    gp = jnp.where(lane == 0, gate0, jnp.where(lane == 1, gate1, jnp.where(lane == 2, pos0,
                                                                       jnp.where(lane == 3, pos1, 0.0))))
    gp_ref[...] = gp
    pc_ref[...] = jnp.broadcast_to(pc, (8, LANES)).astype(jnp.int32)

    pos_t = jnp.where(lane == 0, pos0, jnp.where(lane == 1, pos1, -1.0)).T
    sr = s_ref.shape[0]
    srow = lax.broadcasted_iota(jnp.int32, (sr, tm), 0).astype(F32)
    perm = jnp.where((srow == pos_t[0:1, :]) | (srow == pos_t[1:2, :]), 1.0, 0.0).astype(BF16)
    s_ref[...] = _pack_halves(_dot(perm, h_hi))


def _proj_moe(a, w_o, x, g1, gain, shift, scale, wr_hi, wr_lo, b_r, *, tokens_per_batch):
    t, d = x.shape
    tm = min(MOE_TILE, tokens_per_batch)
    nt = t // tm
    sr = _sorted_rows(tm)
    per_b = tokens_per_batch // tm
    mod_idx = lambda i: (i // per_b, 0, 0)
    row_spec = lambda w: pl.BlockSpec((tm, w), lambda i: (i, 0))
    full = lambda s: pl.BlockSpec(s, lambda i: (0,) * len(s))
    folded = isinstance(a, tuple)
    if folded:
        assert per_b % 2 == 0
        half = per_b // 2
        a_inputs = (a[0], a[1], a[1])
        a_specs = [
            pl.BlockSpec((1, tm, d), lambda i: (i // per_b, jnp.minimum(i % per_b, half - 1), 0)),
            pl.BlockSpec((1, tm, d), lambda i: (i // per_b, jnp.clip(per_b - 1 - i % per_b, 0, half - 1), 0)),
            pl.BlockSpec((1, 8, d), lambda i: (i // per_b, (tm // 8) * jnp.clip(per_b - i % per_b, 1, half), 0)),
        ]
        scratch = [pltpu.VMEM((tm, d), BF16)]
    else:
        a_inputs = (a,)
        a_specs = [row_spec(d)]
        scratch = []
    return pl.pallas_call(
        functools.partial(_proj_moe_kernel, tiles_per_batch=per_b, folded=folded),
        grid=(nt,),
        in_specs=a_specs + [
            full((d, d)), row_spec(d),
            pl.BlockSpec((1, 1, d), mod_idx), full((1, d)),
            pl.BlockSpec((1, 1, d), mod_idx), pl.BlockSpec((1, 1, d), mod_idx),
            full((d, LANES)), full((d, 2 * LANES)), full((1, LANES)),
        ],
        out_specs=[row_spec(d), pl.BlockSpec((sr, d // 2), lambda i: (i, 0)), row_spec(LANES),
                   pl.BlockSpec((8, LANES), lambda i: (i, 0))],
        out_shape=[
            jax.ShapeDtypeStruct((t, d), F32), jax.ShapeDtypeStruct((nt * sr, d // 2), jnp.uint32),
            jax.ShapeDtypeStruct((t, LANES), F32), jax.ShapeDtypeStruct((nt * 8, LANES), jnp.int32),
        ],
        scratch_shapes=scratch,
        compiler_params=_params(("arbitrary",)),
        name="proj_moe",
    )(*a_inputs, w_o, x, g1, gain, shift, scale, wr_hi, wr_lo, b_r)


def _rows(ref, start_units, n_units):
    return ref.at[pl.ds(pl.multiple_of(start_units * RUN_ALIGN, RUN_ALIGN), n_units * RUN_ALIGN)]


def _ffn_kernel(exp_nblk, exp_blk0, blk_e, blk_b0, blk_n, blk_tlo, blk_thi, seg_start, run_start,
                nused_ref, s_hbm, w1_ref, w3_ref, w2_ref, y_hbm, xg, yo, w1b, w3b, w2b, sem, osem,
                *, sr_units, nblk):
    e_step = pl.program_id(0)
    nused = nused_ref[0]
    n_slots, bm, _ = xg.shape
    n_ahead = n_slots - 1
    bm_units = bm // RUN_ALIGN

    def gather(b, slot):
        e = blk_e[b]
        b0 = blk_b0[b]

        def body(ti, carry):
            rs = seg_start[ti * N_EXPERTS + e]
            re = seg_start[(ti + 1) * N_EXPERTS + e]
            lo = jnp.maximum(rs, b0)
            n = jnp.minimum(re, b0 + bm_units) - lo

            @pl.when(n > 0)
            def _():
                src = ti * sr_units + run_start[ti * N_EXPERTS + e] + (lo - rs)
                pltpu.make_async_copy(_rows(s_hbm, src, n), _rows(xg.at[slot], lo - b0, n),
                                      sem.at[slot]).start()

            return carry

        lax.fori_loop(blk_tlo[b], blk_thi[b], body, 0)

    def write_back(g, slot):
        return pltpu.make_async_copy(yo.at[slot], y_hbm.at[pl.ds(pl.multiple_of(g * bm, bm), bm)],
                                     osem.at[slot])

    def emit(g, fill):
        slot = g % 2

        @pl.when(g >= 2)
        def _():
            write_back(g - 2, slot).wait()

        fill(slot)
        write_back(g, slot).start()

    @pl.when(e_step == 0)
    def _():
        xg[...] = jnp.zeros_like(xg)
        for g0 in range(n_ahead):

            @pl.when(g0 < nused)
            def _():
                gather(g0, g0)

    w1b[...] = w1_ref[0, 0].astype(BF16)
    w3b[...] = w3_ref[0, 0].astype(BF16)
    w2b[...] = w2_ref[0, 0].astype(BF16)

    def block(jb, carry):
        g = exp_blk0[e_step] + jb

        @pl.when(g + n_ahead < nused)
        def _():
            gather(g + n_ahead, (g + n_ahead) % n_slots)

        def fill(slot):
            gslot = g % n_slots
            pltpu.make_async_copy(_rows(s_hbm, 0, blk_n[g]), _rows(xg.at[gslot], 0, blk_n[g]),
                                  sem.at[gslot]).wait()
            x_lo, x_hi = _unpack_halves(xg[gslot])
            half = x_lo.shape[1]
            a = _dot(x_lo, w1b[:half, :]) + _dot(x_hi, w1b[half:, :])
            b = _dot(x_lo, w3b[:half, :]) + _dot(x_hi, w3b[half:, :])
            mid = (a / (1.0 + jnp.exp(-a))) * b
            y = _dot(mid.astype(BF16), w2b[...])
            yo[slot] = _pack_halves(y.astype(BF16).astype(F32))

        emit(g, fill)
        return carry

    lax.fori_loop(0, exp_nblk[e_step], block, 0)

    @pl.when(e_step == pl.num_programs(0) - 1)
    def _():
        def tail(g, carry):
            def fill(slot):
                yo[slot] = jnp.zeros(yo.shape[1:], yo.dtype)

            emit(g, fill)
            return carry

        lax.fori_loop(nused, nblk, tail, 0)
        write_back(nblk - 2, nblk % 2).wait()
        write_back(nblk - 1, (nblk - 1) % 2).wait()


def _ffn(tables, s_tiles, w1, w3, w2, *, layer, nblk, sr):
    bm = FFN_BLOCK
    d = w1.shape[2]
    de = w1.shape[3]
    nsp = len(tables)
    assert nblk >= 2
    grid_spec = pltpu.PrefetchScalarGridSpec(
        num_scalar_prefetch=nsp,
        grid=(N_EXPERTS,),
        in_specs=[
            pl.BlockSpec(memory_space=pl.ANY),
            pl.BlockSpec((1, 1, d, de), lambda e, *_: (layer, e, 0, 0)),
            pl.BlockSpec((1, 1, d, de), lambda e, *_: (layer, e, 0, 0)),
            pl.BlockSpec((1, 1, de, d), lambda e, *_: (layer, e, 0, 0)),
        ],
        out_specs=pl.BlockSpec(memory_space=pl.ANY),
        scratch_shapes=[pltpu.VMEM((FFN_GATHER_SLOTS, bm, d // 2), jnp.uint32),
                        pltpu.VMEM((2, bm, d // 2), jnp.uint32),
                        pltpu.VMEM((d, de), BF16), pltpu.VMEM((d, de), BF16), pltpu.VMEM((de, d), BF16),
                        pltpu.SemaphoreType.DMA((FFN_GATHER_SLOTS,)), pltpu.SemaphoreType.DMA((2,))],
    )
    return pl.pallas_call(
        functools.partial(_ffn_kernel, sr_units=sr // RUN_ALIGN, nblk=nblk),
        grid_spec=grid_spec,
        out_shape=jax.ShapeDtypeStruct((nblk * bm, d // 2), jnp.uint32),
        compiler_params=_params(("arbitrary",)),
        name="ffn",
    )(*tables, s_tiles, w1, w3, w2)


def _combine_kernel(run_len, run_start, seg_start, exp_start, tile_rows,
                    y_hbm, x_ref, g2_ref, gp_ref, fgain_ref, o_ref, ybuf, sem, *, nsteps, final):
    i = pl.program_id(0)

    def gather(ti, slot):
        for e in range(N_EXPERTS):
            n = run_len[ti * N_EXPERTS + e]

            @pl.when(n > 0)
            def _():
                src = exp_start[e] + seg_start[ti * N_EXPERTS + e]
                pltpu.make_async_copy(_rows(y_hbm, src, n),
                                      _rows(ybuf.at[slot], run_start[ti * N_EXPERTS + e], n),
                                      sem.at[slot]).start()

    @pl.when(i == 0)
    def _():
        ybuf[...] = jnp.zeros_like(ybuf)
        gather(0, 0)

    @pl.when(i + 1 < nsteps)
    def _():
        gather(i + 1, (i + 1) % 2)

    slot = i % 2
    pltpu.make_async_copy(_rows(y_hbm, 0, tile_rows[i]), _rows(ybuf.at[slot], 0, tile_rows[i]),
                          sem.at[slot]).wait()
    gp = gp_ref[...]
    tm = gp.shape[0]
    sr = ybuf.shape[1]
    col = lax.broadcasted_iota(jnp.int32, (tm, sr), 1).astype(F32)
    weights = (jnp.where(col == gp[:, 2:3], gp[:, 0:1], 0.0)
               + jnp.where(col == gp[:, 3:4], gp[:, 1:2], 0.0)).astype(BF16)
    y_lo, y_hi = _unpack_halves(ybuf[slot])
    y = jnp.concatenate([_dot(weights, y_lo), _dot(weights, y_hi)], axis=-1)
    x2 = x_ref[...] + g2_ref[0] * y
    if final:
        ms = jnp.mean(x2 * x2, axis=-1, keepdims=True)
        x2 = (x2 * lax.rsqrt(ms + EPS)) * fgain_ref[...]
    o_ref[...] = x2


def _combine(tables, yb, x1, g2, gp, fgain, *, tokens_per_batch, final):
    t, d = x1.shape
    tm = min(MOE_TILE, tokens_per_batch)
    nsteps = t // tm
    per_b = tokens_per_batch // tm
    sr = _sorted_rows(tm)
    grid_spec = pltpu.PrefetchScalarGridSpec(
        num_scalar_prefetch=len(tables),
        grid=(nsteps,),
        in_specs=[
            pl.BlockSpec(memory_space=pl.ANY),
            pl.BlockSpec((tm, d), lambda i, *_: (i, 0)),
            pl.BlockSpec((1, 1, d), lambda i, *_: (i // per_b, 0, 0)),
            pl.BlockSpec((tm, LANES), lambda i, *_: (i, 0)),
            pl.BlockSpec((1, d), lambda i, *_: (0, 0)),
        ],
        out_specs=pl.BlockSpec((tm, d), lambda i, *_: (i, 0)),
        scratch_shapes=[pltpu.VMEM((2, sr, d // 2), jnp.uint32), pltpu.SemaphoreType.DMA((2,))],
    )
    return pl.pallas_call(
        functools.partial(_combine_kernel, nsteps=nsteps, final=final),
        grid_spec=grid_spec,
        out_shape=jax.ShapeDtypeStruct((t, d), F32),
        compiler_params=_params(("arbitrary",)),
        name="combine_final" if final else "combine",
    )(*tables, yb, x1, g2, gp, fgain)


def _reversal(size):
    row = lax.broadcasted_iota(jnp.int32, (size, size), 0)
    col = lax.broadcasted_iota(jnp.int32, (size, size), 1)
    return jnp.where(col == size - row, 1.0, 0.0).astype(BF16)


def _fmix_fold_kernel(x_ref, xm_ref, xb_ref, xc_ref, sh_ref, sc_ref, gain_ref, cc_ref, sc_mat_ref,
                      a_ref, b_ref, amid_ref):
    mod = lambda v: _modulate(v, gain_ref[...], sh_ref[0], sc_ref[0])
    h = mod(x_ref[...])
    ft = h.shape[0]
    m_hi, m_lo = _split_bf16(mod(xm_ref[...]))
    rev = _reversal(ft)
    row = lax.broadcasted_iota(jnp.int32, h.shape, 0)
    hr = jnp.where(row == 0, mod(xb_ref[...])[0:1], _dot(rev, m_hi) + _dot(rev, m_lo))
    he = (h + hr).astype(BF16)
    ho = (h - hr).astype(BF16)
    hc = mod(xc_ref[...]).astype(BF16)
    gw = cc_ref.shape[0]
    for g in range(FOURIER_GROUPS):
        cols = slice(g * gw, (g + 1) * gw)
        a_ref[:, cols] = _dot(he[:, cols], cc_ref[...]).astype(BF16)
        b_ref[:, cols] = _dot(ho[:, cols], sc_mat_ref[...]).astype(BF16)
        amid_ref[:, cols] = _dot(hc[:, cols], cc_ref[...])


def _fmix_fold(x, shift, scale, gain, cc, sc, *, bsz, n):
    t, d = x.shape
    hn = n // 2
    ft = min(512, hn)
    tpb = n // ft
    half = tpb // 2
    gw = d // FOURIER_GROUPS
    mod_idx = lambda b, j: (b, 0, 0)
    return pl.pallas_call(
        _fmix_fold_kernel,
        grid=(bsz, half),
        in_specs=[
            pl.BlockSpec((ft, d), lambda b, j: (b * tpb + j, 0)),
            pl.BlockSpec((ft, d), lambda b, j: (b * tpb + tpb - 1 - j, 0)),
            pl.BlockSpec((8, d), lambda b, j: (b * (n // 8) + (ft // 8) * ((tpb - j) % tpb), 0)),
            pl.BlockSpec((8, d), lambda b, j: (b * (n // 8) + hn // 8, 0)),
            pl.BlockSpec((1, 1, d), mod_idx), pl.BlockSpec((1, 1, d), mod_idx),
            pl.BlockSpec((1, d), lambda b, j: (0, 0)),
            pl.BlockSpec((gw, gw), lambda b, j: (0, 0)), pl.BlockSpec((gw, gw), lambda b, j: (0, 0)),
        ],
        out_specs=[pl.BlockSpec((ft, d), lambda b, j: (b * half + j, 0)),
                   pl.BlockSpec((ft, d), lambda b, j: (b * half + j, 0)),
                   pl.BlockSpec((8, d), lambda b, j: (b, 0))],
        out_shape=[jax.ShapeDtypeStruct((bsz * hn, d), BF16), jax.ShapeDtypeStruct((bsz * hn, d), BF16),
                   jax.ShapeDtypeStruct((bsz * 8, d), F32)],
        compiler_params=_params(("arbitrary", "arbitrary")),
        name="fmix_fold",
    )(x, x, x, x, shift, scale, gain, cc, sc)


def _fmix_pos_kernel(ch_ref, sh_ref, sign_ref, a_ref, b_ref, amid_ref, dif_ref, sum_ref):
    p = _dot(ch_ref[...], a_ref[0]) + sign_ref[...] * amid_ref[0, 0:1, :]
    q = _dot(sh_ref[...], b_ref[0])
    dif_ref[0] = (p - q).astype(BF16)
    sum_ref[0] = (p + q).astype(BF16)


def _fmix_pos(ch, sh, sign, a, b, amid):
    bsz, hn, d = a.shape
    mp = ch.shape[0]
    tmr = mp // DFT_ROW_BLOCKS
    assert tmr % 16 == 0
    out = jax.ShapeDtypeStruct((bsz, mp, d), BF16)
    return pl.pallas_call(
        _fmix_pos_kernel,
        grid=(bsz, DFT_ROW_BLOCKS),
        in_specs=[
            pl.BlockSpec((tmr, hn), lambda bi, i: (i, 0)),
            pl.BlockSpec((tmr, hn), lambda bi, i: (i, 0)),
            pl.BlockSpec((tmr, 1), lambda bi, i: (i, 0)),
            pl.BlockSpec((1, hn, d), lambda bi, i: (bi, 0, 0)),
            pl.BlockSpec((1, hn, d), lambda bi, i: (bi, 0, 0)),
            pl.BlockSpec((1, 8, d), lambda bi, i: (bi, 0, 0)),
        ],
        out_specs=[pl.BlockSpec((1, tmr, d), lambda bi, i: (bi, i, 0)),
                   pl.BlockSpec((1, tmr, d), lambda bi, i: (bi, i, 0))],
        out_shape=[out, out],
        compiler_params=_params(("arbitrary", "arbitrary")),
        name="fmix_pos",
    )(ch, sh, sign, a, b, amid)


def _rope_tables(n):
    half = ROPE_AXIS_DIM // 2
    rows = n // GRID_W
    inv = ROPE_THETA ** (-jnp.arange(half, dtype=F32) / half)
    pos = jnp.arange(n, dtype=jnp.int32)
    ang_r = (pos // GRID_W).astype(F32)[:, None] * inv
    ang_c = (pos % GRID_W).astype(F32)[:, None] * inv
    del rows
    cos = jnp.concatenate([jnp.cos(ang_r)] * 2 + [jnp.cos(ang_c)] * 2, axis=-1)
    sin = jnp.concatenate([-jnp.sin(ang_r), jnp.sin(ang_r), -jnp.sin(ang_c), jnp.sin(ang_c)], axis=-1)
    return jnp.tile(cos, (1, LANES // HEAD_DIM)), jnp.tile(sin, (1, LANES // HEAD_DIM))


def _dft_mats(n, scale):
    r = 1
    while r * r < n:
        r *= 2
    c = n // r
    k = jnp.arange(n, dtype=jnp.int32)[:, None]
    ang_a = ((k * jnp.arange(r, dtype=jnp.int32)[None, :] * c) % n).astype(F32) * (2.0 * math.pi / n)
    ang_b = ((k * jnp.arange(c, dtype=jnp.int32)[None, :]) % n).astype(F32) * (2.0 * math.pi / n)
    ca, sa, cb, sb = jnp.cos(ang_a), jnp.sin(ang_a), jnp.cos(ang_b), jnp.sin(ang_b)
    cosm = (ca[:, :, None] * cb[:, None, :] - sa[:, :, None] * sb[:, None, :]).reshape(n, n)
    sinm = (sa[:, :, None] * cb[:, None, :] + ca[:, :, None] * sb[:, None, :]).reshape(n, n)
    return (cosm * scale).astype(BF16), (sinm * scale).astype(BF16)


DFT_ROW_BLOCKS = 4
DFT_ROW_PAD = 64


def _dft_half(n):
    hn = n // 2
    mp = hn + DFT_ROW_PAD
    kb = 32
    ka = mp // kb
    assert ka * kb == mp
    m = jnp.arange(hn, dtype=jnp.int32)[None, :]
    w = 2.0 * math.pi / n
    ang_a = (((jnp.arange(ka, dtype=jnp.int32)[:, None] * kb) * m) % n).astype(F32) * w
    ang_b = ((jnp.arange(kb, dtype=jnp.int32)[:, None] * m) % n).astype(F32) * w
    ca, sa, cb, sb = jnp.cos(ang_a), jnp.sin(ang_a), jnp.cos(ang_b), jnp.sin(ang_b)
    cosm = (ca[:, None, :] * cb[None, :, :] - sa[:, None, :] * sb[None, :, :]).reshape(mp, hn)
    sinm = (sa[:, None, :] * cb[None, :, :] + ca[:, None, :] * sb[None, :, :]).reshape(mp, hn)
    k = jnp.arange(mp, dtype=jnp.int32)[:, None]
    valid = k <= hn
    scale = n ** -0.5
    col_w = jnp.where(m == 0, 0.5 * scale, scale)
    cosm = jnp.where(valid, cosm * col_w, 0.0).astype(BF16)
    sinm = jnp.where(valid, sinm * scale, 0.0).astype(BF16)
    sign = jnp.where(valid, jnp.where(k % 2 == 0, scale, -scale), 0.0).astype(F32)
    return cosm, sinm, sign


def _moe(a, w_o, x, g1, g2, gain, shift, scale, w_rg, b_rg, w_re, b_re, w1, w3, w2, fgain, *,
         layer, tokens_per_batch, final):
    t, d = x.shape
    pad = LANES - N_GROUPS - N_EXPERTS
    w_r = jnp.concatenate([w_rg, w_re, jnp.zeros((d, pad), F32)], axis=1)
    b_r = jnp.concatenate([b_rg, b_re, jnp.zeros((pad,), F32)])[None, :]
    wr_hi = w_r.astype(BF16)
    wr_lo = jnp.concatenate([wr_hi, (w_r - wr_hi.astype(F32)).astype(BF16)], axis=1)
    x1, s_tiles, gp, pc = _proj_moe(a, w_o, x, g1, gain, shift, scale, wr_hi, wr_lo, b_r,
                                    tokens_per_batch=tokens_per_batch)
    tm = min(MOE_TILE, tokens_per_batch)
    nt = t // tm
    sr = _sorted_rows(tm)
    bmu = FFN_BLOCK // RUN_ALIGN
    i32 = jnp.int32
    run_len = pc.reshape(nt, 8, LANES)[:, 0, :N_EXPERTS]
    seg_start = jnp.concatenate([jnp.zeros((1, N_EXPERTS), i32), jnp.cumsum(run_len, axis=0)])
    seg_len = seg_start[-1]
    seg_pad = (seg_len + bmu - 1) // bmu * bmu
    exp_end = jnp.cumsum(seg_pad)
    exp_start = exp_end - seg_pad
    run_start = jnp.cumsum(run_len, axis=1) - run_len
    tile_rows = run_start[:, -1] + run_len[:, -1]
    max_units = (2 * t) // RUN_ALIGN + nt * N_EXPERTS + N_EXPERTS * (bmu - 1)
    nblk = (max_units + bmu - 1) // bmu
    blk_start = jnp.arange(nblk, dtype=i32) * bmu
    blk_e = jnp.minimum(jnp.sum((exp_end[None, :] <= blk_start[:, None]).astype(i32), axis=1),
                        N_EXPERTS - 1)
    pick = (blk_e[:, None] == jnp.arange(N_EXPERTS, dtype=i32)[None, :]).astype(i32)
    blk_b0 = blk_start - jnp.sum(pick * exp_start[None, :], axis=1)
    blk_n = jnp.clip(jnp.sum(pick * seg_len[None, :], axis=1) - blk_b0, 0, bmu)
    seg_e = jnp.sum(pick[:, None, :] * seg_start[None, :, :], axis=2)
    blk_tlo = jnp.sum((seg_e[:, 1:] <= blk_b0[:, None]).astype(i32), axis=1)
    blk_thi = jnp.sum((seg_e[:, :-1] < (blk_b0 + bmu)[:, None]).astype(i32), axis=1)
    nused = (exp_end[-1] // bmu)[None]
    flat = lambda v: v.reshape(-1).astype(i32)
    yb = _ffn((seg_pad // bmu, exp_start // bmu, blk_e, blk_b0, blk_n, blk_tlo, blk_thi,
               flat(seg_start), flat(run_start), nused),
              s_tiles, w1, w3, w2, layer=layer, nblk=nblk, sr=sr)
    return _combine((flat(run_len), flat(run_start), flat(seg_start), exp_start, tile_rows),
                    yb, x1, g2, gp, fgain, tokens_per_batch=tokens_per_batch, final=final)


def kernel(x, c, ctx, c_ctx, w_mod, b_mod, norm_mix, norm_ffn, attn_w_qkv, attn_q_norm, attn_k_norm,
           attn_w_o, fourier_w_o, moe_w_rg, moe_b_rg, moe_w_re, moe_b_re, moe_w1, moe_w3, moe_w2,
           final_norm):
    bsz, n, d = x.shape
    t = bsz * n
    q_dim = N_HEADS * HEAD_DIM
    kv_dim = N_KV_HEADS * HEAD_DIM

    cond = jnp.zeros((16, d), F32).at[:bsz].set(c).at[bsz].set(c_ctx)
    mods = _ada(cond, w_mod, b_mod)

    def lat(layer, j):
        return mods[layer, :bsz, j * d:(j + 1) * d][:, None, :]

    def ctxm(layer, j):
        return mods[layer, bsz:bsz + 1, j * d:(j + 1) * d][:, None, :]

    w_qkv = attn_w_qkv[0].astype(BF16)
    qscale = (HEAD_DIM ** -0.5) * math.log2(math.e)
    head_gain = jnp.stack([jnp.tile(attn_q_norm[0] * qscale, LANES // HEAD_DIM),
                           jnp.tile(attn_k_norm[0], LANES // HEAD_DIM)])
    lane = jnp.arange(LANES)
    ebd = jnp.where((lane[:, None] // HEAD_DIM) == (lane[None, :] // HEAD_DIM),
                    1.0 / HEAD_DIM, 0.0).astype(BF16)
    cos, sin = _rope_tables(n)
    gain0 = norm_mix[0][None, :]
    q, k_l, v_l = _qkv(x, lat(0, 0), lat(0, 1), gain0, w_qkv, head_gain, ebd, cos, sin,
                       with_q=True, per_batch_mods=True)
    n_ctx = ctx.shape[1]
    k_c, v_c = _qkv(ctx, ctxm(0, 0), ctxm(0, 1), gain0, w_qkv[:, q_dim:], head_gain, ebd,
                    jnp.ones((n_ctx, LANES), F32), jnp.zeros((n_ctx, LANES), F32),
                    with_q=False, per_batch_mods=False)
    del kv_dim
    o = _attention(q, k_l, v_l, k_c, v_c).reshape(t, q_dim)

    xf = x.reshape(t, d)
    moe_w = lambda i: (moe_w_rg[i], moe_b_rg[i], moe_w_re[i], moe_b_re[i],
                       moe_w1, moe_w3, moe_w2)
    fgain = final_norm[None, :]
    x2 = _moe(o, attn_w_o[0].astype(BF16), xf, lat(0, 2), lat(0, 5), norm_ffn[0][None, :],
              lat(0, 3), lat(0, 4), *moe_w(0), fgain, layer=0, tokens_per_batch=n, final=False)

    gw = d // FOURIER_GROUPS
    cc, sc = _dft_mats(gw, gw ** -0.5)
    ch, sh, sign = _dft_half(n)
    a_e, b_o, a_mid = _fmix_fold(x2, lat(1, 0), lat(1, 1), norm_mix[1][None, :], cc, sc, bsz=bsz, n=n)
    f = tuple(_fmix_pos(ch, sh, sign, a_e.reshape(bsz, n // 2, d), b_o.reshape(bsz, n // 2, d),
                        a_mid.reshape(bsz, 8, d)))
    out = _moe(f, fourier_w_o[0].astype(BF16), x2, lat(1, 2), lat(1, 5), norm_ffn[1][None, :],
               lat(1, 3), lat(1, 4), *moe_w(1), fgain, layer=1, tokens_per_batch=n, final=True)
    return out.reshape(bsz, n, d)
```

```python
import functools
import math

import jax
import jax.numpy as jnp
from jax import lax
from jax.experimental import pallas as pl
from jax.experimental.pallas import tpu as pltpu

F32 = jnp.float32
BF16 = jnp.bfloat16

N_HEADS = 16
N_KV_HEADS = 4
HEAD_DIM = 64
GQA_GROUP = N_HEADS // N_KV_HEADS
GRID_W = 64
ROPE_AXIS_DIM = HEAD_DIM // 2
ROPE_THETA = 10000.0
FOURIER_GROUPS = 4
N_GROUPS = 4
EXPERTS_PER_GROUP = 8
N_EXPERTS = N_GROUPS * EXPERTS_PER_GROUP
EPS = 1e-6
LANES = 128
NEG_BIG = -1e30
VMEM_LIMIT = 48 * 1024 * 1024

FFN_BLOCK = 512
FFN_GATHER_SLOTS = 4
MOE_TILE = 512
RUN_ALIGN = 8


def _sorted_rows(tile_tokens):
    return 2 * tile_tokens + N_EXPERTS * RUN_ALIGN


def _dot(a, b):
    return jnp.dot(a, b, preferred_element_type=F32)


def _split_bf16(a):
    hi = a.astype(BF16)
    lo = (a - hi.astype(F32)).astype(BF16)
    return hi, lo


_HI16 = 0xFFFF0000


def _pack_halves(v):
    w = v.shape[1] // 2
    lo = lax.bitcast_convert_type(v[:, :w], jnp.uint32) >> 16
    hi = lax.bitcast_convert_type(v[:, w:], jnp.uint32) & jnp.uint32(_HI16)
    return hi | lo


def _unpack_halves(p):
    lo = lax.bitcast_convert_type(p << 16, F32).astype(BF16)
    hi = lax.bitcast_convert_type(p & jnp.uint32(_HI16), F32).astype(BF16)
    return lo, hi


def _params(sem):
    return pltpu.CompilerParams(dimension_semantics=sem, vmem_limit_bytes=VMEM_LIMIT)


def _ada_kernel(c_ref, w_ref, b_ref, o_ref):
    c = c_ref[...]
    s = c / (1.0 + jnp.exp(-c))
    s_hi, s_lo = _split_bf16(s)
    w_hi, w_lo = _split_bf16(w_ref[0])
    o_ref[0] = _dot(s_hi, w_hi) + _dot(s_hi, w_lo) + _dot(s_lo, w_hi) + b_ref[0]


def _ada(cond, w_mod, b_mod):
    depth, d, d6 = w_mod.shape
    rows = cond.shape[0]
    tn = 1536
    return pl.pallas_call(
        _ada_kernel,
        grid=(depth, d6 // tn),
        in_specs=[
            pl.BlockSpec((rows, d), lambda l, j: (0, 0)),
            pl.BlockSpec((1, d, tn), lambda l, j: (l, 0, j)),
            pl.BlockSpec((1, 1, tn), lambda l, j: (l, 0, j)),
        ],
        out_specs=pl.BlockSpec((1, rows, tn), lambda l, j: (l, 0, j)),
        out_shape=jax.ShapeDtypeStruct((depth, rows, d6), F32),
        compiler_params=_params(("arbitrary", "arbitrary")),
        name="ada",
    )(cond, w_mod, b_mod.reshape(depth, 1, d6))


def _modulate(xf, gain, shift, scale):
    ms = jnp.mean(xf * xf, axis=-1, keepdims=True)
    return (xf * lax.rsqrt(ms + EPS)) * (gain * (1.0 + scale)) + shift


def _qkv_kernel(x_ref, sh_ref, sc_ref, gain_ref, w_ref, hg_ref, ebd_ref, cos_ref, sin_ref,
                *out_refs, with_q):
    h = _modulate(x_ref[0], gain_ref[...], sh_ref[0], sc_ref[0])
    qkv = _dot(h.astype(BF16), w_ref[...])
    tm = qkv.shape[0]
    lane = lax.broadcasted_iota(jnp.int32, (tm, LANES), 1)
    first_half = (lane & 16) == 0
    low = lane < HEAD_DIM
    if with_q:
        q_ref, k_ref, v_ref = out_refs
        nq = N_HEADS * HEAD_DIM // LANES
    else:
        k_ref, v_ref = out_refs
        nq = 0
    nk = N_KV_HEADS * HEAD_DIM // LANES
    cos = cos_ref[...]
    sin = sin_ref[...]
    srow = lax.broadcasted_iota(jnp.int32, (LANES, LANES), 0)
    scol = lax.broadcasted_iota(jnp.int32, (LANES, LANES), 1)
    swap = jnp.where(srow == (scol ^ 16), 1.0, 0.0).astype(BF16)

    def norm_rope(c, gain_row):
        ms = _dot((c * c).astype(BF16), ebd_ref[...])
        cn = c * lax.rsqrt(ms + EPS) * gain_row
        partner = _dot(cn.astype(BF16), swap)
        return cn * cos + partner * sin

    for j in range(nq):
        out = norm_rope(qkv[:, j * LANES:(j + 1) * LANES], hg_ref[0:1, :]).astype(BF16)
        q_ref[0, 2 * j] = out[:, :HEAD_DIM]
        q_ref[0, 2 * j + 1] = out[:, HEAD_DIM:]
    for j in range(nk):
        out = norm_rope(qkv[:, (nq + j) * LANES:(nq + j + 1) * LANES], hg_ref[1:2, :]).astype(BF16)
        k_ref[0, 2 * j] = out[:, :HEAD_DIM]
        k_ref[0, 2 * j + 1] = out[:, HEAD_DIM:]
    for j in range(nk):
        c = qkv[:, (nq + nk + j) * LANES:(nq + nk + j + 1) * LANES]
        v_ref[0, 2 * j] = jnp.where(low, c, 1.0).astype(BF16)
        v_ref[0, 2 * j + 1] = jnp.where(low, pltpu.roll(c, HEAD_DIM, 1), 1.0).astype(BF16)


def _qkv(x, shift, scale, gain, w, head_gain, ebd, cos, sin, *, with_q, per_batch_mods):
    b, n, d = x.shape
    tm = min(512, n)
    ncol = w.shape[1]
    mod_idx = (lambda bi, i: (bi, 0, 0)) if per_batch_mods else (lambda bi, i: (0, 0, 0))
    out_shape = []
    out_specs = []
    if with_q:
        out_shape.append(jax.ShapeDtypeStruct((b, N_HEADS, n, HEAD_DIM), BF16))
        out_specs.append(pl.BlockSpec((1, N_HEADS, tm, HEAD_DIM), lambda bi, i: (bi, 0, i, 0)))
    out_shape.append(jax.ShapeDtypeStruct((b, N_KV_HEADS, n, HEAD_DIM), BF16))
    out_specs.append(pl.BlockSpec((1, N_KV_HEADS, tm, HEAD_DIM), lambda bi, i: (bi, 0, i, 0)))
    out_shape.append(jax.ShapeDtypeStruct((b, N_KV_HEADS, n, LANES), BF16))
    out_specs.append(pl.BlockSpec((1, N_KV_HEADS, tm, LANES), lambda bi, i: (bi, 0, i, 0)))
    return pl.pallas_call(
        functools.partial(_qkv_kernel, with_q=with_q),
        grid=(b, n // tm),
        in_specs=[
            pl.BlockSpec((1, tm, d), lambda bi, i: (bi, i, 0)),
            pl.BlockSpec((1, 1, d), mod_idx),
            pl.BlockSpec((1, 1, d), mod_idx),
            pl.BlockSpec((1, d), lambda bi, i: (0, 0)),
            pl.BlockSpec((d, ncol), lambda bi, i: (0, 0)),
            pl.BlockSpec((2, LANES), lambda bi, i: (0, 0)),
            pl.BlockSpec((LANES, LANES), lambda bi, i: (0, 0)),
            pl.BlockSpec((tm, LANES), lambda bi, i: (i, 0)),
            pl.BlockSpec((tm, LANES), lambda bi, i: (i, 0)),
        ],
        out_specs=out_specs,
        out_shape=out_shape,
        compiler_params=_params(("arbitrary", "arbitrary")),
        name="qkv" if with_q else "ctx_kv",
    )(x, shift, scale, gain, w, head_gain, ebd, cos, sin)


def _attn_kernel(q_ref, k_ref, v_ref, kc_ref, vc_ref, o_ref, *, tq, tk, n_sub):
    chunks = []
    for kr, vr in ((k_ref, v_ref), (kc_ref, vc_ref)):
        size = min(tk, kr.shape[2])
        chunks += [(kr, vr, c, size) for c in range(kr.shape[2] // size)]
    sub = tq // n_sub
    m_rows = GQA_GROUP * sub
    for u in range(n_sub):
        q = q_ref[0, :, u * sub:(u + 1) * sub, :].reshape(m_rows, HEAD_DIM)
        m = jnp.full((m_rows, 1), -jnp.inf, F32)
        acc = jnp.zeros((m_rows, LANES), F32)
        for kr, vr, c, size in chunks:
            kc = kr[0, 0, c * size:(c + 1) * size, :]
            vc = vr[0, 0, c * size:(c + 1) * size, :]
            s = lax.dot_general(q, kc, (((1,), (1,)), ((), ())), preferred_element_type=F32)
            m_new = jnp.maximum(m, jnp.max(s, axis=-1, keepdims=True))
            alpha = jnp.exp2(m - m_new)
            p = jnp.exp2(s - m_new)
            acc = alpha * acc + _dot(p.astype(BF16), vc)
            m = m_new
        o = acc * (1.0 / pltpu.roll(acc, HEAD_DIM, 1))
        o_ref[0, u * sub:(u + 1) * sub, :] = jnp.concatenate(
            [o[g * sub:(g + 1) * sub, :HEAD_DIM] for g in range(GQA_GROUP)], axis=-1).astype(BF16)


def _attention(q, k, v, k_ctx, v_ctx):
    b, _, n, _ = q.shape
    n_ctx = k_ctx.shape[2]
    tq = min(512, n)
    tk = 256
    assert n % min(tk, n) == 0 and n_ctx % min(tk, n_ctx) == 0
    kv_spec = lambda rows, width: pl.BlockSpec((1, 1, rows, width), lambda bi, kh, qi: (bi, kh, 0, 0))
    return pl.pallas_call(
        functools.partial(_attn_kernel, tq=tq, tk=tk, n_sub=tq // min(256, tq)),
        grid=(b, N_KV_HEADS, n // tq),
        in_specs=[
            pl.BlockSpec((1, GQA_GROUP, tq, HEAD_DIM), lambda bi, kh, qi: (bi, kh, qi, 0)),
            kv_spec(n, HEAD_DIM), kv_spec(n, LANES), kv_spec(n_ctx, HEAD_DIM), kv_spec(n_ctx, LANES),
        ],
        out_specs=pl.BlockSpec((1, tq, GQA_GROUP * HEAD_DIM), lambda bi, kh, qi: (bi, qi, kh)),
        out_shape=jax.ShapeDtypeStruct((b, n, N_HEADS * HEAD_DIM), BF16),
        compiler_params=_params(("arbitrary", "arbitrary", "arbitrary")),
        name="attn",
    )(q, k, v, k_ctx, v_ctx)


def _proj_moe_kernel(*refs, tiles_per_batch, folded):
    if folded:
        dif_ref, sum_ref, sumb_ref, *refs, a_scr = refs
        j = pl.program_id(0) % tiles_per_batch

        @pl.when(j < tiles_per_batch // 2)
        def _():
            a_scr[...] = dif_ref[0]

        @pl.when(j >= tiles_per_batch // 2)
        def _():
            rows = _dot(_reversal(a_scr.shape[0]), sum_ref[0])
            row = lax.broadcasted_iota(jnp.int32, rows.shape, 0)
            a_scr[...] = jnp.where(row == 0, sumb_ref[0, 0:1, :].astype(F32), rows).astype(BF16)

        a = a_scr[...]
    else:
        a_ref, *refs = refs
        a = a_ref[...]
    (wo_ref, x_ref, g1_ref, gain_ref, sh_ref, sc_ref, wrh_ref, wrl_ref, br_ref,
     x1_ref, s_ref, gp_ref, pc_ref) = refs
    x1 = x_ref[...] + g1_ref[0] * _dot(a, wo_ref[...])
    x1_ref[...] = x1
    h2 = _modulate(x1, gain_ref[...], sh_ref[0], sc_ref[0])
    h_hi, h_lo = _split_bf16(h2)
    r_hi = _dot(h_hi, wrl_ref[...])
    logits = (r_hi[:, :LANES] + _dot(h_lo, wrh_ref[...]) + r_hi[:, LANES:]
              + br_ref[...])
    tm = logits.shape[0]
    lane = lax.broadcasted_iota(jnp.int32, (tm, LANES), 1)
    lane_f = lane.astype(F32)

    def top(vals):
        best = jnp.max(vals, axis=-1, keepdims=True)
        idx = jnp.min(jnp.where(vals == best, lane_f, float(LANES)), axis=-1, keepdims=True)
        return best, idx

    lg = jnp.where(lane < N_GROUPS, logits, NEG_BIG)
    g_max, g_sel = top(lg)
    p_grp = 1.0 / jnp.sum(jnp.where(lane < N_GROUPS, jnp.exp(lg - g_max), 0.0), axis=-1, keepdims=True)
    lo = N_GROUPS + EXPERTS_PER_GROUP * g_sel
    le = jnp.where((lane_f >= lo) & (lane_f < lo + EXPERTS_PER_GROUP), logits, NEG_BIG)
    t1, i1 = top(le)
    t2, i2 = top(jnp.where(lane_f == i1, NEG_BIG, le))
    e21 = jnp.exp(t2 - t1)
    w1 = 1.0 / (1.0 + e21)
    gate0 = p_grp * w1
    gate1 = p_grp * (e21 * w1)
    e0 = i1 - N_GROUPS
    e1 = i2 - N_GROUPS

    onehot = jnp.where((lane_f == e0) | (lane_f == e1), 1.0, 0.0)
    row = lax.broadcasted_iota(jnp.int32, (tm, tm), 0)
    col = lax.broadcasted_iota(jnp.int32, (tm, tm), 1)
    lower = jnp.where(col < row, 1.0, 0.0).astype(BF16)
    before = _dot(lower, onehot.astype(BF16))
    cnt = jnp.sum(onehot, axis=0, keepdims=True)
    pc = jnp.floor((cnt + (RUN_ALIGN - 1)) * (1.0 / RUN_ALIGN))
    urow = lax.broadcasted_iota(jnp.int32, (LANES, LANES), 0)
    ucol = lax.broadcasted_iota(jnp.int32, (LANES, LANES), 1)
    upper = jnp.where(urow < ucol, 1.0, 0.0).astype(BF16)
    run_start = _dot(jnp.broadcast_to(pc, (8, LANES)).astype(BF16), upper)[0:1] * RUN_ALIGN
    base = run_start + before
    pos0 = jnp.sum(jnp.where(lane_f == e0, base, 0.0), axis=-1, keepdims=True)
    pos1 = jnp.sum(jnp.where(lane_f == e1, base, 0.0), axis=-1, keepdims=True)
    gp = jnp.where(lane == 0, gate0, jnp.where(lane == 1, gate1, jnp.where(lane == 2, pos0,
                                                                       jnp.where(lane == 3, pos1, 0.0))))
    gp_ref[...] = gp
    pc_ref[...] = jnp.broadcast_to(pc, (8, LANES)).astype(jnp.int32)

    pos_t = jnp.where(lane == 0, pos0, jnp.where(lane == 1, pos1, -1.0)).T
    sr = s_ref.shape[0]
    srow = lax.broadcasted_iota(jnp.int32, (sr, tm), 0).astype(F32)
    perm = jnp.where((srow == pos_t[0:1, :]) | (srow == pos_t[1:2, :]), 1.0, 0.0).astype(BF16)
    s_ref[...] = _pack_halves(_dot(perm, h_hi))


def _proj_moe(a, w_o, x, g1, gain, shift, scale, wr_hi, wr_lo, b_r, *, tokens_per_batch):
    t, d = x.shape
    tm = min(MOE_TILE, tokens_per_batch)
    nt = t // tm
    sr = _sorted_rows(tm)
    per_b = tokens_per_batch // tm
    mod_idx = lambda i: (i // per_b, 0, 0)
    row_spec = lambda w: pl.BlockSpec((tm, w), lambda i: (i, 0))
    full = lambda s: pl.BlockSpec(s, lambda i: (0,) * len(s))
    folded = isinstance(a, tuple)
    if folded:
        assert per_b % 2 == 0
        half = per_b // 2
        a_inputs = (a[0], a[1], a[1])
        a_specs = [
            pl.BlockSpec((1, tm, d), lambda i: (i // per_b, jnp.minimum(i % per_b, half - 1), 0)),
            pl.BlockSpec((1, tm, d), lambda i: (i // per_b, jnp.clip(per_b - 1 - i % per_b, 0, half - 1), 0)),
            pl.BlockSpec((1, 8, d), lambda i: (i // per_b, (tm // 8) * jnp.clip(per_b - i % per_b, 1, half), 0)),
        ]
        scratch = [pltpu.VMEM((tm, d), BF16)]
    else:
        a_inputs = (a,)
        a_specs = [row_spec(d)]
        scratch = []
    return pl.pallas_call(
        functools.partial(_proj_moe_kernel, tiles_per_batch=per_b, folded=folded),
        grid=(nt,),
        in_specs=a_specs + [
            full((d, d)), row_spec(d),
            pl.BlockSpec((1, 1, d), mod_idx), full((1, d)),
            pl.BlockSpec((1, 1, d), mod_idx), pl.BlockSpec((1, 1, d), mod_idx),
            full((d, LANES)), full((d, 2 * LANES)), full((1, LANES)),
        ],
        out_specs=[row_spec(d), pl.BlockSpec((sr, d // 2), lambda i: (i, 0)), row_spec(LANES),
                   pl.BlockSpec((8, LANES), lambda i: (i, 0))],
        out_shape=[
            jax.ShapeDtypeStruct((t, d), F32), jax.ShapeDtypeStruct((nt * sr, d // 2), jnp.uint32),
            jax.ShapeDtypeStruct((t, LANES), F32), jax.ShapeDtypeStruct((nt * 8, LANES), jnp.int32),
        ],
        scratch_shapes=scratch,
        compiler_params=_params(("arbitrary",)),
        name="proj_moe",
    )(*a_inputs, w_o, x, g1, gain, shift, scale, wr_hi, wr_lo, b_r)


def _rows(ref, start_units, n_units):
    return ref.at[pl.ds(pl.multiple_of(start_units * RUN_ALIGN, RUN_ALIGN), n_units * RUN_ALIGN)]


def _ffn_kernel(exp_nblk, exp_blk0, blk_e, blk_b0, blk_n, blk_tlo, blk_thi, seg_start, run_start,
                nused_ref, s_hbm, w1_ref, w3_ref, w2_ref, y_hbm, xg, yo, w1b, w3b, w2b, sem, osem,
                *, sr_units, nblk):
    e_step = pl.program_id(0)
    nused = nused_ref[0]
    n_slots, bm, _ = xg.shape
    n_ahead = n_slots - 1
    bm_units = bm // RUN_ALIGN

    def gather(b, slot):
        e = blk_e[b]
        b0 = blk_b0[b]

        def body(ti, carry):
            rs = seg_start[ti * N_EXPERTS + e]
            re = seg_start[(ti + 1) * N_EXPERTS + e]
            lo = jnp.maximum(rs, b0)
            n = jnp.minimum(re, b0 + bm_units) - lo

            @pl.when(n > 0)
            def _():
                src = ti * sr_units + run_start[ti * N_EXPERTS + e] + (lo - rs)
                pltpu.make_async_copy(_rows(s_hbm, src, n), _rows(xg.at[slot], lo - b0, n),
                                      sem.at[slot]).start()

            return carry

        lax.fori_loop(blk_tlo[b], blk_thi[b], body, 0)

    def write_back(g, slot):
        return pltpu.make_async_copy(yo.at[slot], y_hbm.at[pl.ds(pl.multiple_of(g * bm, bm), bm)],
                                     osem.at[slot])

    def emit(g, fill):
        slot = g % 2

        @pl.when(g >= 2)
        def _():
            write_back(g - 2, slot).wait()

        fill(slot)
        write_back(g, slot).start()

    @pl.when(e_step == 0)
    def _():
        xg[...] = jnp.zeros_like(xg)
        for g0 in range(n_ahead):

            @pl.when(g0 < nused)
            def _():
                gather(g0, g0)

    w1b[...] = w1_ref[0, 0].astype(BF16)
    w3b[...] = w3_ref[0, 0].astype(BF16)
    w2b[...] = w2_ref[0, 0].astype(BF16)

    def block(jb, carry):
        g = exp_blk0[e_step] + jb

        @pl.when(g + n_ahead < nused)
        def _():
            gather(g + n_ahead, (g + n_ahead) % n_slots)

        def fill(slot):
            gslot = g % n_slots
            pltpu.make_async_copy(_rows(s_hbm, 0, blk_n[g]), _rows(xg.at[gslot], 0, blk_n[g]),
                                  sem.at[gslot]).wait()
            x_lo, x_hi = _unpack_halves(xg[gslot])
            half = x_lo.shape[1]
            a = _dot(x_lo, w1b[:half, :]) + _dot(x_hi, w1b[half:, :])
            b = _dot(x_lo, w3b[:half, :]) + _dot(x_hi, w3b[half:, :])
            mid = (a / (1.0 + jnp.exp(-a))) * b
            y = _dot(mid.astype(BF16), w2b[...])
            yo[slot] = _pack_halves(y.astype(BF16).astype(F32))

        emit(g, fill)
        return carry

    lax.fori_loop(0, exp_nblk[e_step], block, 0)

    @pl.when(e_step == pl.num_programs(0) - 1)
    def _():
        def tail(g, carry):
            def fill(slot):
                yo[slot] = jnp.zeros(yo.shape[1:], yo.dtype)

            emit(g, fill)
            return carry

        lax.fori_loop(nused, nblk, tail, 0)
        write_back(nblk - 2, nblk % 2).wait()
        write_back(nblk - 1, (nblk - 1) % 2).wait()


def _ffn(tables, s_tiles, w1, w3, w2, *, layer, nblk, sr):
    bm = FFN_BLOCK
    d = w1.shape[2]
    de = w1.shape[3]
    nsp = len(tables)
    assert nblk >= 2
    grid_spec = pltpu.PrefetchScalarGridSpec(
        num_scalar_prefetch=nsp,
        grid=(N_EXPERTS,),
        in_specs=[
            pl.BlockSpec(memory_space=pl.ANY),
            pl.BlockSpec((1, 1, d, de), lambda e, *_: (layer, e, 0, 0)),
            pl.BlockSpec((1, 1, d, de), lambda e, *_: (layer, e, 0, 0)),
            pl.BlockSpec((1, 1, de, d), lambda e, *_: (layer, e, 0, 0)),
        ],
        out_specs=pl.BlockSpec(memory_space=pl.ANY),
        scratch_shapes=[pltpu.VMEM((FFN_GATHER_SLOTS, bm, d // 2), jnp.uint32),
                        pltpu.VMEM((2, bm, d // 2), jnp.uint32),
                        pltpu.VMEM((d, de), BF16), pltpu.VMEM((d, de), BF16), pltpu.VMEM((de, d), BF16),
                        pltpu.SemaphoreType.DMA((FFN_GATHER_SLOTS,)), pltpu.SemaphoreType.DMA((2,))],
    )
    return pl.pallas_call(
        functools.partial(_ffn_kernel, sr_units=sr // RUN_ALIGN, nblk=nblk),
        grid_spec=grid_spec,
        out_shape=jax.ShapeDtypeStruct((nblk * bm, d // 2), jnp.uint32),
        compiler_params=_params(("arbitrary",)),
        name="ffn",
    )(*tables, s_tiles, w1, w3, w2)


def _combine_kernel(run_len, run_start, seg_start, exp_start, tile_rows,
                    y_hbm, x_ref, g2_ref, gp_ref, fgain_ref, o_ref, ybuf, sem, *, nsteps, final):
    i = pl.program_id(0)

    def gather(ti, slot):
        for e in range(N_EXPERTS):
            n = run_len[ti * N_EXPERTS + e]

            @pl.when(n > 0)
            def _():
                src = exp_start[e] + seg_start[ti * N_EXPERTS + e]
                pltpu.make_async_copy(_rows(y_hbm, src, n),
                                      _rows(ybuf.at[slot], run_start[ti * N_EXPERTS + e], n),
                                      sem.at[slot]).start()

    @pl.when(i == 0)
    def _():
        ybuf[...] = jnp.zeros_like(ybuf)
        gather(0, 0)

    @pl.when(i + 1 < nsteps)
    def _():
        gather(i + 1, (i + 1) % 2)

    slot = i % 2
    pltpu.make_async_copy(_rows(y_hbm, 0, tile_rows[i]), _rows(ybuf.at[slot], 0, tile_rows[i]),
                          sem.at[slot]).wait()
    gp = gp_ref[...]
    tm = gp.shape[0]
    sr = ybuf.shape[1]
    col = lax.broadcasted_iota(jnp.int32, (tm, sr), 1).astype(F32)
    weights = (jnp.where(col == gp[:, 2:3], gp[:, 0:1], 0.0)
               + jnp.where(col == gp[:, 3:4], gp[:, 1:2], 0.0)).astype(BF16)
    y_lo, y_hi = _unpack_halves(ybuf[slot])
    y = jnp.concatenate([_dot(weights, y_lo), _dot(weights, y_hi)], axis=-1)
    x2 = x_ref[...] + g2_ref[0] * y
    if final:
        ms = jnp.mean(x2 * x2, axis=-1, keepdims=True)
        x2 = (x2 * lax.rsqrt(ms + EPS)) * fgain_ref[...]
    o_ref[...] = x2


def _combine(tables, yb, x1, g2, gp, fgain, *, tokens_per_batch, final):
    t, d = x1.shape
    tm = min(MOE_TILE, tokens_per_batch)
    nsteps = t // tm
    per_b = tokens_per_batch // tm
    sr = _sorted_rows(tm)
    grid_spec = pltpu.PrefetchScalarGridSpec(
        num_scalar_prefetch=len(tables),
        grid=(nsteps,),
        in_specs=[
            pl.BlockSpec(memory_space=pl.ANY),
            pl.BlockSpec((tm, d), lambda i, *_: (i, 0)),
            pl.BlockSpec((1, 1, d), lambda i, *_: (i // per_b, 0, 0)),
            pl.BlockSpec((tm, LANES), lambda i, *_: (i, 0)),
            pl.BlockSpec((1, d), lambda i, *_: (0, 0)),
        ],
        out_specs=pl.BlockSpec((tm, d), lambda i, *_: (i, 0)),
        scratch_shapes=[pltpu.VMEM((2, sr, d // 2), jnp.uint32), pltpu.SemaphoreType.DMA((2,))],
    )
    return pl.pallas_call(
        functools.partial(_combine_kernel, nsteps=nsteps, final=final),
        grid_spec=grid_spec,
        out_shape=jax.ShapeDtypeStruct((t, d), F32),
        compiler_params=_params(("arbitrary",)),
        name="combine_final" if final else "combine",
    )(*tables, yb, x1, g2, gp, fgain)


def _reversal(size):
    row = lax.broadcasted_iota(jnp.int32, (size, size), 0)
    col = lax.broadcasted_iota(jnp.int32, (size, size), 1)
    return jnp.where(col == size - row, 1.0, 0.0).astype(BF16)


def _fmix_fold_kernel(x_ref, xm_ref, xb_ref, xc_ref, sh_ref, sc_ref, gain_ref, cc_ref, sc_mat_ref,
                      a_ref, b_ref, amid_ref):
    mod = lambda v: _modulate(v, gain_ref[...], sh_ref[0], sc_ref[0])
    h = mod(x_ref[...])
    ft = h.shape[0]
    mirrored = mod(xm_ref[...]).astype(BF16)
    row = lax.broadcasted_iota(jnp.int32, h.shape, 0)
    hr = jnp.where(row == 0, mod(xb_ref[...])[0:1], _dot(_reversal(ft), mirrored))
    he = (h + hr).astype(BF16)
    ho = (h - hr).astype(BF16)
    hc = mod(xc_ref[...]).astype(BF16)
    gw = cc_ref.shape[0]
    for g in range(FOURIER_GROUPS):
        cols = slice(g * gw, (g + 1) * gw)
        a_ref[:, cols] = _dot(he[:, cols], cc_ref[...]).astype(BF16)
        b_ref[:, cols] = _dot(ho[:, cols], sc_mat_ref[...]).astype(BF16)
        amid_ref[:, cols] = _dot(hc[:, cols], cc_ref[...])


def _fmix_fold(x, shift, scale, gain, cc, sc, *, bsz, n):
    t, d = x.shape
    hn = n // 2
    ft = min(512, hn)
    tpb = n // ft
    half = tpb // 2
    gw = d // FOURIER_GROUPS
    mod_idx = lambda b, j: (b, 0, 0)
    return pl.pallas_call(
        _fmix_fold_kernel,
        grid=(bsz, half),
        in_specs=[
            pl.BlockSpec((ft, d), lambda b, j: (b * tpb + j, 0)),
            pl.BlockSpec((ft, d), lambda b, j: (b * tpb + tpb - 1 - j, 0)),
            pl.BlockSpec((8, d), lambda b, j: (b * (n // 8) + (ft // 8) * ((tpb - j) % tpb), 0)),
            pl.BlockSpec((8, d), lambda b, j: (b * (n // 8) + hn // 8, 0)),
            pl.BlockSpec((1, 1, d), mod_idx), pl.BlockSpec((1, 1, d), mod_idx),
            pl.BlockSpec((1, d), lambda b, j: (0, 0)),
            pl.BlockSpec((gw, gw), lambda b, j: (0, 0)), pl.BlockSpec((gw, gw), lambda b, j: (0, 0)),
        ],
        out_specs=[pl.BlockSpec((ft, d), lambda b, j: (b * half + j, 0)),
                   pl.BlockSpec((ft, d), lambda b, j: (b * half + j, 0)),
                   pl.BlockSpec((8, d), lambda b, j: (b, 0))],
        out_shape=[jax.ShapeDtypeStruct((bsz * hn, d), BF16), jax.ShapeDtypeStruct((bsz * hn, d), BF16),
                   jax.ShapeDtypeStruct((bsz * 8, d), F32)],
        compiler_params=_params(("arbitrary", "arbitrary")),
        name="fmix_fold",
    )(x, x, x, x, shift, scale, gain, cc, sc)


def _fmix_pos_kernel(ch_ref, sh_ref, sign_ref, a_ref, b_ref, amid_ref, dif_ref, sum_ref):
    p = _dot(ch_ref[...], a_ref[0]) + sign_ref[...] * amid_ref[0, 0:1, :]
    q = _dot(sh_ref[...], b_ref[0])
    dif_ref[0] = (p - q).astype(BF16)
    sum_ref[0] = (p + q).astype(BF16)


def _fmix_pos(ch, sh, sign, a, b, amid):
    bsz, hn, d = a.shape
    mp = ch.shape[0]
    tmr = mp // DFT_ROW_BLOCKS
    assert tmr % 16 == 0
    out = jax.ShapeDtypeStruct((bsz, mp, d), BF16)
    return pl.pallas_call(
        _fmix_pos_kernel,
        grid=(bsz, DFT_ROW_BLOCKS),
        in_specs=[
            pl.BlockSpec((tmr, hn), lambda bi, i: (i, 0)),
            pl.BlockSpec((tmr, hn), lambda bi, i: (i, 0)),
            pl.BlockSpec((tmr, 1), lambda bi, i: (i, 0)),
            pl.BlockSpec((1, hn, d), lambda bi, i: (bi, 0, 0)),
            pl.BlockSpec((1, hn, d), lambda bi, i: (bi, 0, 0)),
            pl.BlockSpec((1, 8, d), lambda bi, i: (bi, 0, 0)),
        ],
        out_specs=[pl.BlockSpec((1, tmr, d), lambda bi, i: (bi, i, 0)),
                   pl.BlockSpec((1, tmr, d), lambda bi, i: (bi, i, 0))],
        out_shape=[out, out],
        compiler_params=_params(("arbitrary", "arbitrary")),
        name="fmix_pos",
    )(ch, sh, sign, a, b, amid)


def _rope_tables(n):
    half = ROPE_AXIS_DIM // 2
    rows = n // GRID_W
    inv = ROPE_THETA ** (-jnp.arange(half, dtype=F32) / half)
    pos = jnp.arange(n, dtype=jnp.int32)
    ang_r = (pos // GRID_W).astype(F32)[:, None] * inv
    ang_c = (pos % GRID_W).astype(F32)[:, None] * inv
    del rows
    cos = jnp.concatenate([jnp.cos(ang_r)] * 2 + [jnp.cos(ang_c)] * 2, axis=-1)
    sin = jnp.concatenate([-jnp.sin(ang_r), jnp.sin(ang_r), -jnp.sin(ang_c), jnp.sin(ang_c)], axis=-1)
    return jnp.tile(cos, (1, LANES // HEAD_DIM)), jnp.tile(sin, (1, LANES // HEAD_DIM))


def _dft_mats(n, scale):
    r = 1
    while r * r < n:
        r *= 2
    c = n // r
    k = jnp.arange(n, dtype=jnp.int32)[:, None]
    ang_a = ((k * jnp.arange(r, dtype=jnp.int32)[None, :] * c) % n).astype(F32) * (2.0 * math.pi / n)
    ang_b = ((k * jnp.arange(c, dtype=jnp.int32)[None, :]) % n).astype(F32) * (2.0 * math.pi / n)
    ca, sa, cb, sb = jnp.cos(ang_a), jnp.sin(ang_a), jnp.cos(ang_b), jnp.sin(ang_b)
    cosm = (ca[:, :, None] * cb[:, None, :] - sa[:, :, None] * sb[:, None, :]).reshape(n, n)
    sinm = (sa[:, :, None] * cb[:, None, :] + ca[:, :, None] * sb[:, None, :]).reshape(n, n)
    return (cosm * scale).astype(BF16), (sinm * scale).astype(BF16)


DFT_ROW_BLOCKS = 4
DFT_ROW_PAD = 64


def _dft_half(n):
    hn = n // 2
    mp = hn + DFT_ROW_PAD
    kb = 32
    ka = mp // kb
    assert ka * kb == mp
    m = jnp.arange(hn, dtype=jnp.int32)[None, :]
    w = 2.0 * math.pi / n
    ang_a = (((jnp.arange(ka, dtype=jnp.int32)[:, None] * kb) * m) % n).astype(F32) * w
    ang_b = ((jnp.arange(kb, dtype=jnp.int32)[:, None] * m) % n).astype(F32) * w
    ca, sa, cb, sb = jnp.cos(ang_a), jnp.sin(ang_a), jnp.cos(ang_b), jnp.sin(ang_b)
    cosm = (ca[:, None, :] * cb[None, :, :] - sa[:, None, :] * sb[None, :, :]).reshape(mp, hn)
    sinm = (sa[:, None, :] * cb[None, :, :] + ca[:, None, :] * sb[None, :, :]).reshape(mp, hn)
    k = jnp.arange(mp, dtype=jnp.int32)[:, None]
    valid = k <= hn
    scale = n ** -0.5
    col_w = jnp.where(m == 0, 0.5 * scale, scale)
    cosm = jnp.where(valid, cosm * col_w, 0.0).astype(BF16)
    sinm = jnp.where(valid, sinm * scale, 0.0).astype(BF16)
    sign = jnp.where(valid, jnp.where(k % 2 == 0, scale, -scale), 0.0).astype(F32)
    return cosm, sinm, sign


def _moe(a, w_o, x, g1, g2, gain, shift, scale, w_rg, b_rg, w_re, b_re, w1, w3, w2, fgain, *,
         layer, tokens_per_batch, final):
    t, d = x.shape
    pad = LANES - N_GROUPS - N_EXPERTS
    w_r = jnp.concatenate([w_rg, w_re, jnp.zeros((d, pad), F32)], axis=1)
    b_r = jnp.concatenate([b_rg, b_re, jnp.zeros((pad,), F32)])[None, :]
    wr_hi = w_r.astype(BF16)
    wr_lo = jnp.concatenate([wr_hi, (w_r - wr_hi.astype(F32)).astype(BF16)], axis=1)
    x1, s_tiles, gp, pc = _proj_moe(a, w_o, x, g1, gain, shift, scale, wr_hi, wr_lo, b_r,
                                    tokens_per_batch=tokens_per_batch)
    tm = min(MOE_TILE, tokens_per_batch)
    nt = t // tm
    sr = _sorted_rows(tm)
    bmu = FFN_BLOCK // RUN_ALIGN
    i32 = jnp.int32
    run_len = pc.reshape(nt, 8, LANES)[:, 0, :N_EXPERTS]
    seg_start = jnp.concatenate([jnp.zeros((1, N_EXPERTS), i32), jnp.cumsum(run_len, axis=0)])
    seg_len = seg_start[-1]
    seg_pad = (seg_len + bmu - 1) // bmu * bmu
    exp_end = jnp.cumsum(seg_pad)
    exp_start = exp_end - seg_pad
    run_start = jnp.cumsum(run_len, axis=1) - run_len
    tile_rows = run_start[:, -1] + run_len[:, -1]
    max_units = (2 * t) // RUN_ALIGN + nt * N_EXPERTS + N_EXPERTS * (bmu - 1)
    nblk = (max_units + bmu - 1) // bmu
    blk_start = jnp.arange(nblk, dtype=i32) * bmu
    blk_e = jnp.minimum(jnp.sum((exp_end[None, :] <= blk_start[:, None]).astype(i32), axis=1),
                        N_EXPERTS - 1)
    pick = (blk_e[:, None] == jnp.arange(N_EXPERTS, dtype=i32)[None, :]).astype(i32)
    blk_b0 = blk_start - jnp.sum(pick * exp_start[None, :], axis=1)
    blk_n = jnp.clip(jnp.sum(pick * seg_len[None, :], axis=1) - blk_b0, 0, bmu)
    seg_e = jnp.sum(pick[:, None, :] * seg_start[None, :, :], axis=2)
    blk_tlo = jnp.sum((seg_e[:, 1:] <= blk_b0[:, None]).astype(i32), axis=1)
    blk_thi = jnp.sum((seg_e[:, :-1] < (blk_b0 + bmu)[:, None]).astype(i32), axis=1)
    nused = (exp_end[-1] // bmu)[None]
    flat = lambda v: v.reshape(-1).astype(i32)
    yb = _ffn((seg_pad // bmu, exp_start // bmu, blk_e, blk_b0, blk_n, blk_tlo, blk_thi,
               flat(seg_start), flat(run_start), nused),
              s_tiles, w1, w3, w2, layer=layer, nblk=nblk, sr=sr)
    return _combine((flat(run_len), flat(run_start), flat(seg_start), exp_start, tile_rows),
                    yb, x1, g2, gp, fgain, tokens_per_batch=tokens_per_batch, final=final)


def kernel(x, c, ctx, c_ctx, w_mod, b_mod, norm_mix, norm_ffn, attn_w_qkv, attn_q_norm, attn_k_norm,
           attn_w_o, fourier_w_o, moe_w_rg, moe_b_rg, moe_w_re, moe_b_re, moe_w1, moe_w3, moe_w2,
           final_norm):
    bsz, n, d = x.shape
    t = bsz * n
    q_dim = N_HEADS * HEAD_DIM
    kv_dim = N_KV_HEADS * HEAD_DIM

    cond = jnp.zeros((16, d), F32).at[:bsz].set(c).at[bsz].set(c_ctx)
    mods = _ada(cond, w_mod, b_mod)

    def lat(layer, j):
        return mods[layer, :bsz, j * d:(j + 1) * d][:, None, :]

    def ctxm(layer, j):
        return mods[layer, bsz:bsz + 1, j * d:(j + 1) * d][:, None, :]

    w_qkv = attn_w_qkv[0].astype(BF16)
    qscale = (HEAD_DIM ** -0.5) * math.log2(math.e)
    head_gain = jnp.stack([jnp.tile(attn_q_norm[0] * qscale, LANES // HEAD_DIM),
                           jnp.tile(attn_k_norm[0], LANES // HEAD_DIM)])
    lane = jnp.arange(LANES)
    ebd = jnp.where((lane[:, None] // HEAD_DIM) == (lane[None, :] // HEAD_DIM),
                    1.0 / HEAD_DIM, 0.0).astype(BF16)
    cos, sin = _rope_tables(n)
    gain0 = norm_mix[0][None, :]
    q, k_l, v_l = _qkv(x, lat(0, 0), lat(0, 1), gain0, w_qkv, head_gain, ebd, cos, sin,
                       with_q=True, per_batch_mods=True)
    n_ctx = ctx.shape[1]
    k_c, v_c = _qkv(ctx, ctxm(0, 0), ctxm(0, 1), gain0, w_qkv[:, q_dim:], head_gain, ebd,
                    jnp.ones((n_ctx, LANES), F32), jnp.zeros((n_ctx, LANES), F32),
                    with_q=False, per_batch_mods=False)
    del kv_dim
    o = _attention(q, k_l, v_l, k_c, v_c).reshape(t, q_dim)

    xf = x.reshape(t, d)
    moe_w = lambda i: (moe_w_rg[i], moe_b_rg[i], moe_w_re[i], moe_b_re[i],
                       moe_w1, moe_w3, moe_w2)
    fgain = final_norm[None, :]
    x2 = _moe(o, attn_w_o[0].astype(BF16), xf, lat(0, 2), lat(0, 5), norm_ffn[0][None, :],
              lat(0, 3), lat(0, 4), *moe_w(0), fgain, layer=0, tokens_per_batch=n, final=False)

    gw = d // FOURIER_GROUPS
    cc, sc = _dft_mats(gw, gw ** -0.5)
    ch, sh, sign = _dft_half(n)
    a_e, b_o, a_mid = _fmix_fold(x2, lat(1, 0), lat(1, 1), norm_mix[1][None, :], cc, sc, bsz=bsz, n=n)
    f = tuple(_fmix_pos(ch, sh, sign, a_e.reshape(bsz, n // 2, d), b_o.reshape(bsz, n // 2, d),
                        a_mid.reshape(bsz, 8, d)))
    out = _moe(f, fourier_w_o[0].astype(BF16), x2, lat(1, 2), lat(1, 5), norm_ffn[1][None, :],
               lat(1, 3), lat(1, 4), *moe_w(1), fgain, layer=1, tokens_per_batch=n, final=True)
    return out.reshape(bsz, n, d)
```

```python
import functools
import math

import jax
import jax.numpy as jnp
from jax import lax
from jax.experimental import pallas as pl
from jax.experimental.pallas import tpu as pltpu

F32 = jnp.float32
BF16 = jnp.bfloat16

N_HEADS = 16
N_KV_HEADS = 4
HEAD_DIM = 64
GQA_GROUP = N_HEADS // N_KV_HEADS
GRID_W = 64
ROPE_AXIS_DIM = HEAD_DIM // 2
ROPE_THETA = 10000.0
FOURIER_GROUPS = 4
N_GROUPS = 4
EXPERTS_PER_GROUP = 8
N_EXPERTS = N_GROUPS * EXPERTS_PER_GROUP
EPS = 1e-6
LANES = 128
NEG_BIG = -1e30
VMEM_LIMIT = 48 * 1024 * 1024

FFN_BLOCK = 512
FFN_GATHER_SLOTS = 4
MOE_TILE = 512
RUN_ALIGN = 8


def _sorted_rows(tile_tokens):
    return 2 * tile_tokens + N_EXPERTS * RUN_ALIGN


def _dot(a, b):
    return jnp.dot(a, b, preferred_element_type=F32)


def _split_bf16(a):
    hi = a.astype(BF16)
    lo = (a - hi.astype(F32)).astype(BF16)
    return hi, lo


_HI16 = 0xFFFF0000


def _pack_halves(v):
    w = v.shape[1] // 2
    lo = lax.bitcast_convert_type(v[:, :w], jnp.uint32) >> 16
    hi = lax.bitcast_convert_type(v[:, w:], jnp.uint32) & jnp.uint32(_HI16)
    return hi | lo


def _unpack_halves(p):
    lo = lax.bitcast_convert_type(p << 16, F32).astype(BF16)
    hi = lax.bitcast_convert_type(p & jnp.uint32(_HI16), F32).astype(BF16)
    return lo, hi


def _params(sem):
    return pltpu.CompilerParams(dimension_semantics=sem, vmem_limit_bytes=VMEM_LIMIT)


def _ada_kernel(c_ref, w_ref, b_ref, o_ref):
    c = c_ref[...]
    s = c / (1.0 + jnp.exp(-c))
    s_hi, s_lo = _split_bf16(s)
    w_hi, w_lo = _split_bf16(w_ref[0])
    o_ref[0] = _dot(s_hi, w_hi) + _dot(s_hi, w_lo) + _dot(s_lo, w_hi) + b_ref[0]


def _ada(cond, w_mod, b_mod):
    depth, d, d6 = w_mod.shape
    rows = cond.shape[0]
    tn = 1536
    return pl.pallas_call(
        _ada_kernel,
        grid=(depth, d6 // tn),
        in_specs=[
            pl.BlockSpec((rows, d), lambda l, j: (0, 0)),
            pl.BlockSpec((1, d, tn), lambda l, j: (l, 0, j)),
            pl.BlockSpec((1, 1, tn), lambda l, j: (l, 0, j)),
        ],
        out_specs=pl.BlockSpec((1, rows, tn), lambda l, j: (l, 0, j)),
        out_shape=jax.ShapeDtypeStruct((depth, rows, d6), F32),
        compiler_params=_params(("arbitrary", "arbitrary")),
        name="ada",
    )(cond, w_mod, b_mod.reshape(depth, 1, d6))


def _modulate(xf, gain, shift, scale):
    ms = jnp.mean(xf * xf, axis=-1, keepdims=True)
    return (xf * lax.rsqrt(ms + EPS)) * (gain * (1.0 + scale)) + shift


def _qkv_kernel(x_ref, sh_ref, sc_ref, gain_ref, w_ref, hg_ref, ebd_ref, cos_ref, sin_ref,
                *out_refs, with_q):
    h = _modulate(x_ref[0], gain_ref[...], sh_ref[0], sc_ref[0])
    qkv = _dot(h.astype(BF16), w_ref[...])
    tm = qkv.shape[0]
    lane = lax.broadcasted_iota(jnp.int32, (tm, LANES), 1)
    first_half = (lane & 16) == 0
    low = lane < HEAD_DIM
    if with_q:
        q_ref, k_ref, v_ref = out_refs
        nq = N_HEADS * HEAD_DIM // LANES
    else:
        k_ref, v_ref = out_refs
        nq = 0
    nk = N_KV_HEADS * HEAD_DIM // LANES
    cos = cos_ref[...]
    sin = sin_ref[...]
    srow = lax.broadcasted_iota(jnp.int32, (LANES, LANES), 0)
    scol = lax.broadcasted_iota(jnp.int32, (LANES, LANES), 1)
    swap = jnp.where(srow == (scol ^ 16), 1.0, 0.0).astype(BF16)

    def norm_rope(c, gain_row):
        ms = _dot((c * c).astype(BF16), ebd_ref[...])
        cn = c * lax.rsqrt(ms + EPS) * gain_row
        partner = _dot(cn.astype(BF16), swap)
        return cn * cos + partner * sin

    for j in range(nq):
        out = norm_rope(qkv[:, j * LANES:(j + 1) * LANES], hg_ref[0:1, :]).astype(BF16)
        q_ref[0, 2 * j] = out[:, :HEAD_DIM]
        q_ref[0, 2 * j + 1] = out[:, HEAD_DIM:]
    for j in range(nk):
        out = norm_rope(qkv[:, (nq + j) * LANES:(nq + j + 1) * LANES], hg_ref[1:2, :]).astype(BF16)
        k_ref[0, 2 * j] = out[:, :HEAD_DIM]
        k_ref[0, 2 * j + 1] = out[:, HEAD_DIM:]
    for j in range(nk):
        c = qkv[:, (nq + nk + j) * LANES:(nq + nk + j + 1) * LANES]
        v_ref[0, 2 * j] = jnp.where(low, c, 1.0).astype(BF16)
        v_ref[0, 2 * j + 1] = jnp.where(low, pltpu.roll(c, HEAD_DIM, 1), 1.0).astype(BF16)


def _qkv(x, shift, scale, gain, w, head_gain, ebd, cos, sin, *, with_q, per_batch_mods):
    b, n, d = x.shape
    tm = min(512, n)
    ncol = w.shape[1]
    mod_idx = (lambda bi, i: (bi, 0, 0)) if per_batch_mods else (lambda bi, i: (0, 0, 0))
    out_shape = []
    out_specs = []
    if with_q:
        out_shape.append(jax.ShapeDtypeStruct((b, N_HEADS, n, HEAD_DIM), BF16))
        out_specs.append(pl.BlockSpec((1, N_HEADS, tm, HEAD_DIM), lambda bi, i: (bi, 0, i, 0)))
    out_shape.append(jax.ShapeDtypeStruct((b, N_KV_HEADS, n, HEAD_DIM), BF16))
    out_specs.append(pl.BlockSpec((1, N_KV_HEADS, tm, HEAD_DIM), lambda bi, i: (bi, 0, i, 0)))
    out_shape.append(jax.ShapeDtypeStruct((b, N_KV_HEADS, n, LANES), BF16))
    out_specs.append(pl.BlockSpec((1, N_KV_HEADS, tm, LANES), lambda bi, i: (bi, 0, i, 0)))
    return pl.pallas_call(
        functools.partial(_qkv_kernel, with_q=with_q),
        grid=(b, n // tm),
        in_specs=[
            pl.BlockSpec((1, tm, d), lambda bi, i: (bi, i, 0)),
            pl.BlockSpec((1, 1, d), mod_idx),
            pl.BlockSpec((1, 1, d), mod_idx),
            pl.BlockSpec((1, d), lambda bi, i: (0, 0)),
            pl.BlockSpec((d, ncol), lambda bi, i: (0, 0)),
            pl.BlockSpec((2, LANES), lambda bi, i: (0, 0)),
            pl.BlockSpec((LANES, LANES), lambda bi, i: (0, 0)),
            pl.BlockSpec((tm, LANES), lambda bi, i: (i, 0)),
            pl.BlockSpec((tm, LANES), lambda bi, i: (i, 0)),
        ],
        out_specs=out_specs,
        out_shape=out_shape,
        compiler_params=_params(("arbitrary", "arbitrary")),
        name="qkv" if with_q else "ctx_kv",
    )(x, shift, scale, gain, w, head_gain, ebd, cos, sin)


def _attn_kernel(q_ref, k_ref, v_ref, kc_ref, vc_ref, o_ref, *, tq, tk, n_sub):
    chunks = []
    for kr, vr in ((k_ref, v_ref), (kc_ref, vc_ref)):
        size = min(tk, kr.shape[2])
        chunks += [(kr, vr, c, size) for c in range(kr.shape[2] // size)]
    sub = tq // n_sub
    m_rows = GQA_GROUP * sub
    for u in range(n_sub):
        q = q_ref[0, :, u * sub:(u + 1) * sub, :].reshape(m_rows, HEAD_DIM)
        m = jnp.full((m_rows, 1), -jnp.inf, F32)
        acc = jnp.zeros((m_rows, LANES), F32)
        for kr, vr, c, size in chunks:
            kc = kr[0, 0, c * size:(c + 1) * size, :]
            vc = vr[0, 0, c * size:(c + 1) * size, :]
            s = lax.dot_general(q, kc, (((1,), (1,)), ((), ())), preferred_element_type=F32)
            m_new = jnp.maximum(m, jnp.max(s, axis=-1, keepdims=True))
            alpha = jnp.exp2(m - m_new)
            p = jnp.exp2(s - m_new)
            acc = alpha * acc + _dot(p.astype(BF16), vc)
            m = m_new
        o = acc * (1.0 / pltpu.roll(acc, HEAD_DIM, 1))
        o_ref[0, u * sub:(u + 1) * sub, :] = jnp.concatenate(
            [o[g * sub:(g + 1) * sub, :HEAD_DIM] for g in range(GQA_GROUP)], axis=-1).astype(BF16)


def _attention(q, k, v, k_ctx, v_ctx):
    b, _, n, _ = q.shape
    n_ctx = k_ctx.shape[2]
    tq = min(512, n)
    tk = 256
    assert n % min(tk, n) == 0 and n_ctx % min(tk, n_ctx) == 0
    kv_spec = lambda rows, width: pl.BlockSpec((1, 1, rows, width), lambda bi, kh, qi: (bi, kh, 0, 0))
    return pl.pallas_call(
        functools.partial(_attn_kernel, tq=tq, tk=tk, n_sub=tq // min(256, tq)),
        grid=(b, N_KV_HEADS, n // tq),
        in_specs=[
            pl.BlockSpec((1, GQA_GROUP, tq, HEAD_DIM), lambda bi, kh, qi: (bi, kh, qi, 0)),
            kv_spec(n, HEAD_DIM), kv_spec(n, LANES), kv_spec(n_ctx, HEAD_DIM), kv_spec(n_ctx, LANES),
        ],
        out_specs=pl.BlockSpec((1, tq, GQA_GROUP * HEAD_DIM), lambda bi, kh, qi: (bi, qi, kh)),
        out_shape=jax.ShapeDtypeStruct((b, n, N_HEADS * HEAD_DIM), BF16),
        compiler_params=_params(("arbitrary", "arbitrary", "arbitrary")),
        name="attn",
    )(q, k, v, k_ctx, v_ctx)


def _proj_moe_kernel(*refs, tiles_per_batch, folded):
    if folded:
        dif_ref, sum_ref, sumb_ref, *refs, a_scr = refs
        j = pl.program_id(0) % tiles_per_batch

        @pl.when(j < tiles_per_batch // 2)
        def _():
            a_scr[...] = dif_ref[0]

        @pl.when(j >= tiles_per_batch // 2)
        def _():
            rows = _dot(_reversal(a_scr.shape[0]), sum_ref[0])
            row = lax.broadcasted_iota(jnp.int32, rows.shape, 0)
            a_scr[...] = jnp.where(row == 0, sumb_ref[0, 0:1, :].astype(F32), rows).astype(BF16)

        a = a_scr[...]
    else:
        a_ref, *refs = refs
        a = a_ref[...]
    (wo_ref, x_ref, g1_ref, gain_ref, sh_ref, sc_ref, wrh_ref, wrl_ref, br_ref,
     x1_ref, s_ref, gp_ref, pc_ref) = refs
    x1 = x_ref[...] + g1_ref[0] * _dot(a, wo_ref[...])
    x1_ref[...] = x1
    h2 = _modulate(x1, gain_ref[...], sh_ref[0], sc_ref[0])
    h_hi, h_lo = _split_bf16(h2)
    r_hi = _dot(h_hi, wrl_ref[...])
    logits = (r_hi[:, :LANES] + _dot(h_lo, wrh_ref[...]) + r_hi[:, LANES:]
              + br_ref[...])
    tm = logits.shape[0]
    lane = lax.broadcasted_iota(jnp.int32, (tm, LANES), 1)
    lane_f = lane.astype(F32)

    def top(vals):
        best = jnp.max(vals, axis=-1, keepdims=True)
        idx = jnp.min(jnp.where(vals == best, lane_f, float(LANES)), axis=-1, keepdims=True)
        return best, idx

    lg = jnp.where(lane < N_GROUPS, logits, NEG_BIG)
    g_max, g_sel = top(lg)
    p_grp = 1.0 / jnp.sum(jnp.where(lane < N_GROUPS, jnp.exp(lg - g_max), 0.0), axis=-1, keepdims=True)
    lo = N_GROUPS + EXPERTS_PER_GROUP * g_sel
    le = jnp.where((lane_f >= lo) & (lane_f < lo + EXPERTS_PER_GROUP), logits, NEG_BIG)
    t1, i1 = top(le)
    t2, i2 = top(jnp.where(lane_f == i1, NEG_BIG, le))
    e21 = jnp.exp(t2 - t1)
    w1 = 1.0 / (1.0 + e21)
    gate0 = p_grp * w1
    gate1 = p_grp * (e21 * w1)
    e0 = i1 - N_GROUPS
    e1 = i2 - N_GROUPS

    onehot = jnp.where((lane_f == e0) | (lane_f == e1), 1.0, 0.0)
    row = lax.broadcasted_iota(jnp.int32, (tm, tm), 0)
    col = lax.broadcasted_iota(jnp.int32, (tm, tm), 1)
    lower = jnp.where(col < row, 1.0, 0.0).astype(BF16)
    before = _dot(lower, onehot.astype(BF16))
    cnt = jnp.sum(onehot, axis=0, keepdims=True)
    pc = jnp.floor((cnt + (RUN_ALIGN - 1)) * (1.0 / RUN_ALIGN))
    urow = lax.broadcasted_iota(jnp.int32, (LANES, LANES), 0)
    ucol = lax.broadcasted_iota(jnp.int32, (LANES, LANES), 1)
    upper = jnp.where(urow < ucol, 1.0, 0.0).astype(BF16)
    run_start = _dot(jnp.broadcast_to(pc, (8, LANES)).astype(BF16), upper)[0:1] * RUN_ALIGN
    base = run_start + before
    pos0 = jnp.sum(jnp.where(lane_f == e0, base, 0.0), axis=-1, keepdims=True)
    pos1 = jnp.sum(jnp.where(lane_f == e1, base, 0.0), axis=-1, keepdims=True)
    gp = jnp.where(lane == 0, gate0, jnp.where(lane == 1, gate1, jnp.where(lane == 2, pos0,
                                                                       jnp.where(lane == 3, pos1, 0.0))))
    gp_ref[...] = gp
    pc_ref[...] = jnp.broadcast_to(pc, (8, LANES)).astype(jnp.int32)

    pos_t = jnp.where(lane == 0, pos0, jnp.where(lane == 1, pos1, -1.0)).T
    sr = s_ref.shape[0]
    srow = lax.broadcasted_iota(jnp.int32, (sr, tm), 0).astype(F32)
    perm = jnp.where((srow == pos_t[0:1, :]) | (srow == pos_t[1:2, :]), 1.0, 0.0).astype(BF16)
    s_ref[...] = _pack_halves(_dot(perm, h_hi))


def _proj_moe(a, w_o, x, g1, gain, shift, scale, wr_hi, wr_lo, b_r, *, tokens_per_batch):
    t, d = x.shape
    tm = min(MOE_TILE, tokens_per_batch)
    nt = t // tm
    sr = _sorted_rows(tm)
    per_b = tokens_per_batch // tm
    mod_idx = lambda i: (i // per_b, 0, 0)
    row_spec = lambda w: pl.BlockSpec((tm, w), lambda i: (i, 0))
    full = lambda s: pl.BlockSpec(s, lambda i: (0,) * len(s))
    folded = isinstance(a, tuple)
    if folded:
        assert per_b % 2 == 0
        half = per_b // 2
        a_inputs = (a[0], a[1], a[1])
        a_specs = [
            pl.BlockSpec((1, tm, d), lambda i: (i // per_b, jnp.minimum(i % per_b, half - 1), 0)),
            pl.BlockSpec((1, tm, d), lambda i: (i // per_b, jnp.clip(per_b - 1 - i % per_b, 0, half - 1), 0)),
            pl.BlockSpec((1, 8, d), lambda i: (i // per_b, (tm // 8) * jnp.clip(per_b - i % per_b, 1, half), 0)),
        ]
        scratch = [pltpu.VMEM((tm, d), BF16)]
    else:
        a_inputs = (a,)
        a_specs = [row_spec(d)]
        scratch = []
    return pl.pallas_call(
        functools.partial(_proj_moe_kernel, tiles_per_batch=per_b, folded=folded),
        grid=(nt,),
        in_specs=a_specs + [
            full((d, d)), row_spec(d),
            pl.BlockSpec((1, 1, d), mod_idx), full((1, d)),
            pl.BlockSpec((1, 1, d), mod_idx), pl.BlockSpec((1, 1, d), mod_idx),
            full((d, LANES)), full((d, 2 * LANES)), full((1, LANES)),
        ],
        out_specs=[row_spec(d), pl.BlockSpec((sr, d // 2), lambda i: (i, 0)), row_spec(LANES),
                   pl.BlockSpec((8, LANES), lambda i: (i, 0))],
        out_shape=[
            jax.ShapeDtypeStruct((t, d), F32), jax.ShapeDtypeStruct((nt * sr, d // 2), jnp.uint32),
            jax.ShapeDtypeStruct((t, LANES), F32), jax.ShapeDtypeStruct((nt * 8, LANES), jnp.int32),
        ],
        scratch_shapes=scratch,
        compiler_params=_params(("arbitrary",)),
        name="proj_moe",
    )(*a_inputs, w_o, x, g1, gain, shift, scale, wr_hi, wr_lo, b_r)


def _rows(ref, start_units, n_units):
    return ref.at[pl.ds(pl.multiple_of(start_units * RUN_ALIGN, RUN_ALIGN), n_units * RUN_ALIGN)]


def _ffn_kernel(exp_nblk, exp_blk0, blk_e, blk_b0, blk_n, blk_tlo, blk_thi, seg_start, run_start,
                nused_ref, s_hbm, w1_ref, w3_ref, w2_ref, y_hbm, xg, yo, w1b, w3b, w2b, sem, osem,
                *, sr_units, nblk):
    e_step = pl.program_id(0)
    nused = nused_ref[0]
    n_slots, bm, _ = xg.shape
    n_ahead = n_slots - 1
    bm_units = bm // RUN_ALIGN

    def gather(b, slot):
        e = blk_e[b]
        b0 = blk_b0[b]

        def body(ti, carry):
            rs = seg_start[ti * N_EXPERTS + e]
            re = seg_start[(ti + 1) * N_EXPERTS + e]
            lo = jnp.maximum(rs, b0)
            n = jnp.minimum(re, b0 + bm_units) - lo

            @pl.when(n > 0)
            def _():
                src = ti * sr_units + run_start[ti * N_EXPERTS + e] + (lo - rs)
                pltpu.make_async_copy(_rows(s_hbm, src, n), _rows(xg.at[slot], lo - b0, n),
                                      sem.at[slot]).start()

            return carry

        lax.fori_loop(blk_tlo[b], blk_thi[b], body, 0)

    def write_back(g, slot):
        return pltpu.make_async_copy(yo.at[slot], y_hbm.at[pl.ds(pl.multiple_of(g * bm, bm), bm)],
                                     osem.at[slot])

    def emit(g, fill):
        slot = g % 2

        @pl.when(g >= 2)
        def _():
            write_back(g - 2, slot).wait()

        fill(slot)
        write_back(g, slot).start()

    @pl.when(e_step == 0)
    def _():
        xg[...] = jnp.zeros_like(xg)
        for g0 in range(n_ahead):

            @pl.when(g0 < nused)
            def _():
                gather(g0, g0)

    w1b[...] = w1_ref[0, 0].astype(BF16)
    w3b[...] = w3_ref[0, 0].astype(BF16)
    w2b[...] = w2_ref[0, 0].astype(BF16)

    def block(jb, carry):
        g = exp_blk0[e_step] + jb

        @pl.when(g + n_ahead < nused)
        def _():
            gather(g + n_ahead, (g + n_ahead) % n_slots)

        def fill(slot):
            gslot = g % n_slots
            pltpu.make_async_copy(_rows(s_hbm, 0, blk_n[g]), _rows(xg.at[gslot], 0, blk_n[g]),
                                  sem.at[gslot]).wait()
            def mlp(rows):
                x_lo, x_hi = _unpack_halves(xg[gslot, 0:rows, :])
                half = x_lo.shape[1]
                a = _dot(x_lo, w1b[:half, :]) + _dot(x_hi, w1b[half:, :])
                b = _dot(x_lo, w3b[:half, :]) + _dot(x_hi, w3b[half:, :])
                mid = (a / (1.0 + jnp.exp(-a))) * b
                y = _dot(mid.astype(BF16), w2b[...])
                yo[slot, 0:rows, :] = _pack_halves(y.astype(BF16).astype(F32))

            short = blk_n[g] <= bm_units // 2

            @pl.when(jnp.logical_not(short))
            def _():
                mlp(bm)

            @pl.when(short)
            def _():
                mlp(bm // 2)
                yo[slot, bm // 2:bm, :] = jnp.zeros((bm - bm // 2, yo.shape[2]), yo.dtype)

        emit(g, fill)
        return carry

    lax.fori_loop(0, exp_nblk[e_step], block, 0)

    @pl.when(e_step == pl.num_programs(0) - 1)
    def _():
        def tail(g, carry):
            def fill(slot):
                yo[slot] = jnp.zeros(yo.shape[1:], yo.dtype)

            emit(g, fill)
            return carry

        lax.fori_loop(nused, nblk, tail, 0)
        write_back(nblk - 2, nblk % 2).wait()
        write_back(nblk - 1, (nblk - 1) % 2).wait()


def _ffn(tables, s_tiles, w1, w3, w2, *, layer, nblk, sr):
    bm = FFN_BLOCK
    d = w1.shape[2]
    de = w1.shape[3]
    nsp = len(tables)
    assert nblk >= 2
    grid_spec = pltpu.PrefetchScalarGridSpec(
        num_scalar_prefetch=nsp,
        grid=(N_EXPERTS,),
        in_specs=[
            pl.BlockSpec(memory_space=pl.ANY),
            pl.BlockSpec((1, 1, d, de), lambda e, *_: (layer, e, 0, 0)),
            pl.BlockSpec((1, 1, d, de), lambda e, *_: (layer, e, 0, 0)),
            pl.BlockSpec((1, 1, de, d), lambda e, *_: (layer, e, 0, 0)),
        ],
        out_specs=pl.BlockSpec(memory_space=pl.ANY),
        scratch_shapes=[pltpu.VMEM((FFN_GATHER_SLOTS, bm, d // 2), jnp.uint32),
                        pltpu.VMEM((2, bm, d // 2), jnp.uint32),
                        pltpu.VMEM((d, de), BF16), pltpu.VMEM((d, de), BF16), pltpu.VMEM((de, d), BF16),
                        pltpu.SemaphoreType.DMA((FFN_GATHER_SLOTS,)), pltpu.SemaphoreType.DMA((2,))],
    )
    return pl.pallas_call(
        functools.partial(_ffn_kernel, sr_units=sr // RUN_ALIGN, nblk=nblk),
        grid_spec=grid_spec,
        out_shape=jax.ShapeDtypeStruct((nblk * bm, d // 2), jnp.uint32),
        compiler_params=_params(("arbitrary",)),
        name="ffn",
    )(*tables, s_tiles, w1, w3, w2)


def _combine_kernel(run_len, run_start, seg_start, exp_start, tile_rows,
                    y_hbm, x_ref, g2_ref, gp_ref, fgain_ref, o_ref, ybuf, sem, *, nsteps, final):
    i = pl.program_id(0)

    def gather(ti, slot):
        for e in range(N_EXPERTS):
            n = run_len[ti * N_EXPERTS + e]

            @pl.when(n > 0)
            def _():
                src = exp_start[e] + seg_start[ti * N_EXPERTS + e]
                pltpu.make_async_copy(_rows(y_hbm, src, n),
                                      _rows(ybuf.at[slot], run_start[ti * N_EXPERTS + e], n),
                                      sem.at[slot]).start()

    @pl.when(i == 0)
    def _():
        ybuf[...] = jnp.zeros_like(ybuf)
        gather(0, 0)

    @pl.when(i + 1 < nsteps)
    def _():
        gather(i + 1, (i + 1) % 2)

    slot = i % 2
    pltpu.make_async_copy(_rows(y_hbm, 0, tile_rows[i]), _rows(ybuf.at[slot], 0, tile_rows[i]),
                          sem.at[slot]).wait()
    gp = gp_ref[...]
    tm = gp.shape[0]
    sr = ybuf.shape[1]
    col = lax.broadcasted_iota(jnp.int32, (tm, sr), 1).astype(F32)
    weights = (jnp.where(col == gp[:, 2:3], gp[:, 0:1], 0.0)
               + jnp.where(col == gp[:, 3:4], gp[:, 1:2], 0.0)).astype(BF16)
    y_lo, y_hi = _unpack_halves(ybuf[slot])
    y = jnp.concatenate([_dot(weights, y_lo), _dot(weights, y_hi)], axis=-1)
    x2 = x_ref[...] + g2_ref[0] * y
    if final:
        ms = jnp.mean(x2 * x2, axis=-1, keepdims=True)
        x2 = (x2 * lax.rsqrt(ms + EPS)) * fgain_ref[...]
    o_ref[...] = x2


def _combine(tables, yb, x1, g2, gp, fgain, *, tokens_per_batch, final):
    t, d = x1.shape
    tm = min(MOE_TILE, tokens_per_batch)
    nsteps = t // tm
    per_b = tokens_per_batch // tm
    sr = _sorted_rows(tm)
    grid_spec = pltpu.PrefetchScalarGridSpec(
        num_scalar_prefetch=len(tables),
        grid=(nsteps,),
        in_specs=[
            pl.BlockSpec(memory_space=pl.ANY),
            pl.BlockSpec((tm, d), lambda i, *_: (i, 0)),
            pl.BlockSpec((1, 1, d), lambda i, *_: (i // per_b, 0, 0)),
            pl.BlockSpec((tm, LANES), lambda i, *_: (i, 0)),
            pl.BlockSpec((1, d), lambda i, *_: (0, 0)),
        ],
        out_specs=pl.BlockSpec((tm, d), lambda i, *_: (i, 0)),
        scratch_shapes=[pltpu.VMEM((2, sr, d // 2), jnp.uint32), pltpu.SemaphoreType.DMA((2,))],
    )
    return pl.pallas_call(
        functools.partial(_combine_kernel, nsteps=nsteps, final=final),
        grid_spec=grid_spec,
        out_shape=jax.ShapeDtypeStruct((t, d), F32),
        compiler_params=_params(("arbitrary",)),
        name="combine_final" if final else "combine",
    )(*tables, yb, x1, g2, gp, fgain)


def _reversal(size):
    row = lax.broadcasted_iota(jnp.int32, (size, size), 0)
    col = lax.broadcasted_iota(jnp.int32, (size, size), 1)
    return jnp.where(col == size - row, 1.0, 0.0).astype(BF16)


def _fmix_fold_kernel(x_ref, xm_ref, xb_ref, xc_ref, sh_ref, sc_ref, gain_ref, cc_ref, sc_mat_ref,
                      a_ref, b_ref, amid_ref):
    mod = lambda v: _modulate(v, gain_ref[...], sh_ref[0], sc_ref[0])
    h = mod(x_ref[...])
    ft = h.shape[0]
    mirrored = mod(xm_ref[...]).astype(BF16)
    row = lax.broadcasted_iota(jnp.int32, h.shape, 0)
    hr = jnp.where(row == 0, mod(xb_ref[...])[0:1], _dot(_reversal(ft), mirrored))
    he = (h + hr).astype(BF16)
    ho = (h - hr).astype(BF16)
    hc = mod(xc_ref[...]).astype(BF16)
    gw = cc_ref.shape[0]
    for g in range(FOURIER_GROUPS):
        cols = slice(g * gw, (g + 1) * gw)
        a_ref[:, cols] = _dot(he[:, cols], cc_ref[...]).astype(BF16)
        b_ref[:, cols] = _dot(ho[:, cols], sc_mat_ref[...]).astype(BF16)
        amid_ref[:, cols] = _dot(hc[:, cols], cc_ref[...])


def _fmix_fold(x, shift, scale, gain, cc, sc, *, bsz, n):
    t, d = x.shape
    hn = n // 2
    ft = min(512, hn)
    tpb = n // ft
    half = tpb // 2
    gw = d // FOURIER_GROUPS
    mod_idx = lambda b, j: (b, 0, 0)
    return pl.pallas_call(
        _fmix_fold_kernel,
        grid=(bsz, half),
        in_specs=[
            pl.BlockSpec((ft, d), lambda b, j: (b * tpb + j, 0)),
            pl.BlockSpec((ft, d), lambda b, j: (b * tpb + tpb - 1 - j, 0)),
            pl.BlockSpec((8, d), lambda b, j: (b * (n // 8) + (ft // 8) * ((tpb - j) % tpb), 0)),
            pl.BlockSpec((8, d), lambda b, j: (b * (n // 8) + hn // 8, 0)),
            pl.BlockSpec((1, 1, d), mod_idx), pl.BlockSpec((1, 1, d), mod_idx),
            pl.BlockSpec((1, d), lambda b, j: (0, 0)),
            pl.BlockSpec((gw, gw), lambda b, j: (0, 0)), pl.BlockSpec((gw, gw), lambda b, j: (0, 0)),
        ],
        out_specs=[pl.BlockSpec((ft, d), lambda b, j: (b * half + j, 0)),
                   pl.BlockSpec((ft, d), lambda b, j: (b * half + j, 0)),
                   pl.BlockSpec((8, d), lambda b, j: (b, 0))],
        out_shape=[jax.ShapeDtypeStruct((bsz * hn, d), BF16), jax.ShapeDtypeStruct((bsz * hn, d), BF16),
                   jax.ShapeDtypeStruct((bsz * 8, d), F32)],
        compiler_params=_params(("arbitrary", "arbitrary")),
        name="fmix_fold",
    )(x, x, x, x, shift, scale, gain, cc, sc)


def _fmix_pos_kernel(ch_ref, sh_ref, sign_ref, a_ref, b_ref, amid_ref, dif_ref, sum_ref):
    p = _dot(ch_ref[...], a_ref[0]) + sign_ref[...] * amid_ref[0, 0:1, :]
    q = _dot(sh_ref[...], b_ref[0])
    dif_ref[0] = (p - q).astype(BF16)
    sum_ref[0] = (p + q).astype(BF16)


def _fmix_pos(ch, sh, sign, a, b, amid):
    bsz, hn, d = a.shape
    mp = ch.shape[0]
    tmr = mp // DFT_ROW_BLOCKS
    assert tmr % 16 == 0
    out = jax.ShapeDtypeStruct((bsz, mp, d), BF16)
    return pl.pallas_call(
        _fmix_pos_kernel,
        grid=(bsz, DFT_ROW_BLOCKS),
        in_specs=[
            pl.BlockSpec((tmr, hn), lambda bi, i: (i, 0)),
            pl.BlockSpec((tmr, hn), lambda bi, i: (i, 0)),
            pl.BlockSpec((tmr, 1), lambda bi, i: (i, 0)),
            pl.BlockSpec((1, hn, d), lambda bi, i: (bi, 0, 0)),
            pl.BlockSpec((1, hn, d), lambda bi, i: (bi, 0, 0)),
            pl.BlockSpec((1, 8, d), lambda bi, i: (bi, 0, 0)),
        ],
        out_specs=[pl.BlockSpec((1, tmr, d), lambda bi, i: (bi, i, 0)),
                   pl.BlockSpec((1, tmr, d), lambda bi, i: (bi, i, 0))],
        out_shape=[out, out],
        compiler_params=_params(("arbitrary", "arbitrary")),
        name="fmix_pos",
    )(ch, sh, sign, a, b, amid)


def _rope_tables(n):
    half = ROPE_AXIS_DIM // 2
    rows = n // GRID_W
    inv = ROPE_THETA ** (-jnp.arange(half, dtype=F32) / half)
    pos = jnp.arange(n, dtype=jnp.int32)
    ang_r = (pos // GRID_W).astype(F32)[:, None] * inv
    ang_c = (pos % GRID_W).astype(F32)[:, None] * inv
    del rows
    cos = jnp.concatenate([jnp.cos(ang_r)] * 2 + [jnp.cos(ang_c)] * 2, axis=-1)
    sin = jnp.concatenate([-jnp.sin(ang_r), jnp.sin(ang_r), -jnp.sin(ang_c), jnp.sin(ang_c)], axis=-1)
    return jnp.tile(cos, (1, LANES // HEAD_DIM)), jnp.tile(sin, (1, LANES // HEAD_DIM))


def _dft_mats(n, scale):
    r = 1
    while r * r < n:
        r *= 2
    c = n // r
    k = jnp.arange(n, dtype=jnp.int32)[:, None]
    ang_a = ((k * jnp.arange(r, dtype=jnp.int32)[None, :] * c) % n).astype(F32) * (2.0 * math.pi / n)
    ang_b = ((k * jnp.arange(c, dtype=jnp.int32)[None, :]) % n).astype(F32) * (2.0 * math.pi / n)
    ca, sa, cb, sb = jnp.cos(ang_a), jnp.sin(ang_a), jnp.cos(ang_b), jnp.sin(ang_b)
    cosm = (ca[:, :, None] * cb[:, None, :] - sa[:, :, None] * sb[:, None, :]).reshape(n, n)
    sinm = (sa[:, :, None] * cb[:, None, :] + ca[:, :, None] * sb[:, None, :]).reshape(n, n)
    return (cosm * scale).astype(BF16), (sinm * scale).astype(BF16)


DFT_ROW_BLOCKS = 4
DFT_ROW_PAD = 64


def _dft_half(n):
    hn = n // 2
    mp = hn + DFT_ROW_PAD
    kb = 32
    ka = mp // kb
    assert ka * kb == mp
    m = jnp.arange(hn, dtype=jnp.int32)[None, :]
    w = 2.0 * math.pi / n
    ang_a = (((jnp.arange(ka, dtype=jnp.int32)[:, None] * kb) * m) % n).astype(F32) * w
    ang_b = ((jnp.arange(kb, dtype=jnp.int32)[:, None] * m) % n).astype(F32) * w
    ca, sa, cb, sb = jnp.cos(ang_a), jnp.sin(ang_a), jnp.cos(ang_b), jnp.sin(ang_b)
    cosm = (ca[:, None, :] * cb[None, :, :] - sa[:, None, :] * sb[None, :, :]).reshape(mp, hn)
    sinm = (sa[:, None, :] * cb[None, :, :] + ca[:, None, :] * sb[None, :, :]).reshape(mp, hn)
    k = jnp.arange(mp, dtype=jnp.int32)[:, None]
    valid = k <= hn
    scale = n ** -0.5
    col_w = jnp.where(m == 0, 0.5 * scale, scale)
    cosm = jnp.where(valid, cosm * col_w, 0.0).astype(BF16)
    sinm = jnp.where(valid, sinm * scale, 0.0).astype(BF16)
    sign = jnp.where(valid, jnp.where(k % 2 == 0, scale, -scale), 0.0).astype(F32)
    return cosm, sinm, sign


def _moe(a, w_o, x, g1, g2, gain, shift, scale, w_rg, b_rg, w_re, b_re, w1, w3, w2, fgain, *,
         layer, tokens_per_batch, final):
    t, d = x.shape
    pad = LANES - N_GROUPS - N_EXPERTS
    w_r = jnp.concatenate([w_rg, w_re, jnp.zeros((d, pad), F32)], axis=1)
    b_r = jnp.concatenate([b_rg, b_re, jnp.zeros((pad,), F32)])[None, :]
    wr_hi = w_r.astype(BF16)
    wr_lo = jnp.concatenate([wr_hi, (w_r - wr_hi.astype(F32)).astype(BF16)], axis=1)
    x1, s_tiles, gp, pc = _proj_moe(a, w_o, x, g1, gain, shift, scale, wr_hi, wr_lo, b_r,
                                    tokens_per_batch=tokens_per_batch)
    tm = min(MOE_TILE, tokens_per_batch)
    nt = t // tm
    sr = _sorted_rows(tm)
    bmu = FFN_BLOCK // RUN_ALIGN
    i32 = jnp.int32
    run_len = pc.reshape(nt, 8, LANES)[:, 0, :N_EXPERTS]
    seg_start = jnp.concatenate([jnp.zeros((1, N_EXPERTS), i32), jnp.cumsum(run_len, axis=0)])
    seg_len = seg_start[-1]
    seg_pad = (seg_len + bmu - 1) // bmu * bmu
    exp_end = jnp.cumsum(seg_pad)
    exp_start = exp_end - seg_pad
    run_start = jnp.cumsum(run_len, axis=1) - run_len
    tile_rows = run_start[:, -1] + run_len[:, -1]
    max_units = (2 * t) // RUN_ALIGN + nt * N_EXPERTS + N_EXPERTS * (bmu - 1)
    nblk = (max_units + bmu - 1) // bmu
    blk_start = jnp.arange(nblk, dtype=i32) * bmu
    blk_e = jnp.minimum(jnp.sum((exp_end[None, :] <= blk_start[:, None]).astype(i32), axis=1),
                        N_EXPERTS - 1)
    pick = (blk_e[:, None] == jnp.arange(N_EXPERTS, dtype=i32)[None, :]).astype(i32)
    blk_b0 = blk_start - jnp.sum(pick * exp_start[None, :], axis=1)
    blk_n = jnp.clip(jnp.sum(pick * seg_len[None, :], axis=1) - blk_b0, 0, bmu)
    seg_e = jnp.sum(pick[:, None, :] * seg_start[None, :, :], axis=2)
    blk_tlo = jnp.sum((seg_e[:, 1:] <= blk_b0[:, None]).astype(i32), axis=1)
    blk_thi = jnp.sum((seg_e[:, :-1] < (blk_b0 + bmu)[:, None]).astype(i32), axis=1)
    nused = (exp_end[-1] // bmu)[None]
    flat = lambda v: v.reshape(-1).astype(i32)
    yb = _ffn((seg_pad // bmu, exp_start // bmu, blk_e, blk_b0, blk_n, blk_tlo, blk_thi,
               flat(seg_start), flat(run_start), nused),
              s_tiles, w1, w3, w2, layer=layer, nblk=nblk, sr=sr)
    return _combine((flat(run_len), flat(run_start), flat(seg_start), exp_start, tile_rows),
                    yb, x1, g2, gp, fgain, tokens_per_batch=tokens_per_batch, final=final)


def kernel(x, c, ctx, c_ctx, w_mod, b_mod, norm_mix, norm_ffn, attn_w_qkv, attn_q_norm, attn_k_norm,
           attn_w_o, fourier_w_o, moe_w_rg, moe_b_rg, moe_w_re, moe_b_re, moe_w1, moe_w3, moe_w2,
           final_norm):
    bsz, n, d = x.shape
    t = bsz * n
    q_dim = N_HEADS * HEAD_DIM
    kv_dim = N_KV_HEADS * HEAD_DIM

    cond = jnp.zeros((16, d), F32).at[:bsz].set(c).at[bsz].set(c_ctx)
    mods = _ada(cond, w_mod, b_mod)

    def lat(layer, j):
        return mods[layer, :bsz, j * d:(j + 1) * d][:, None, :]

    def ctxm(layer, j):
        return mods[layer, bsz:bsz + 1, j * d:(j + 1) * d][:, None, :]

    w_qkv = attn_w_qkv[0].astype(BF16)
    qscale = (HEAD_DIM ** -0.5) * math.log2(math.e)
    head_gain = jnp.stack([jnp.tile(attn_q_norm[0] * qscale, LANES // HEAD_DIM),
                           jnp.tile(attn_k_norm[0], LANES // HEAD_DIM)])
    lane = jnp.arange(LANES)
    ebd = jnp.where((lane[:, None] // HEAD_DIM) == (lane[None, :] // HEAD_DIM),
                    1.0 / HEAD_DIM, 0.0).astype(BF16)
    cos, sin = _rope_tables(n)
    gain0 = norm_mix[0][None, :]
    q, k_l, v_l = _qkv(x, lat(0, 0), lat(0, 1), gain0, w_qkv, head_gain, ebd, cos, sin,
                       with_q=True, per_batch_mods=True)
    n_ctx = ctx.shape[1]
    k_c, v_c = _qkv(ctx, ctxm(0, 0), ctxm(0, 1), gain0, w_qkv[:, q_dim:], head_gain, ebd,
                    jnp.ones((n_ctx, LANES), F32), jnp.zeros((n_ctx, LANES), F32),
                    with_q=False, per_batch_mods=False)
    del kv_dim
    o = _attention(q, k_l, v_l, k_c, v_c).reshape(t, q_dim)

    xf = x.reshape(t, d)
    moe_w = lambda i: (moe_w_rg[i], moe_b_rg[i], moe_w_re[i], moe_b_re[i],
                       moe_w1, moe_w3, moe_w2)
    fgain = final_norm[None, :]
    x2 = _moe(o, attn_w_o[0].astype(BF16), xf, lat(0, 2), lat(0, 5), norm_ffn[0][None, :],
              lat(0, 3), lat(0, 4), *moe_w(0), fgain, layer=0, tokens_per_batch=n, final=False)

    gw = d // FOURIER_GROUPS
    cc, sc = _dft_mats(gw, gw ** -0.5)
    ch, sh, sign = _dft_half(n)
    a_e, b_o, a_mid = _fmix_fold(x2, lat(1, 0), lat(1, 1), norm_mix[1][None, :], cc, sc, bsz=bsz, n=n)
    f = tuple(_fmix_pos(ch, sh, sign, a_e.reshape(bsz, n // 2, d), b_o.reshape(bsz, n // 2, d),
                        a_mid.reshape(bsz, 8, d)))
    out = _moe(f, fourier_w_o[0].astype(BF16), x2, lat(1, 2), lat(1, 5), norm_ffn[1][None, :],
               lat(1, 3), lat(1, 4), *moe_w(1), fgain, layer=1, tokens_per_batch=n, final=True)
    return out.reshape(bsz, n, d)
```
